```python
import math
import jax, jax.numpy as jnp
from jax import lax
import numpy as np

D_MODEL = 1024
BATCH = 8
SEQ = 2048
DEPTH = 1
DEC_BATCH = 128
DEC_SEQ = 8
PAST_LEN = 16384
PAGE_SIZE = 128

D_MIX = D_MODEL
D_A = D_MIX // 2
D_B = D_MIX - D_A
HEAD_DIM = 128
N_HEADS_A = D_A // HEAD_DIM
CONV_W = 4
CHUNK = 64
POOL_WINDOWS = (2, 4, 8, 16)
N_POOL_GROUPS = len(POOL_WINDOWS)
POOL_GROUP_DIM = D_B // N_POOL_GROUPS
POOL_BUF = max(POOL_WINDOWS) - 1
N_EXPERT_GROUPS = 4
EXPERTS_PER_GROUP = 8
TOP_K_FINE = 2
D_EXPERT = D_MODEL // 4
ALPHA = (2.0 * DEPTH) ** 0.25
BETA_INIT = (8.0 * DEPTH) ** -0.25
LN_EPS = 1e-5
RMS_EPS = 1e-6
L2_EPS = 1e-6
COL_Q = 0
COL_K = D_A
COL_V = 2 * D_A
COL_Z = 3 * D_A
COL_B = 4 * D_A
COL_A = 4 * D_A + N_HEADS_A
COL_P = 4 * D_A + 2 * N_HEADS_A
D_IN_PROJ = COL_P + D_B

kernel_name = "hymba_gdn_pool_hiermoe_deepnorm_step"


def layer_norm(x, g, b):
    xf = x.astype(jnp.float32)
    mu = jnp.mean(xf, -1, keepdims=True)
    var = jnp.mean(jnp.square(xf - mu), -1, keepdims=True)
    y = (xf - mu) * lax.rsqrt(var + LN_EPS) * g.astype(jnp.float32) + b.astype(jnp.float32)
    return y.astype(x.dtype)


def l2norm(x):
    return x * lax.rsqrt(jnp.sum(x * x, -1, keepdims=True) + L2_EPS)


def causal_conv(u, buf, w):
    L = u.shape[1]
    xp = jnp.concatenate([buf.astype(u.dtype), u], axis=1)
    y = sum(xp[:, j:j + L] * w[j] for j in range(CONV_W))
    return jax.nn.silu(y), xp[:, -(CONV_W - 1):]


def gated_delta_rule(q, k, v, g, beta, s0):
    B, L, H, DK = q.shape
    DV = v.shape[-1]
    pad = (-L) % CHUNK
    padt = lambda a: jnp.pad(a, [(0, 0), (0, pad)] + [(0, 0)] * (a.ndim - 2))
    q, k, v, g, beta = (padt(a) for a in (q, k, v, g, beta))
    n = (L + pad) // CHUNK

    def chunks(a):
        a = a.reshape((B, n, CHUNK) + a.shape[2:])
        return jnp.moveaxis(a, (1, 3), (0, 2))

    qc, kc, vc, gc, bc = (chunks(a) for a in (q, k, v, g, beta))
    gc = jnp.cumsum(gc, axis=-1)
    kb = kc * bc[..., None]
    vb = vc * bc[..., None]
    idx = jnp.arange(CHUNK)
    incl = idx[:, None] >= idx[None, :]
    strict = idx[:, None] > idx[None, :]
    diff = gc[..., :, None] - gc[..., None, :]
    decay = jnp.exp(jnp.where(incl, diff, -jnp.inf))
    a_strict = jnp.where(strict, jnp.einsum('nbhid,nbhjd->nbhij', kb, kc) * decay, 0.0)
    m = jnp.eye(CHUNK, dtype=jnp.float32) + a_strict
    rhs = jnp.concatenate([vb, kb * jnp.exp(gc)[..., None]], axis=-1)
    sol = lax.linalg.triangular_solve(m, rhs, left_side=True, lower=True, unit_diagonal=True)
    u_c = sol[..., :DV]
    w_c = sol[..., DV:]
    qk = jnp.einsum('nbhid,nbhjd->nbhij', qc, kc) * decay

    def step(S, xs):
        q_i, k_i, u_i, w_i, qk_i, g_i = xs
        v_new = u_i - jnp.einsum('bhcd,bhde->bhce', w_i, S)
        o = (jnp.einsum('bhcd,bhde->bhce', q_i * jnp.exp(g_i)[..., None], S)
             + jnp.einsum('bhij,bhje->bhie', qk_i, v_new))
        g_last = g_i[..., -1]
        S = (S * jnp.exp(g_last)[..., None, None]
             + jnp.einsum('bhcd,bhce->bhde', k_i * jnp.exp(g_last[..., None] - g_i)[..., None], v_new))
        return S, o

    s_fin, o = lax.scan(step, s0, (qc, kc, u_c, w_c, qk, gc))
    o = jnp.moveaxis(o, (0, 2), (1, 3)).reshape(B, n * CHUNK, H, DV)[:, :L]
    return o, s_fin


def gdn_mixer(proj, conv_buf, s0, conv_w, a_log, dt_bias, norm_w):
    B, L, _ = proj.shape
    qkv, conv_new = causal_conv(proj[..., COL_Q:COL_Q + 3 * D_A], conv_buf, conv_w)
    qkv = qkv.astype(jnp.float32).reshape(B, L, 3, N_HEADS_A, HEAD_DIM)
    q = l2norm(qkv[:, :, 0]) * (HEAD_DIM ** -0.5)
    k = l2norm(qkv[:, :, 1])
    v = qkv[:, :, 2]
    z = proj[..., COL_Z:COL_Z + D_A].astype(jnp.float32).reshape(B, L, N_HEADS_A, HEAD_DIM)
    beta = jax.nn.sigmoid(proj[..., COL_B:COL_B + N_HEADS_A].astype(jnp.float32))
    a = proj[..., COL_A:COL_A + N_HEADS_A].astype(jnp.float32)
    g = -jnp.exp(a_log.astype(jnp.float32)) * jax.nn.softplus(a + dt_bias.astype(jnp.float32))
    o, s_new = gated_delta_rule(q, k, v, g, beta, s0.astype(jnp.float32))
    o = o * lax.rsqrt(jnp.mean(o * o, -1, keepdims=True) + RMS_EPS) * norm_w.astype(jnp.float32)
    o = o * jax.nn.silu(z)
    return o.reshape(B, L, D_A).astype(proj.dtype), conv_new, s_new


def pool_mixer(u, buf, start, pool_w, pool_scale):
    B, L, _ = u.shape
    xp = jnp.concatenate([buf.astype(u.dtype), u], axis=1)
    c = jnp.cumsum(xp.astype(jnp.float32), axis=1)
    c = jnp.concatenate([jnp.zeros((B, 1, D_B), jnp.float32), c], axis=1)
    pos = start + jnp.arange(L)
    means = []
    for gi, w in enumerate(POOL_WINDOWS):
        sl = slice(gi * POOL_GROUP_DIM, (gi + 1) * POOL_GROUP_DIM)
        s = c[:, POOL_BUF + 1:POOL_BUF + 1 + L, sl] - c[:, POOL_BUF + 1 - w:POOL_BUF + 1 - w + L, sl]
        cnt = jnp.minimum(pos + 1, w).astype(jnp.float32)
        means.append(s / cnt[None, :, None])
    d = (jnp.concatenate(means, -1) - u.astype(jnp.float32)).astype(u.dtype)
    d = d.reshape(B, L, N_POOL_GROUPS, POOL_GROUP_DIM)
    y = jnp.einsum('blgc,gcd->blgd', d, pool_w).reshape(B, L, D_B) * pool_scale
    return y, xp[:, -POOL_BUF:]


def hier_moe(h, w_rg, b_rg, w_re, b_re, w_gate, w_up, w_down):
    B, L, _ = h.shape
    lg = (h @ w_rg).astype(jnp.float32) + b_rg.astype(jnp.float32)
    pg = jax.nn.softmax(lg, axis=-1)
    onehot_g = jax.nn.one_hot(jnp.argmax(lg, axis=-1), N_EXPERT_GROUPS, dtype=jnp.float32)
    pg_sel = jnp.sum(pg * onehot_g, -1)
    le = ((h @ w_re).astype(jnp.float32) + b_re.astype(jnp.float32)).reshape(
        B, L, N_EXPERT_GROUPS, EXPERTS_PER_GROUP)
    le_sel = jnp.einsum('blge,blg->ble', le, onehot_g)
    top_v, top_i = lax.top_k(le_sel, TOP_K_FINE)
    wk = jax.nn.softmax(top_v, axis=-1) * pg_sel[..., None]
    w_fine = jnp.einsum('blk,blke->ble', wk, jax.nn.one_hot(top_i, EXPERTS_PER_GROUP, dtype=jnp.float32))
    gate = (onehot_g[..., :, None] * w_fine[..., None, :]).astype(h.dtype)
    y = jnp.zeros_like(h)
    for gi in range(N_EXPERT_GROUPS):
        a = jnp.einsum('bld,edf->blef', h, w_gate[gi])
        b = jnp.einsum('bld,edf->blef', h, w_up[gi])
        act = jax.nn.silu(a) * b * gate[:, :, gi, :, None]
        y = y + jnp.einsum('blef,efd->bld', act, w_down[gi])
    return y


def decoder_layer(x, delta0, conv0, pool0, start, w_in, conv_w, a_log, dt_bias, gdn_norm_w,
                  pool_w, pool_scale, w_out, ln1_g, ln1_b, w_rg, b_rg, w_re, b_re,
                  w_gate, w_up, w_down, ln2_g, ln2_b):
    proj = x @ w_in
    ya, conv_new, delta_new = gdn_mixer(proj, conv0, delta0, conv_w, a_log, dt_bias, gdn_norm_w)
    yb, pool_new = pool_mixer(proj[..., COL_P:COL_P + D_B], pool0, start, pool_w, pool_scale)
    mix = jnp.concatenate([ya, yb], axis=-1) @ w_out
    h = layer_norm(ALPHA * x + mix, ln1_g, ln1_b)
    y = layer_norm(ALPHA * h + hier_moe(h, w_rg, b_rg, w_re, b_re, w_gate, w_up, w_down), ln2_g, ln2_b)
    return y, delta_new, conv_new, pool_new


def setup_inputs(seed: int = 0) -> dict:
    key = jax.random.key(seed)
    ks = jax.random.split(key, 24)
    nrm = lambda k, s, sc: jax.random.normal(k, s, jnp.float32) * sc
    G, E = N_EXPERT_GROUPS, EXPERTS_PER_GROUP
    dt = jnp.exp(jax.random.uniform(ks[6], (DEPTH, N_HEADS_A), minval=math.log(1e-3), maxval=math.log(1e-1)))
    return {
        "x_prompt": nrm(ks[0], (BATCH, SEQ, D_MODEL), 1.0),
        "x_sample": nrm(ks[1], (DEC_BATCH, DEC_SEQ, D_MODEL), 1.0),
        "state_delta": nrm(ks[2], (DEPTH, DEC_BATCH, N_HEADS_A, HEAD_DIM, HEAD_DIM), 0.5),
        "state_conv": nrm(ks[3], (DEPTH, DEC_BATCH, CONV_W - 1, 3 * D_A), 1.0),
        "state_pool": nrm(ks[4], (DEPTH, DEC_BATCH, POOL_BUF, D_B), 1.0),
        "w_in": nrm(ks[5], (DEPTH, D_MODEL, D_IN_PROJ), D_MODEL ** -0.5),
        "conv_w": nrm(ks[7], (DEPTH, CONV_W, 3 * D_A), CONV_W ** -0.5),
        "a_log": jnp.log(jax.random.uniform(ks[8], (DEPTH, N_HEADS_A), minval=1.0, maxval=16.0)),
        "dt_bias": dt + jnp.log(-jnp.expm1(-dt)),
        "gdn_norm_w": 1.0 + nrm(ks[9], (DEPTH, HEAD_DIM), 0.02),
        "pool_w": nrm(ks[10], (DEPTH, N_POOL_GROUPS, POOL_GROUP_DIM, POOL_GROUP_DIM), POOL_GROUP_DIM ** -0.5),
        "pool_scale": 1.0 + nrm(ks[11], (DEPTH, D_B), 0.02),
        "w_out": nrm(ks[12], (DEPTH, D_MIX, D_MODEL), BETA_INIT * D_MIX ** -0.5),
        "ln1_g": 1.0 + nrm(ks[13], (DEPTH, D_MODEL), 0.02),
        "ln1_b": nrm(ks[14], (DEPTH, D_MODEL), 0.02),
        "w_rg": nrm(ks[15], (DEPTH, D_MODEL, G), D_MODEL ** -0.5),
        "b_rg": nrm(ks[16], (DEPTH, G), 0.01),
        "w_re": nrm(ks[17], (DEPTH, D_MODEL, G * E), D_MODEL ** -0.5),
        "b_re": nrm(ks[18], (DEPTH, G * E), 0.01),
        "w_gate": nrm(ks[19], (DEPTH, G, E, D_MODEL, D_EXPERT), D_MODEL ** -0.5),
        "w_up": nrm(ks[20], (DEPTH, G, E, D_MODEL, D_EXPERT), D_MODEL ** -0.5),
        "w_down": nrm(ks[21], (DEPTH, G, E, D_EXPERT, D_MODEL), BETA_INIT * D_EXPERT ** -0.5),
        "ln2_g": 1.0 + nrm(ks[22], (DEPTH, D_MODEL), 0.02),
        "ln2_b": nrm(ks[23], (DEPTH, D_MODEL), 0.02),
    }


def reference(x_prompt, x_sample, state_delta, state_conv, state_pool, w_in, conv_w, a_log, dt_bias,
              gdn_norm_w, pool_w, pool_scale, w_out, ln1_g, ln1_b, w_rg, b_rg, w_re, b_re,
              w_gate, w_up, w_down, ln2_g, ln2_b):
    yp, ys = x_prompt, x_sample
    dp_l, cp_l, pp_l, ds_l, cs_l, ps_l = [], [], [], [], [], []
    for l in range(DEPTH):
        wts = (w_in[l], conv_w[l], a_log[l], dt_bias[l], gdn_norm_w[l], pool_w[l], pool_scale[l],
               w_out[l], ln1_g[l], ln1_b[l], w_rg[l], b_rg[l], w_re[l], b_re[l],
               w_gate[l], w_up[l], w_down[l], ln2_g[l], ln2_b[l])
        d0 = jnp.zeros((BATCH, N_HEADS_A, HEAD_DIM, HEAD_DIM), jnp.float32)
        c0 = jnp.zeros((BATCH, CONV_W - 1, 3 * D_A), x_prompt.dtype)
        p0 = jnp.zeros((BATCH, POOL_BUF, D_B), x_prompt.dtype)
        yp, dp, cp, pp = decoder_layer(yp, d0, c0, p0, 0, *wts)
        ys, ds, cs, ps = decoder_layer(ys, state_delta[l], state_conv[l], state_pool[l], PAST_LEN, *wts)
        dp_l.append(dp.astype(state_delta.dtype)); cp_l.append(cp.astype(state_conv.dtype))
        pp_l.append(pp.astype(state_pool.dtype)); ds_l.append(ds.astype(state_delta.dtype))
        cs_l.append(cs.astype(state_conv.dtype)); ps_l.append(ps.astype(state_pool.dtype))
    delta_prompt = jnp.stack(dp_l)
    conv_prompt = jnp.stack(cp_l)
    pool_prompt = jnp.stack(pp_l)
    delta_sample = jnp.stack(ds_l)
    conv_sample = jnp.stack(cs_l)
    pool_sample = jnp.stack(ps_l)
    return (yp, ys, delta_prompt, conv_prompt, pool_prompt, delta_sample, conv_sample, pool_sample)
```

```python
import functools

import jax
import jax.numpy as jnp
from jax import lax
from jax.experimental import pallas as pl
from jax.experimental.pallas import tpu as pltpu

F32 = jnp.float32
BF16 = jnp.bfloat16

D_MODEL = 1024
D_A = 512
D_B = 512
HEAD_DIM = 128
N_HEADS = 4
CONV_W = 4
CHUNK_SHIFT = 6
POOL_WINDOWS = (2, 4, 8, 16)
POOL_BUF = 15
N_GROUPS = 4
E_PER_GROUP = 8
N_EXPERTS = N_GROUPS * E_PER_GROUP
D_EXPERT = 256
ALPHA = 2.0 ** 0.25
LN_EPS = 1e-5
RMS_EPS = 1e-6
L2_EPS = 1e-6

C_QKV = 3 * D_A
C_Z = 3 * D_A
C_P = 4 * D_A
C_BA = 4 * D_A + D_B
C_TOT = C_BA + 128
LANE_B = 0
LANE_A = N_HEADS

VMEM_LIMIT = 56 * 1024 * 1024


def _dot(a, b):
    return jnp.dot(a.astype(BF16), b.astype(BF16), preferred_element_type=F32)


def _dot_nt(a, b):
    return lax.dot_general(a.astype(BF16), b.astype(BF16), (((1,), (1,)), ((), ())), preferred_element_type=F32)


def _split3(x):
    hi = x.astype(BF16)
    r = x - hi.astype(F32)
    mid = r.astype(BF16)
    lo = (r - mid.astype(F32)).astype(BF16)
    return hi, mid, lo


def _dot01(m01, x):
    hi, mid, lo = _split3(x)
    f = lambda p: jnp.dot(m01, p, preferred_element_type=F32)
    return f(hi) + f(mid) + f(lo)


def _silu(x):
    return x * jax.nn.sigmoid(x)


def _softplus(x):
    return jnp.maximum(x, 0.0) + jnp.log1p(jnp.exp(-jnp.abs(x)))


def _iota2(n, m):
    return lax.broadcasted_iota(jnp.int32, (n, m), 0), lax.broadcasted_iota(jnp.int32, (n, m), 1)


def _proj_kernel(x_ref, w_ref, o_ref):
    o_ref[...] = jnp.dot(x_ref[...].astype(BF16), w_ref[...], preferred_element_type=F32)


def _in_proj(x2d, w_cat, tm):
    t = x2d.shape[0]
    return pl.pallas_call(
        _proj_kernel,
        out_shape=jax.ShapeDtypeStruct((t, C_TOT), F32),
        grid=(t // tm,),
        in_specs=[pl.BlockSpec((tm, D_MODEL), lambda i: (i, 0)),
                  pl.BlockSpec((D_MODEL, C_TOT), lambda i: (0, 0))],
        out_specs=pl.BlockSpec((tm, C_TOT), lambda i: (i, 0)),
        compiler_params=pltpu.CompilerParams(dimension_semantics=("parallel",), vmem_limit_bytes=VMEM_LIMIT),
        name="in_proj",
    )(x2d, w_cat)


def _unit_lower_inverse(a, r, c, chunk_shift):
    b0 = min(4, chunk_shift)
    eye = jnp.where(r == c, 1.0, 0.0).astype(F32)
    x = jnp.where((r >> b0) == (c >> b0), a, 0.0)
    t = eye - x
    xs = x
    for _ in range(b0 - 1):
        xs = _dot(xs, xs)
        t = t + _dot(t, xs)
    for lvl in range(b0, chunk_shift):
        off = jnp.where(((r >> (lvl + 1)) == (c >> (lvl + 1))) & ((r >> lvl) != (c >> lvl)), a, 0.0)
        t = t - _dot(_dot(t, off), t)
    return t


def _gate_slabs(ba, arow, dtrow, chunk_shift):
    n = ba.shape[0]
    beta = jax.nn.sigmoid(ba)
    g = -jnp.exp(arow) * _softplus(ba + dtrow)
    r, c = _iota2(n, n)
    same = (r >> chunk_shift) == (c >> chunk_shift)
    ltri = jnp.where(same & (r >= c), 1.0, 0.0).astype(BF16)
    lall = jnp.where(same, 1.0, 0.0).astype(BF16)
    cs = _dot01(jnp.concatenate([ltri, lall], axis=0), g)
    return beta, cs[:n], cs[n:]


def _head_prepare(q, k, v, beta_c, gc_c, gc_r, egc_c, chunk_shift):
    n = q.shape[0]
    r, c = _iota2(n, n)
    same = (r >> chunk_shift) == (c >> chunk_shift)
    decay = jnp.exp(jnp.where(same & (r >= c), gc_c - gc_r, -jnp.inf))
    kb = k * beta_c
    vb = v * beta_c
    a = jnp.where(same & (r > c), _dot_nt(kb, k) * decay, 0.0)
    t = _unit_lower_inverse(a, r, c, chunk_shift)
    sol = _dot(t, jnp.concatenate([vb, kb * egc_c], axis=1))
    return sol[:, :HEAD_DIM], sol[:, HEAD_DIM:], _dot_nt(q, k) * decay


def _l2norm(x):
    return x * lax.rsqrt(jnp.sum(x * x, axis=-1, keepdims=True) + L2_EPS)


def _gated_rmsnorm(o, z, normw):
    o = o * lax.rsqrt(jnp.mean(o * o, axis=-1, keepdims=True) + RMS_EPS) * normw
    return o * _silu(z)


def _pool_out(s, cnt, p_g, poolw_g, pscale_g):
    d = s / cnt - p_g
    return _dot(d, poolw_g) * pscale_g


def _mixer_prompt_kernel(proj_ref, convw_ref, arow_ref, dtrow_ref, normw_ref, poolw_ref, pscale_ref,
                         mix_ref, sfin_ref, cc_ref, pc_ref, s_ref, *, lb):
    l = pl.program_id(1)
    n = lb
    csz = 1 << CHUNK_SHIFT

    @pl.when(l == 0)
    def _init():
        cc_ref[...] = jnp.zeros_like(cc_ref)
        pc_ref[...] = jnp.zeros_like(pc_ref)
        s_ref[...] = jnp.zeros_like(s_ref)

    u = proj_ref[0, :, 0:C_QKV]
    ext = jnp.concatenate([cc_ref[...], u], axis=0)
    cw = convw_ref[...]
    acc = ext * cw[CONV_W - 1:CONV_W, :]
    for d in range(1, CONV_W):
        acc = acc + pltpu.roll(ext, d, 0) * cw[CONV_W - 1 - d:CONV_W - d, :]
    cc_ref[...] = u[n - 8:n, :]
    y = _silu(acc[8:, :])

    beta_s, gc_s, gl_s = _gate_slabs(proj_ref[0, :, C_BA:C_TOT], arow_ref[...], dtrow_ref[...], CHUNK_SHIFT)
    egc_s = jnp.exp(gc_s)
    ekg_s = jnp.exp(gl_s - gc_s)
    egl_s = jnp.exp(gl_s)
    gc_t = gc_s.T

    for h in range(N_HEADS):
        hs = slice(h * HEAD_DIM, (h + 1) * HEAD_DIM)
        q = _l2norm(y[:, hs]) * (HEAD_DIM ** -0.5)
        k = _l2norm(y[:, D_A + h * HEAD_DIM:D_A + (h + 1) * HEAD_DIM])
        v = y[:, 2 * D_A + h * HEAD_DIM:2 * D_A + (h + 1) * HEAD_DIM]
        la = LANE_A + h
        lbeta = LANE_B + h
        egc_c = egc_s[:, la:la + 1]
        u_, w_, qkd = _head_prepare(q, k, v, beta_s[:, lbeta:lbeta + 1], gc_s[:, la:la + 1],
                                    gc_t[la:la + 1, :], egc_c, CHUNK_SHIFT)
        qg = q * egc_c
        kg_t = (k * ekg_s[:, la:la + 1]).T
        egl_rep = jnp.broadcast_to(egl_s[:, la:la + 1], (n, HEAD_DIM))
        s = s_ref[h]
        outs = []
        zero = jnp.zeros((csz, HEAD_DIM), F32)
        for ci in range(n // csz):
            rows = slice(ci * csz, (ci + 1) * csz)
            pair = slice((ci // 2) * 2 * csz, (ci // 2 + 1) * 2 * csz)
            ws = _dot(jnp.concatenate([w_[rows], qg[rows]], axis=0), s)
            vn = u_[rows] - ws[:csz]
            vp = jnp.concatenate([vn, zero] if ci % 2 == 0 else [zero, vn], axis=0)
            outs.append(ws[csz:] + _dot(qkd[rows, pair], vp))
            s = s * egl_rep[ci * csz:ci * csz + 1, :] + _dot(kg_t[:, pair], vp)
        s_ref[h] = s
        o = jnp.concatenate(outs, axis=0)
        mix_ref[0, :, hs] = _gated_rmsnorm(o, proj_ref[0, :, C_Z + h * HEAD_DIM:C_Z + (h + 1) * HEAD_DIM],
                                           normw_ref[...])

    sfin_ref[0] = s_ref[...]

    p = proj_ref[0, :, C_P:C_P + D_B]
    extp = jnp.concatenate([pc_ref[...], p], axis=0)
    pc_ref[...] = p[n - 16:n, :]
    r, c = _iota2(n, n + 16)
    lag = r + 16 - c
    pos = l * n + lax.broadcasted_iota(jnp.int32, (n, 1), 0)
    for gi, w in enumerate(POOL_WINDOWS):
        gs = slice(gi * HEAD_DIM, (gi + 1) * HEAD_DIM)
        band = jnp.where((lag >= 0) & (lag < w), 1.0, 0.0).astype(BF16)
        cnt = jnp.minimum(pos + 1, w).astype(F32)
        mix_ref[0, :, D_A + gi * HEAD_DIM:D_A + (gi + 1) * HEAD_DIM] = _pool_out(
            _dot01(band, extp[:, gs]), cnt, p[:, gs], poolw_ref[gi], pscale_ref[:, gs])


def _mixer_prompt(proj, conv_w, arow, dtrow, normw, poolw, pscale, lb):
    b, seq, _ = proj.shape
    const2 = lambda i, j: (0, 0)
    return pl.pallas_call(
        functools.partial(_mixer_prompt_kernel, lb=lb),
        out_shape=(jax.ShapeDtypeStruct((b, seq, D_MODEL), F32),
                   jax.ShapeDtypeStruct((b, N_HEADS, HEAD_DIM, HEAD_DIM), F32)),
        grid=(b, seq // lb),
        in_specs=[pl.BlockSpec((1, lb, C_TOT), lambda i, j: (i, j, 0)),
                  pl.BlockSpec((CONV_W, C_QKV), const2),
                  pl.BlockSpec((1, 128), const2),
                  pl.BlockSpec((1, 128), const2),
                  pl.BlockSpec((1, HEAD_DIM), const2),
                  pl.BlockSpec((N_GROUPS, HEAD_DIM, HEAD_DIM), lambda i, j: (0, 0, 0)),
                  pl.BlockSpec((1, D_B), const2)],
        out_specs=(pl.BlockSpec((1, lb, D_MODEL), lambda i, j: (i, j, 0)),
                   pl.BlockSpec((1, N_HEADS, HEAD_DIM, HEAD_DIM), lambda i, j: (i, 0, 0, 0))),
        scratch_shapes=[pltpu.VMEM((8, C_QKV), F32), pltpu.VMEM((16, D_B), F32),
                        pltpu.VMEM((N_HEADS, HEAD_DIM, HEAD_DIM), F32)],
        compiler_params=pltpu.CompilerParams(dimension_semantics=("parallel", "arbitrary"),
                                             vmem_limit_bytes=VMEM_LIMIT),
        name="mixer_prompt",
    )(proj, conv_w, arow, dtrow, normw, poolw, pscale)


def _mixer_sample_kernel(proj_ref, cst_ref, pst_ref, sin_ref, convw_ref, arow_ref, dtrow_ref, normw_ref,
                         poolw_ref, pscale_ref, mix_ref, sout_ref, *, ns, seq, start):
    n = ns * seq
    sshift = seq.bit_length() - 1
    rowi = lax.broadcasted_iota(jnp.int32, (n, 1), 0)
    tpos = rowi & (seq - 1)

    u = proj_ref[:, 0:C_QKV]
    st = cst_ref[...]
    cw = convw_ref[...]
    acc = u * cw[CONV_W - 1:CONV_W, :]
    for d in range(1, CONV_W):
        term = jnp.where(tpos >= d, pltpu.roll(u, d, 0), pltpu.roll(st, n - seq + d, 0))
        acc = acc + term * cw[CONV_W - 1 - d:CONV_W - d, :]
    y = _silu(acc)

    beta_s, gc_s, gl_s = _gate_slabs(proj_ref[:, C_BA:C_TOT], arow_ref[...], dtrow_ref[...], sshift)
    egc_s = jnp.exp(gc_s)
    ekg_s = jnp.exp(gl_s - gc_s)
    egl_s = jnp.exp(gl_s)
    gc_t = gc_s.T

    for h in range(N_HEADS):
        hs = slice(h * HEAD_DIM, (h + 1) * HEAD_DIM)
        q = _l2norm(y[:, hs]) * (HEAD_DIM ** -0.5)
        k = _l2norm(y[:, D_A + h * HEAD_DIM:D_A + (h + 1) * HEAD_DIM])
        v = y[:, 2 * D_A + h * HEAD_DIM:2 * D_A + (h + 1) * HEAD_DIM]
        la = LANE_A + h
        lbeta = LANE_B + h
        egc_c = egc_s[:, la:la + 1]
        u_, w_, qkd = _head_prepare(q, k, v, beta_s[:, lbeta:lbeta + 1], gc_s[:, la:la + 1],
                                    gc_t[la:la + 1, :], egc_c, sshift)
        qg = q * egc_c
        kg_t = (k * ekg_s[:, la:la + 1]).T
        egl_rep = jnp.broadcast_to(egl_s[:, la:la + 1], (n, HEAD_DIM))
        ws_w, ws_q = [], []
        for si in range(ns):
            rows = slice(si * seq, (si + 1) * seq)
            ws = _dot(jnp.concatenate([w_[rows], qg[rows]], axis=0), sin_ref[si, h])
            ws_w.append(ws[:seq])
            ws_q.append(ws[seq:])
        vn = u_ - jnp.concatenate(ws_w, axis=0)
        o = jnp.concatenate(ws_q, axis=0) + _dot(qkd, vn)
        for si in range(ns):
            vmask = jnp.where((rowi >> sshift) == si, vn, 0.0)
            sout_ref[si, h] = sin_ref[si, h] * egl_rep[si * seq:si * seq + 1, :] + _dot(kg_t, vmask)
        mix_ref[:, hs] = _gated_rmsnorm(o, proj_ref[:, C_Z + h * HEAD_DIM:C_Z + (h + 1) * HEAD_DIM], normw_ref[...])

    p = proj_ref[:, C_P:C_P + D_B]
    pst = pst_ref[...]
    r, c = _iota2(n, n)
    band_new_base = ((r >> sshift) == (c >> sshift)) & (r >= c)
    r2, c2 = _iota2(n, ns * 16)
    same2 = (r2 >> sshift) == (c2 >> 4)
    t2 = r2 & (seq - 1)
    j2 = c2 & 15
    pos = start + tpos
    for gi, w in enumerate(POOL_WINDOWS):
        gs = slice(gi * HEAD_DIM, (gi + 1) * HEAD_DIM)
        band_new = jnp.where(band_new_base & ((r - c) < w), 1.0, 0.0).astype(BF16)
        band_st = jnp.where(same2 & (j2 >= 17 + t2 - w), 1.0, 0.0).astype(BF16)
        s = _dot01(band_new, p[:, gs]) + _dot01(band_st, pst[:, gs])
        cnt = jnp.minimum(pos + 1, w).astype(F32)
        mix_ref[:, D_A + gi * HEAD_DIM:D_A + (gi + 1) * HEAD_DIM] = _pool_out(
            s, cnt, p[:, gs], poolw_ref[gi], pscale_ref[:, gs])


def _mixer_sample(proj, cst, pst, sin, conv_w, arow, dtrow, normw, poolw, pscale, ns, seq, start):
    t = proj.shape[0]
    nb = t // seq
    n = ns * seq
    const1 = lambda i: (0, 0)
    return pl.pallas_call(
        functools.partial(_mixer_sample_kernel, ns=ns, seq=seq, start=start),
        out_shape=(jax.ShapeDtypeStruct((t, D_MODEL), F32),
                   jax.ShapeDtypeStruct((nb, N_HEADS, HEAD_DIM, HEAD_DIM), F32)),
        grid=(nb // ns,),
        in_specs=[pl.BlockSpec((n, C_TOT), lambda i: (i, 0)),
                  pl.BlockSpec((n, C_QKV), lambda i: (i, 0)),
                  pl.BlockSpec((ns * 16, D_B), lambda i: (i, 0)),
                  pl.BlockSpec((ns, N_HEADS, HEAD_DIM, HEAD_DIM), lambda i: (i, 0, 0, 0)),
                  pl.BlockSpec((CONV_W, C_QKV), const1),
                  pl.BlockSpec((1, 128), const1),
                  pl.BlockSpec((1, 128), const1),
                  pl.BlockSpec((1, HEAD_DIM), const1),
                  pl.BlockSpec((N_GROUPS, HEAD_DIM, HEAD_DIM), lambda i: (0, 0, 0)),
                  pl.BlockSpec((1, D_B), const1)],
        out_specs=(pl.BlockSpec((n, D_MODEL), lambda i: (i, 0)),
                   pl.BlockSpec((ns, N_HEADS, HEAD_DIM, HEAD_DIM), lambda i: (i, 0, 0, 0))),
        compiler_params=pltpu.CompilerParams(dimension_semantics=("parallel",), vmem_limit_bytes=VMEM_LIMIT),
        name="mixer_sample",
    )(proj, cst, pst, sin, conv_w, arow, dtrow, normw, poolw, pscale)


def _layer_norm(x, g, b):
    mu = jnp.mean(x, axis=-1, keepdims=True)
    xc = x - mu
    var = jnp.mean(xc * xc, axis=-1, keepdims=True)
    return xc * lax.rsqrt(var + LN_EPS) * g + b


def _outproj_router_kernel(mix_ref, x_ref, wout_ref, g1_ref, b1_ref, wrh_ref, wrl_ref, br_ref, h_ref, gate_ref):
    h = _layer_norm(ALPHA * x_ref[...] + _dot(mix_ref[...], wout_ref[...]), g1_ref[...], b1_ref[...])
    h_ref[...] = h
    hh, hm, _ = _split3(h)
    f = lambda a, b: jnp.dot(a, b, preferred_element_type=F32)
    logits = f(hh, wrh_ref[...]) + (f(hm, wrh_ref[...]) + f(hh, wrl_ref[...])) + br_ref[...]
    tm = logits.shape[0]
    lane = lax.broadcasted_iota(jnp.int32, (tm, 128), 1)
    big = jnp.int32(1 << 20)
    neg = -jnp.inf
    gmask = (lane >= N_EXPERTS) & (lane < N_EXPERTS + N_GROUPS)
    lg = jnp.where(gmask, logits, neg)
    gmax = jnp.max(lg, axis=1, keepdims=True)
    gidx = jnp.min(jnp.where(lg == gmax, lane - N_EXPERTS, big), axis=1, keepdims=True)
    pg = 1.0 / jnp.sum(jnp.where(gmask, jnp.exp(logits - gmax), 0.0), axis=1, keepdims=True)
    emask = (lane < N_EXPERTS) & ((lane >> 3) == gidx)
    le = jnp.where(emask, logits, neg)
    v1 = jnp.max(le, axis=1, keepdims=True)
    i1 = jnp.min(jnp.where((le == v1) & emask, lane, big), axis=1, keepdims=True)
    emask2 = emask & (lane != i1)
    le2 = jnp.where(emask2, logits, neg)
    v2 = jnp.max(le2, axis=1, keepdims=True)
    i2 = jnp.min(jnp.where((le2 == v2) & emask2, lane, big), axis=1, keepdims=True)
    e2 = jnp.exp(v2 - v1)
    den = 1.0 + e2
    gate_ref[...] = jnp.where(lane == i1, (1.0 / den) * pg, 0.0) + jnp.where(lane == i2, (e2 / den) * pg, 0.0)


def _outproj_router(mix, x2d, wout, g1, b1, wrh, wrl, br, tm):
    t = x2d.shape[0]
    row = lambda i: (i, 0)
    const = lambda i: (0, 0)
    return pl.pallas_call(
        _outproj_router_kernel,
        out_shape=(jax.ShapeDtypeStruct((t, D_MODEL), F32), jax.ShapeDtypeStruct((t, 128), F32)),
        grid=(t // tm,),
        in_specs=[pl.BlockSpec((tm, D_MODEL), row), pl.BlockSpec((tm, D_MODEL), row),
                  pl.BlockSpec((D_MODEL, D_MODEL), const), pl.BlockSpec((1, D_MODEL), const),
                  pl.BlockSpec((1, D_MODEL), const), pl.BlockSpec((D_MODEL, 128), const),
                  pl.BlockSpec((D_MODEL, 128), const), pl.BlockSpec((1, 128), const)],
        out_specs=(pl.BlockSpec((tm, D_MODEL), row), pl.BlockSpec((tm, 128), row)),
        compiler_params=pltpu.CompilerParams(dimension_semantics=("parallel",), vmem_limit_bytes=VMEM_LIMIT),
        name="outproj_router",
    )(mix, x2d, wout, g1, b1, wrh, wrl, br)


def _moe_kernel(h_ref, gate_ref, wg_ref, wu_ref, wd_ref, g2_ref, b2_ref, y_ref, hb_ref):
    e = pl.program_id(1)

    @pl.when(e == 0)
    def _init():
        hb_ref[...] = h_ref[...].astype(BF16)
        y_ref[...] = jnp.zeros_like(y_ref)

    hb = hb_ref[...]
    a = jnp.dot(hb, wg_ref[0].astype(BF16), preferred_element_type=F32)
    b = jnp.dot(hb, wu_ref[0].astype(BF16), preferred_element_type=F32)
    gate = gate_ref[...]
    lane = lax.broadcasted_iota(jnp.int32, gate.shape, 1)
    gcol = jnp.sum(jnp.where(lane == e, gate, 0.0), axis=1, keepdims=True)
    act = _silu(a) * b * gcol
    y_ref[...] += jnp.dot(act.astype(BF16), wd_ref[0].astype(BF16), preferred_element_type=F32)

    @pl.when(e == N_EXPERTS - 1)
    def _fin():
        y_ref[...] = _layer_norm(ALPHA * h_ref[...] + y_ref[...], g2_ref[...], b2_ref[...])


def _moe(h, gate, wg, wu, wd, g2, b2, tm):
    t = h.shape[0]
    return pl.pallas_call(
        _moe_kernel,
        out_shape=jax.ShapeDtypeStruct((t, D_MODEL), F32),
        grid=(t // tm, N_EXPERTS),
        in_specs=[pl.BlockSpec((tm, D_MODEL), lambda i, e: (i, 0)),
                  pl.BlockSpec((tm, 128), lambda i, e: (i, 0)),
                  pl.BlockSpec((1, D_MODEL, D_EXPERT), lambda i, e: (e, 0, 0)),
                  pl.BlockSpec((1, D_MODEL, D_EXPERT), lambda i, e: (e, 0, 0)),
                  pl.BlockSpec((1, D_EXPERT, D_MODEL), lambda i, e: (e, 0, 0)),
                  pl.BlockSpec((1, D_MODEL), lambda i, e: (0, 0)),
                  pl.BlockSpec((1, D_MODEL), lambda i, e: (0, 0))],
        out_specs=pl.BlockSpec((tm, D_MODEL), lambda i, e: (i, 0)),
        scratch_shapes=[pltpu.VMEM((tm, D_MODEL), BF16)],
        compiler_params=pltpu.CompilerParams(dimension_semantics=("parallel", "arbitrary"),
                                             vmem_limit_bytes=VMEM_LIMIT),
        name="moe",
    )(h, gate, wg, wu, wd, g2, b2)


def _tile(t, want):
    tm = min(want, t)
    while t % tm:
        tm //= 2
    return tm


def _prep_weights(w_in, conv_w, a_log, dt_bias, gdn_norm_w, pool_w, pool_scale, w_out, ln1_g, ln1_b,
                  w_rg, b_rg, w_re, b_re, w_gate, w_up, w_down, ln2_g, ln2_b):
    col_b = 4 * D_A
    col_p = 4 * D_A + 2 * N_HEADS
    w_cat = jnp.concatenate([w_in[:, :col_b], w_in[:, col_p:], w_in[:, col_b:col_p],
                             jnp.zeros((D_MODEL, 128 - 2 * N_HEADS), w_in.dtype)], axis=1).astype(BF16)
    lane_pad = lambda v, off: jnp.zeros((1, 128), F32).at[0, off:off + v.shape[0]].set(v.astype(F32))
    w_r = jnp.concatenate([w_re, w_rg, jnp.zeros((D_MODEL, 128 - N_EXPERTS - N_GROUPS), F32)], axis=1)
    wrh = w_r.astype(BF16)
    wrl = (w_r - wrh.astype(F32)).astype(BF16)
    b_r = jnp.zeros((1, 128), F32).at[0, :N_EXPERTS].set(b_re).at[0, N_EXPERTS:N_EXPERTS + N_GROUPS].set(b_rg)
    return dict(
        w_cat=w_cat, conv_w=conv_w, arow=lane_pad(a_log, LANE_A), dtrow=lane_pad(dt_bias, LANE_A),
        normw=gdn_norm_w.reshape(1, HEAD_DIM), poolw=pool_w.astype(BF16), pscale=pool_scale.reshape(1, D_B),
        wout=w_out.astype(BF16), g1=ln1_g.reshape(1, D_MODEL), b1=ln1_b.reshape(1, D_MODEL),
        wrh=wrh, wrl=wrl, br=b_r,
        wg=w_gate.reshape(N_EXPERTS, D_MODEL, D_EXPERT), wu=w_up.reshape(N_EXPERTS, D_MODEL, D_EXPERT),
        wd=w_down.reshape(N_EXPERTS, D_EXPERT, D_MODEL),
        g2=ln2_g.reshape(1, D_MODEL), b2=ln2_b.reshape(1, D_MODEL))


def _post_mixer(mix2d, x2d, p):
    t = x2d.shape[0]
    h, gate = _outproj_router(mix2d, x2d, p["wout"], p["g1"], p["b1"], p["wrh"], p["wrl"], p["br"], _tile(t, 512))
    return _moe(h, gate, p["wg"], p["wu"], p["wd"], p["g2"], p["b2"], _tile(t, 1024))


def _layer_prompt(x, p, lb=256):
    b, seq, _ = x.shape
    x2d = x.reshape(b * seq, D_MODEL)
    proj = _in_proj(x2d, p["w_cat"], _tile(b * seq, 512)).reshape(b, seq, C_TOT)
    mix, s_fin = _mixer_prompt(proj, p["conv_w"], p["arow"], p["dtrow"], p["normw"], p["poolw"], p["pscale"],
                               min(lb, seq))
    y = _post_mixer(mix.reshape(b * seq, D_MODEL), x2d, p).reshape(b, seq, D_MODEL)
    conv_new = proj[:, seq - (CONV_W - 1):, 0:C_QKV]
    pool_new = proj[:, seq - POOL_BUF:, C_P:C_P + D_B]
    return y, s_fin, conv_new, pool_new


def _layer_sample(x, s0, conv0, pool0, start, p, ns=16):
    b, seq, _ = x.shape
    x2d = x.reshape(b * seq, D_MODEL)
    proj = _in_proj(x2d, p["w_cat"], _tile(b * seq, 512))
    cst = jnp.pad(conv0, ((0, 0), (seq - (CONV_W - 1), 0), (0, 0))).reshape(b * seq, C_QKV)
    pst = jnp.pad(pool0, ((0, 0), (1, 0), (0, 0))).reshape(b * 16, D_B)
    mix, s_new = _mixer_sample(proj, cst, pst, s0, p["conv_w"], p["arow"], p["dtrow"], p["normw"], p["poolw"],
                               p["pscale"], min(ns, b), seq, start)
    y = _post_mixer(mix, x2d, p).reshape(b, seq, D_MODEL)
    proj3 = proj.reshape(b, seq, C_TOT)
    conv_new = proj3[:, seq - (CONV_W - 1):, 0:C_QKV]
    pool_new = jnp.concatenate([pool0[:, seq:, :], proj3[:, :, C_P:C_P + D_B]], axis=1)
    return y, s_new, conv_new, pool_new


def kernel(x_prompt, x_sample, state_delta, state_conv, state_pool, w_in, conv_w, a_log, dt_bias, gdn_norm_w,
           pool_w, pool_scale, w_out, ln1_g, ln1_b, w_rg, b_rg, w_re, b_re, w_gate, w_up, w_down, ln2_g, ln2_b):
    depth = w_in.shape[0]
    past_len = 16384
    yp, ys = x_prompt, x_sample
    outs = [[] for _ in range(6)]
    for l in range(depth):
        p = _prep_weights(w_in[l], conv_w[l], a_log[l], dt_bias[l], gdn_norm_w[l], pool_w[l], pool_scale[l],
                          w_out[l], ln1_g[l], ln1_b[l], w_rg[l], b_rg[l], w_re[l], b_re[l], w_gate[l], w_up[l],
                          w_down[l], ln2_g[l], ln2_b[l])
        yp, dp, cp, pp = _layer_prompt(yp, p)
        ys, ds, cs, ps = _layer_sample(ys, state_delta[l], state_conv[l], state_pool[l], past_len, p)
        for lst, v in zip(outs, (dp, cp, pp, ds, cs, ps)):
            lst.append(v)
    return (yp, ys) + tuple(jnp.stack(v) for v in outs)
```

```python
import functools

import jax
import jax.numpy as jnp
from jax import lax
from jax.experimental import pallas as pl
from jax.experimental.pallas import tpu as pltpu

F32 = jnp.float32
BF16 = jnp.bfloat16

D_MODEL = 1024
D_A = 512
D_B = 512
HEAD_DIM = 128
N_HEADS = 4
CONV_W = 4
CHUNK_SHIFT = 6
POOL_WINDOWS = (2, 4, 8, 16)
POOL_BUF = 15
N_GROUPS = 4
E_PER_GROUP = 8
N_EXPERTS = N_GROUPS * E_PER_GROUP
D_EXPERT = 256
ALPHA = 2.0 ** 0.25
LN_EPS = 1e-5
RMS_EPS = 1e-6
L2_EPS = 1e-6

C_QKV = 3 * D_A
C_Z = 3 * D_A
C_P = 4 * D_A
C_BA = 4 * D_A + D_B
C_TOT = C_BA + 128
LANE_B = 0
LANE_A = N_HEADS

VMEM_LIMIT = 56 * 1024 * 1024


def _dot(a, b):
    return jnp.dot(a.astype(BF16), b.astype(BF16), preferred_element_type=F32)


def _dot_nt(a, b):
    return lax.dot_general(a.astype(BF16), b.astype(BF16), (((1,), (1,)), ((), ())), preferred_element_type=F32)


def _split3(x):
    hi = x.astype(BF16)
    r = x - hi.astype(F32)
    mid = r.astype(BF16)
    lo = (r - mid.astype(F32)).astype(BF16)
    return hi, mid, lo


def _dot01(m01, x):
    hi, mid, lo = _split3(x)
    f = lambda p: jnp.dot(m01, p, preferred_element_type=F32)
    return f(hi) + f(mid) + f(lo)


def _silu(x):
    return x * jax.nn.sigmoid(x)


def _softplus(x):
    return jnp.maximum(x, 0.0) + jnp.log1p(jnp.exp(-jnp.abs(x)))


def _iota2(n, m):
    return lax.broadcasted_iota(jnp.int32, (n, m), 0), lax.broadcasted_iota(jnp.int32, (n, m), 1)


def _proj_kernel(x_ref, w_ref, o_ref):
    o_ref[...] = jnp.dot(x_ref[...].astype(BF16), w_ref[...], preferred_element_type=F32)


def _in_proj(x2d, w_cat, tm):
    t = x2d.shape[0]
    return pl.pallas_call(
        _proj_kernel,
        out_shape=jax.ShapeDtypeStruct((t, C_TOT), F32),
        grid=(t // tm,),
        in_specs=[pl.BlockSpec((tm, D_MODEL), lambda i: (i, 0)),
                  pl.BlockSpec((D_MODEL, C_TOT), lambda i: (0, 0))],
        out_specs=pl.BlockSpec((tm, C_TOT), lambda i: (i, 0)),
        compiler_params=pltpu.CompilerParams(dimension_semantics=("parallel",), vmem_limit_bytes=VMEM_LIMIT),
        name="in_proj",
    )(x2d, w_cat)


def _unit_lower_inverse(a, r, c, chunk_shift):
    b0 = min(4, chunk_shift)
    eye = jnp.where(r == c, 1.0, 0.0).astype(F32)
    x = jnp.where((r >> b0) == (c >> b0), a, 0.0)
    t = eye - x
    xs = x
    for _ in range(b0 - 1):
        xs = _dot(xs, xs)
        t = t + _dot(t, xs)
    for lvl in range(b0, chunk_shift):
        off = jnp.where(((r >> (lvl + 1)) == (c >> (lvl + 1))) & ((r >> lvl) != (c >> lvl)), a, 0.0)
        t = t - _dot(_dot(t, off), t)
    return t


def _gate_slabs(ba, arow, dtrow, chunk_shift):
    n = ba.shape[0]
    beta = jax.nn.sigmoid(ba)
    g = -jnp.exp(arow) * _softplus(ba + dtrow)
    r, c = _iota2(n, n)
    same = (r >> chunk_shift) == (c >> chunk_shift)
    ltri = jnp.where(same & (r >= c), 1.0, 0.0).astype(BF16)
    lall = jnp.where(same, 1.0, 0.0).astype(BF16)
    cs = _dot01(jnp.concatenate([ltri, lall], axis=0), g)
    return beta, cs[:n], cs[n:]


def _head_prepare(q, k, v, beta_c, gc_c, gc_r, egc_c, chunk_shift):
    n = q.shape[0]
    r, c = _iota2(n, n)
    same = (r >> chunk_shift) == (c >> chunk_shift)
    decay = jnp.exp(jnp.where(same & (r >= c), gc_c - gc_r, -jnp.inf))
    kb = k * beta_c
    vb = v * beta_c
    a = jnp.where(same & (r > c), _dot_nt(kb, k) * decay, 0.0)
    t = _unit_lower_inverse(a, r, c, chunk_shift)
    sol = _dot(t, jnp.concatenate([vb, kb * egc_c], axis=1))
    return sol[:, :HEAD_DIM], sol[:, HEAD_DIM:], _dot_nt(q, k) * decay


def _l2norm(x):
    return x * lax.rsqrt(jnp.sum(x * x, axis=-1, keepdims=True) + L2_EPS)


def _gated_rmsnorm(o, z, normw):
    o = o * lax.rsqrt(jnp.mean(o * o, axis=-1, keepdims=True) + RMS_EPS) * normw
    return o * _silu(z)


def _pool_out(s, cnt, p_g, poolw_g, pscale_g):
    d = s / cnt - p_g
    return _dot(d, poolw_g) * pscale_g


def _mixer_prompt_kernel(proj_ref, convw_ref, arow_ref, dtrow_ref, normw_ref, poolw_ref, pscale_ref,
                         mix_ref, sfin_ref, cc_ref, pc_ref, s_ref, *, lb):
    l = pl.program_id(1)
    n = lb
    csz = 1 << CHUNK_SHIFT

    @pl.when(l == 0)
    def _init():
        cc_ref[...] = jnp.zeros_like(cc_ref)
        pc_ref[...] = jnp.zeros_like(pc_ref)
        s_ref[...] = jnp.zeros_like(s_ref)

    u = proj_ref[0, :, 0:C_QKV]
    ext = jnp.concatenate([cc_ref[...], u], axis=0)
    cw = convw_ref[...]
    acc = ext * cw[CONV_W - 1:CONV_W, :]
    for d in range(1, CONV_W):
        acc = acc + pltpu.roll(ext, d, 0) * cw[CONV_W - 1 - d:CONV_W - d, :]
    cc_ref[...] = u[n - 8:n, :]
    y = _silu(acc[8:, :])

    beta_s, gc_s, gl_s = _gate_slabs(proj_ref[0, :, C_BA:C_TOT], arow_ref[...], dtrow_ref[...], CHUNK_SHIFT)
    egc_s = jnp.exp(gc_s)
    ekg_s = jnp.exp(gl_s - gc_s)
    egl_s = jnp.exp(gl_s)
    gc_t = gc_s.T

    for h in range(N_HEADS):
        hs = slice(h * HEAD_DIM, (h + 1) * HEAD_DIM)
        q = _l2norm(y[:, hs]) * (HEAD_DIM ** -0.5)
        k = _l2norm(y[:, D_A + h * HEAD_DIM:D_A + (h + 1) * HEAD_DIM])
        v = y[:, 2 * D_A + h * HEAD_DIM:2 * D_A + (h + 1) * HEAD_DIM]
        la = LANE_A + h
        lbeta = LANE_B + h
        egc_c = egc_s[:, la:la + 1]
        u_, w_, qkd = _head_prepare(q, k, v, beta_s[:, lbeta:lbeta + 1], gc_s[:, la:la + 1],
                                    gc_t[la:la + 1, :], egc_c, CHUNK_SHIFT)
        qg = q * egc_c
        kg_t = (k * ekg_s[:, la:la + 1]).T
        egl_rep = jnp.broadcast_to(egl_s[:, la:la + 1], (n, HEAD_DIM))
        s = s_ref[h]
        outs = []
        zero = jnp.zeros((csz, HEAD_DIM), F32)
        for ci in range(n // csz):
            rows = slice(ci * csz, (ci + 1) * csz)
            pair = slice((ci // 2) * 2 * csz, (ci // 2 + 1) * 2 * csz)
            ws = _dot(jnp.concatenate([w_[rows], qg[rows]], axis=0), s)
            vn = u_[rows] - ws[:csz]
            vp = jnp.concatenate([vn, zero] if ci % 2 == 0 else [zero, vn], axis=0)
            outs.append(ws[csz:] + _dot(qkd[rows, pair], vp))
            s = s * egl_rep[ci * csz:ci * csz + 1, :] + _dot(kg_t[:, pair], vp)
        s_ref[h] = s
        o = jnp.concatenate(outs, axis=0)
        mix_ref[0, :, hs] = _gated_rmsnorm(o, proj_ref[0, :, C_Z + h * HEAD_DIM:C_Z + (h + 1) * HEAD_DIM],
                                           normw_ref[...])

    sfin_ref[0] = s_ref[...]

    p = proj_ref[0, :, C_P:C_P + D_B]
    extp = jnp.concatenate([pc_ref[...], p], axis=0)
    pc_ref[...] = p[n - 16:n, :]
    r, c = _iota2(n, n + 16)
    lag = r + 16 - c
    pos = l * n + lax.broadcasted_iota(jnp.int32, (n, 1), 0)
    for gi, w in enumerate(POOL_WINDOWS):
        gs = slice(gi * HEAD_DIM, (gi + 1) * HEAD_DIM)
        band = jnp.where((lag >= 0) & (lag < w), 1.0, 0.0).astype(BF16)
        cnt = jnp.minimum(pos + 1, w).astype(F32)
        mix_ref[0, :, D_A + gi * HEAD_DIM:D_A + (gi + 1) * HEAD_DIM] = _pool_out(
            _dot01(band, extp[:, gs]), cnt, p[:, gs], poolw_ref[gi], pscale_ref[:, gs])


def _mixer_prompt(proj, conv_w, arow, dtrow, normw, poolw, pscale, lb):
    b, seq, _ = proj.shape
    const2 = lambda i, j: (0, 0)
    return pl.pallas_call(
        functools.partial(_mixer_prompt_kernel, lb=lb),
        out_shape=(jax.ShapeDtypeStruct((b, seq, D_MODEL), F32),
                   jax.ShapeDtypeStruct((b, N_HEADS, HEAD_DIM, HEAD_DIM), F32)),
        grid=(b, seq // lb),
        in_specs=[pl.BlockSpec((1, lb, C_TOT), lambda i, j: (i, j, 0)),
                  pl.BlockSpec((CONV_W, C_QKV), const2),
                  pl.BlockSpec((1, 128), const2),
                  pl.BlockSpec((1, 128), const2),
                  pl.BlockSpec((1, HEAD_DIM), const2),
                  pl.BlockSpec((N_GROUPS, HEAD_DIM, HEAD_DIM), lambda i, j: (0, 0, 0)),
                  pl.BlockSpec((1, D_B), const2)],
        out_specs=(pl.BlockSpec((1, lb, D_MODEL), lambda i, j: (i, j, 0)),
                   pl.BlockSpec((1, N_HEADS, HEAD_DIM, HEAD_DIM), lambda i, j: (i, 0, 0, 0))),
        scratch_shapes=[pltpu.VMEM((8, C_QKV), F32), pltpu.VMEM((16, D_B), F32),
                        pltpu.VMEM((N_HEADS, HEAD_DIM, HEAD_DIM), F32)],
        compiler_params=pltpu.CompilerParams(dimension_semantics=("parallel", "arbitrary"),
                                             vmem_limit_bytes=VMEM_LIMIT),
        name="mixer_prompt",
    )(proj, conv_w, arow, dtrow, normw, poolw, pscale)


def _mixer_sample_kernel(proj_ref, cst_ref, pst_ref, sin_ref, convw_ref, arow_ref, dtrow_ref, normw_ref,
                         poolw_ref, pscale_ref, mix_ref, sout_ref, *, ns, seq, start):
    n = ns * seq
    sshift = seq.bit_length() - 1
    rowi = lax.broadcasted_iota(jnp.int32, (n, 1), 0)
    tpos = rowi & (seq - 1)

    u = proj_ref[:, 0:C_QKV]
    st = cst_ref[...]
    cw = convw_ref[...]
    acc = u * cw[CONV_W - 1:CONV_W, :]
    for d in range(1, CONV_W):
        term = jnp.where(tpos >= d, pltpu.roll(u, d, 0), pltpu.roll(st, n - seq + d, 0))
        acc = acc + term * cw[CONV_W - 1 - d:CONV_W - d, :]
    y = _silu(acc)

    beta_s, gc_s, gl_s = _gate_slabs(proj_ref[:, C_BA:C_TOT], arow_ref[...], dtrow_ref[...], sshift)
    egc_s = jnp.exp(gc_s)
    ekg_s = jnp.exp(gl_s - gc_s)
    egl_s = jnp.exp(gl_s)
    gc_t = gc_s.T

    for h in range(N_HEADS):
        hs = slice(h * HEAD_DIM, (h + 1) * HEAD_DIM)
        q = _l2norm(y[:, hs]) * (HEAD_DIM ** -0.5)
        k = _l2norm(y[:, D_A + h * HEAD_DIM:D_A + (h + 1) * HEAD_DIM])
        v = y[:, 2 * D_A + h * HEAD_DIM:2 * D_A + (h + 1) * HEAD_DIM]
        la = LANE_A + h
        lbeta = LANE_B + h
        egc_c = egc_s[:, la:la + 1]
        u_, w_, qkd = _head_prepare(q, k, v, beta_s[:, lbeta:lbeta + 1], gc_s[:, la:la + 1],
                                    gc_t[la:la + 1, :], egc_c, sshift)
        qg = q * egc_c
        kg_t = (k * ekg_s[:, la:la + 1]).T
        egl_rep = jnp.broadcast_to(egl_s[:, la:la + 1], (n, HEAD_DIM))
        ws_w, ws_q = [], []
        for si in range(ns):
            rows = slice(si * seq, (si + 1) * seq)
            ws = _dot(jnp.concatenate([w_[rows], qg[rows]], axis=0), sin_ref[si, h])
            ws_w.append(ws[:seq])
            ws_q.append(ws[seq:])
        vn = u_ - jnp.concatenate(ws_w, axis=0)
        o = jnp.concatenate(ws_q, axis=0) + _dot(qkd, vn)
        for si in range(ns):
            vmask = jnp.where((rowi >> sshift) == si, vn, 0.0)
            sout_ref[si, h] = sin_ref[si, h] * egl_rep[si * seq:si * seq + 1, :] + _dot(kg_t, vmask)
        mix_ref[:, hs] = _gated_rmsnorm(o, proj_ref[:, C_Z + h * HEAD_DIM:C_Z + (h + 1) * HEAD_DIM], normw_ref[...])

    p = proj_ref[:, C_P:C_P + D_B]
    pst = pst_ref[...]
    r, c = _iota2(n, n)
    band_new_base = ((r >> sshift) == (c >> sshift)) & (r >= c)
    r2, c2 = _iota2(n, ns * 16)
    same2 = (r2 >> sshift) == (c2 >> 4)
    t2 = r2 & (seq - 1)
    j2 = c2 & 15
    pos = start + tpos
    for gi, w in enumerate(POOL_WINDOWS):
        gs = slice(gi * HEAD_DIM, (gi + 1) * HEAD_DIM)
        band_new = jnp.where(band_new_base & ((r - c) < w), 1.0, 0.0).astype(BF16)
        band_st = jnp.where(same2 & (j2 >= 17 + t2 - w), 1.0, 0.0).astype(BF16)
        s = _dot01(band_new, p[:, gs]) + _dot01(band_st, pst[:, gs])
        cnt = jnp.minimum(pos + 1, w).astype(F32)
        mix_ref[:, D_A + gi * HEAD_DIM:D_A + (gi + 1) * HEAD_DIM] = _pool_out(
            s, cnt, p[:, gs], poolw_ref[gi], pscale_ref[:, gs])


def _mixer_sample(proj, cst, pst, sin, conv_w, arow, dtrow, normw, poolw, pscale, ns, seq, start):
    t = proj.shape[0]
    nb = t // seq
    n = ns * seq
    const1 = lambda i: (0, 0)
    return pl.pallas_call(
        functools.partial(_mixer_sample_kernel, ns=ns, seq=seq, start=start),
        out_shape=(jax.ShapeDtypeStruct((t, D_MODEL), F32),
                   jax.ShapeDtypeStruct((nb, N_HEADS, HEAD_DIM, HEAD_DIM), F32)),
        grid=(nb // ns,),
        in_specs=[pl.BlockSpec((n, C_TOT), lambda i: (i, 0)),
                  pl.BlockSpec((n, C_QKV), lambda i: (i, 0)),
                  pl.BlockSpec((ns * 16, D_B), lambda i: (i, 0)),
                  pl.BlockSpec((ns, N_HEADS, HEAD_DIM, HEAD_DIM), lambda i: (i, 0, 0, 0)),
                  pl.BlockSpec((CONV_W, C_QKV), const1),
                  pl.BlockSpec((1, 128), const1),
                  pl.BlockSpec((1, 128), const1),
                  pl.BlockSpec((1, HEAD_DIM), const1),
                  pl.BlockSpec((N_GROUPS, HEAD_DIM, HEAD_DIM), lambda i: (0, 0, 0)),
                  pl.BlockSpec((1, D_B), const1)],
        out_specs=(pl.BlockSpec((n, D_MODEL), lambda i: (i, 0)),
                   pl.BlockSpec((ns, N_HEADS, HEAD_DIM, HEAD_DIM), lambda i: (i, 0, 0, 0))),
        compiler_params=pltpu.CompilerParams(dimension_semantics=("parallel",), vmem_limit_bytes=VMEM_LIMIT),
        name="mixer_sample",
    )(proj, cst, pst, sin, conv_w, arow, dtrow, normw, poolw, pscale)


def _layer_norm(x, g, b):
    mu = jnp.mean(x, axis=-1, keepdims=True)
    xc = x - mu
    var = jnp.mean(xc * xc, axis=-1, keepdims=True)
    return xc * lax.rsqrt(var + LN_EPS) * g + b


def _outproj_router_kernel(mix_ref, x_ref, wout_ref, g1_ref, b1_ref, wrh_ref, wrl_ref, br_ref,
                           h_ref, sel_ref, gw_ref, cnt_ref, carry_ref):
    @pl.when(pl.program_id(0) == 0)
    def _init():
        carry_ref[...] = jnp.zeros_like(carry_ref)

    h = _layer_norm(ALPHA * x_ref[...] + _dot(mix_ref[...], wout_ref[...]), g1_ref[...], b1_ref[...])
    h_ref[...] = h
    hh, hm, _ = _split3(h)
    f = lambda a, b: jnp.dot(a, b, preferred_element_type=F32)
    logits = f(hh, wrh_ref[...]) + (f(hm, wrh_ref[...]) + f(hh, wrl_ref[...])) + br_ref[...]
    tm = logits.shape[0]
    lane = lax.broadcasted_iota(jnp.int32, (tm, 128), 1)
    big = jnp.int32(1 << 20)
    neg = -jnp.inf
    gmask = (lane >= N_EXPERTS) & (lane < N_EXPERTS + N_GROUPS)
    lg = jnp.where(gmask, logits, neg)
    gmax = jnp.max(lg, axis=1, keepdims=True)
    gidx = jnp.min(jnp.where(lg == gmax, lane - N_EXPERTS, big), axis=1, keepdims=True)
    pg = 1.0 / jnp.sum(jnp.where(gmask, jnp.exp(logits - gmax), 0.0), axis=1, keepdims=True)
    emask = (lane < N_EXPERTS) & ((lane >> 3) == gidx)
    le = jnp.where(emask, logits, neg)
    v1 = jnp.max(le, axis=1, keepdims=True)
    i1 = jnp.min(jnp.where((le == v1) & emask, lane, big), axis=1, keepdims=True)
    emask2 = emask & (lane != i1)
    le2 = jnp.where(emask2, logits, neg)
    v2 = jnp.max(le2, axis=1, keepdims=True)
    i2 = jnp.min(jnp.where((le2 == v2) & emask2, lane, big), axis=1, keepdims=True)
    e2 = jnp.exp(v2 - v1)
    den = 1.0 + e2
    gw_ref[...] = jnp.where(lane == 0, (1.0 / den) * pg, jnp.where(lane == 1, (e2 / den) * pg, 0.0))
    onehot = jnp.where((lane == i1) | (lane == i2), 1.0, 0.0)
    r, c = _iota2(tm, tm)
    before = jnp.dot(jnp.where(r > c, 1.0, 0.0).astype(BF16), onehot.astype(BF16), preferred_element_type=F32)
    before = before + carry_ref[...]
    r1 = jnp.sum(jnp.where(lane == i1, before, 0.0), axis=1, keepdims=True).astype(jnp.int32)
    r2 = jnp.sum(jnp.where(lane == i2, before, 0.0), axis=1, keepdims=True).astype(jnp.int32)
    carry_ref[...] += jnp.sum(onehot, axis=0, keepdims=True)
    cnt_ref[...] = carry_ref[...].astype(jnp.int32)
    sel_ref[...] = jnp.where(lane == 0, i1, jnp.where(lane == 1, i2, jnp.where(lane == 2, r1,
                                                                               jnp.where(lane == 3, r2, 0))))


def _outproj_router(mix, x2d, wout, g1, b1, wrh, wrl, br, tm):
    t = x2d.shape[0]
    row = lambda i: (i, 0)
    const = lambda i: (0, 0)
    return pl.pallas_call(
        _outproj_router_kernel,
        out_shape=(jax.ShapeDtypeStruct((t, D_MODEL), F32), jax.ShapeDtypeStruct((t, 128), jnp.int32),
                   jax.ShapeDtypeStruct((t, 128), F32), jax.ShapeDtypeStruct((1, 128), jnp.int32)),
        grid=(t // tm,),
        in_specs=[pl.BlockSpec((tm, D_MODEL), row), pl.BlockSpec((tm, D_MODEL), row),
                  pl.BlockSpec((D_MODEL, D_MODEL), const), pl.BlockSpec((1, D_MODEL), const),
                  pl.BlockSpec((1, D_MODEL), const), pl.BlockSpec((D_MODEL, 128), const),
                  pl.BlockSpec((D_MODEL, 128), const), pl.BlockSpec((1, 128), const)],
        out_specs=(pl.BlockSpec((tm, D_MODEL), row), pl.BlockSpec((tm, 128), row), pl.BlockSpec((tm, 128), row),
                   pl.BlockSpec((1, 128), const)),
        scratch_shapes=[pltpu.VMEM((1, 128), F32)],
        compiler_params=pltpu.CompilerParams(dimension_semantics=("arbitrary",), vmem_limit_bytes=VMEM_LIMIT),
        name="outproj_router",
    )(mix, x2d, wout, g1, b1, wrh, wrl, br)


ROW_SLAB = D_MODEL // 128
EXPERT_TILE = 256
ROUTE_TILE = 256


def _to_slabs(ref, x, n):
    for c in range(ROW_SLAB):
        ref[pl.ds(c, n, stride=ROW_SLAB), :] = x[:, c * 128:(c + 1) * 128]


def _from_slabs(ref, n):
    return jnp.concatenate([ref[pl.ds(c, n, stride=ROW_SLAB), :] for c in range(ROW_SLAB)], axis=1)


def _slab(ref, row):
    if isinstance(row, int):
        return ref.at[pl.ds(row * ROW_SLAB, ROW_SLAB)]
    return ref.at[pl.ds(pl.multiple_of(row * ROW_SLAB, ROW_SLAB), ROW_SLAB)]


def _dispatch_kernel(pad_start_ref, pad_cnt_ref, tail_ref, h_ref, pos_ref, xs_ref, stage_ref, sem, zsem, tsem,
                     *, td, tmx):
    @pl.when(pl.program_id(0) == 0)
    def _zero_unused_rows():
        stage_ref[...] = jnp.zeros_like(stage_ref)
        zcopy = lambda row: pltpu.make_async_copy(_slab(stage_ref, 0), _slab(xs_ref, row), zsem)
        tile_rows = tmx * ROW_SLAB
        tcopy = lambda tile: pltpu.make_async_copy(
            stage_ref.at[pl.ds(0, tile_rows)],
            xs_ref.at[pl.ds(pl.multiple_of(tile * tile_rows, tile_rows), tile_rows)], tsem)

        def per_expert(e, total):
            start = pad_start_ref[e]
            n = pad_cnt_ref[e]

            def body(r, carry):
                zcopy(start + r).start()
                return carry

            lax.fori_loop(0, n, body, 0)
            return total + n

        total = lax.fori_loop(0, N_EXPERTS, per_expert, 0)

        def tail_start(r, carry):
            tcopy(tail_ref[0] + r).start()
            return carry

        lax.fori_loop(0, tail_ref[1], tail_start, 0)

        def wait_one(r, carry):
            zcopy(0).wait()
            return carry

        lax.fori_loop(0, total, wait_one, 0)

        def tail_wait(r, carry):
            tcopy(0).wait()
            return carry

        lax.fori_loop(0, tail_ref[1], tail_wait, 0)

    _to_slabs(stage_ref, h_ref[...], td)
    copies = []
    for t in range(td):
        for k in range(2):
            cp = pltpu.make_async_copy(_slab(stage_ref, t), _slab(xs_ref, pos_ref[0, 0, 2 * t + k]), sem)
            cp.start()
            copies.append(cp)
    for cp in copies:
        cp.wait()


def _dispatch(h, pos3, pad_start, pad_cnt, tail, n_rows, td, tmx):
    t = h.shape[0]
    assert td >= tmx
    return pl.pallas_call(
        functools.partial(_dispatch_kernel, td=td, tmx=tmx),
        out_shape=jax.ShapeDtypeStruct((n_rows * ROW_SLAB, 128), F32),
        grid_spec=pltpu.PrefetchScalarGridSpec(
            num_scalar_prefetch=3,
            grid=(t // td,),
            in_specs=[pl.BlockSpec((td, D_MODEL), lambda i, *_: (i, 0)),
                      pl.BlockSpec((1, 1, 2 * td), lambda i, *_: (i, 0, 0), memory_space=pltpu.SMEM)],
            out_specs=pl.BlockSpec(memory_space=pl.ANY),
            scratch_shapes=[pltpu.VMEM((td * ROW_SLAB, 128), F32),
                            pltpu.SemaphoreType.DMA, pltpu.SemaphoreType.DMA, pltpu.SemaphoreType.DMA]),
        compiler_params=pltpu.CompilerParams(dimension_semantics=("arbitrary",), vmem_limit_bytes=VMEM_LIMIT),
        name="moe_dispatch",
    )(pad_start, pad_cnt, tail, h, pos3)


def _experts_kernel(tile_idx_ref, tile_e_ref, tile_ok_ref, x_ref, wg_ref, wu_ref, wd_ref, o_ref, *, tmx):
    ok = tile_ok_ref[pl.program_id(0)] != 0

    @pl.when(ok)
    def _compute():
        x = _from_slabs(x_ref, tmx).astype(BF16)
        a = jnp.dot(x, wg_ref[0].astype(BF16), preferred_element_type=F32)
        b = jnp.dot(x, wu_ref[0].astype(BF16), preferred_element_type=F32)
        act = (_silu(a) * b).astype(BF16)
        _to_slabs(o_ref, jnp.dot(act, wd_ref[0].astype(BF16), preferred_element_type=F32), tmx)

    @pl.when(jnp.logical_not(ok))
    def _unused_tile():
        o_ref[...] = jnp.zeros_like(o_ref)


def _experts(xs, tile_idx, tile_e, tile_ok, wg, wu, wd, tmx):
    nt = tile_idx.shape[0]
    rows = lambda j, ti, te, ok: (ti[j], 0)
    wsel = lambda j, ti, te, ok: (te[j], 0, 0)
    own = lambda j, ti, te, ok: (j, 0)
    return pl.pallas_call(
        functools.partial(_experts_kernel, tmx=tmx),
        out_shape=jax.ShapeDtypeStruct(xs.shape, F32),
        grid_spec=pltpu.PrefetchScalarGridSpec(
            num_scalar_prefetch=3,
            grid=(nt,),
            in_specs=[pl.BlockSpec((tmx * ROW_SLAB, 128), rows),
                      pl.BlockSpec((1, D_MODEL, D_EXPERT), wsel),
                      pl.BlockSpec((1, D_MODEL, D_EXPERT), wsel),
                      pl.BlockSpec((1, D_EXPERT, D_MODEL), wsel)],
            out_specs=pl.BlockSpec((tmx * ROW_SLAB, 128), own)),
        compiler_params=pltpu.CompilerParams(dimension_semantics=("arbitrary",), vmem_limit_bytes=VMEM_LIMIT),
        name="moe_experts",
    )(tile_idx, tile_e, tile_ok, xs, wg, wu, wd)


def _combine_kernel(h_ref, gw_ref, pos_ref, g2_ref, b2_ref, os_ref, y_ref, stage_ref, sem, *, td):
    copies = []
    for t in range(td):
        for k in range(2):
            cp = pltpu.make_async_copy(_slab(os_ref, pos_ref[0, 0, 2 * t + k]), _slab(stage_ref.at[k], t), sem)
            cp.start()
            copies.append(cp)
    for cp in copies:
        cp.wait()
    gw = gw_ref[...]
    moe = gw[:, 0:1] * _from_slabs(stage_ref.at[0], td) + gw[:, 1:2] * _from_slabs(stage_ref.at[1], td)
    y_ref[...] = _layer_norm(ALPHA * h_ref[...] + moe, g2_ref[...], b2_ref[...])


def _combine(h, gw, pos3, g2, b2, os, td):
    t = h.shape[0]
    row = lambda i: (i, 0)
    const = lambda i: (0, 0)
    return pl.pallas_call(
        functools.partial(_combine_kernel, td=td),
        out_shape=jax.ShapeDtypeStruct((t, D_MODEL), F32),
        grid=(t // td,),
        in_specs=[pl.BlockSpec((td, D_MODEL), row), pl.BlockSpec((td, 128), row),
                  pl.BlockSpec((1, 1, 2 * td), lambda i: (i, 0, 0), memory_space=pltpu.SMEM),
                  pl.BlockSpec((1, D_MODEL), const), pl.BlockSpec((1, D_MODEL), const),
                  pl.BlockSpec(memory_space=pl.ANY)],
        out_specs=pl.BlockSpec((td, D_MODEL), row),
        scratch_shapes=[pltpu.VMEM((2, td * ROW_SLAB, 128), F32), pltpu.SemaphoreType.DMA],
        compiler_params=pltpu.CompilerParams(dimension_semantics=("arbitrary",), vmem_limit_bytes=VMEM_LIMIT),
        name="moe_combine",
    )(h, gw, pos3, g2, b2, os)


def _route_plan(sel, cnt, t, tmx, td):
    i32 = jnp.int32
    counts = cnt[0, :N_EXPERTS]
    padded = ((counts + tmx - 1) // tmx) * tmx
    ends = jnp.cumsum(padded).astype(i32)
    offs = ends - padded
    pos = (jnp.take(offs, sel[:, 0:2]) + sel[:, 2:4]).astype(i32)
    nt = 2 * t // tmx + N_EXPERTS
    n_used = ends[-1] // tmx
    tile = jnp.arange(nt, dtype=i32)
    tile_idx = jnp.minimum(tile, jnp.maximum(n_used - 1, 0))
    tile_e = jnp.minimum(jnp.searchsorted(ends, tile_idx * tmx, side="right"), N_EXPERTS - 1).astype(i32)
    tile_ok = (tile < n_used).astype(i32)
    tail = jnp.stack([n_used, nt - n_used]).astype(i32)
    return (pos.reshape(t // td, 1, 2 * td), tile_idx, tile_e, tile_ok, (offs + counts).astype(i32),
            (padded - counts).astype(i32), tail, nt * tmx)


def _tile(t, want):
    tm = min(want, t)
    while t % tm:
        tm //= 2
    return tm


def _prep_weights(w_in, conv_w, a_log, dt_bias, gdn_norm_w, pool_w, pool_scale, w_out, ln1_g, ln1_b,
                  w_rg, b_rg, w_re, b_re, w_gate, w_up, w_down, ln2_g, ln2_b):
    col_b = 4 * D_A
    col_p = 4 * D_A + 2 * N_HEADS
    w_cat = jnp.concatenate([w_in[:, :col_b], w_in[:, col_p:], w_in[:, col_b:col_p],
                             jnp.zeros((D_MODEL, 128 - 2 * N_HEADS), w_in.dtype)], axis=1).astype(BF16)
    lane_pad = lambda v, off: jnp.zeros((1, 128), F32).at[0, off:off + v.shape[0]].set(v.astype(F32))
    w_r = jnp.concatenate([w_re, w_rg, jnp.zeros((D_MODEL, 128 - N_EXPERTS - N_GROUPS), F32)], axis=1)
    wrh = w_r.astype(BF16)
    wrl = (w_r - wrh.astype(F32)).astype(BF16)
    b_r = jnp.zeros((1, 128), F32).at[0, :N_EXPERTS].set(b_re).at[0, N_EXPERTS:N_EXPERTS + N_GROUPS].set(b_rg)
    return dict(
        w_cat=w_cat, conv_w=conv_w, arow=lane_pad(a_log, LANE_A), dtrow=lane_pad(dt_bias, LANE_A),
        normw=gdn_norm_w.reshape(1, HEAD_DIM), poolw=pool_w.astype(BF16), pscale=pool_scale.reshape(1, D_B),
        wout=w_out.astype(BF16), g1=ln1_g.reshape(1, D_MODEL), b1=ln1_b.reshape(1, D_MODEL),
        wrh=wrh, wrl=wrl, br=b_r,
        wg=w_gate.reshape(N_EXPERTS, D_MODEL, D_EXPERT), wu=w_up.reshape(N_EXPERTS, D_MODEL, D_EXPERT),
        wd=w_down.reshape(N_EXPERTS, D_EXPERT, D_MODEL),
        g2=ln2_g.reshape(1, D_MODEL), b2=ln2_b.reshape(1, D_MODEL))


def _post_mixer(mix2d, x2d, p):
    t = x2d.shape[0]
    h, sel, gw, cnt = _outproj_router(mix2d, x2d, p["wout"], p["g1"], p["b1"], p["wrh"], p["wrl"], p["br"],
                                      _tile(t, 512))
    td = _tile(t, ROUTE_TILE)
    pos3, tile_idx, tile_e, tile_ok, pad_start, pad_cnt, tail, n_rows = _route_plan(sel, cnt, t, EXPERT_TILE, td)
    xs = _dispatch(h, pos3, pad_start, pad_cnt, tail, n_rows, td, EXPERT_TILE)
    os = _experts(xs, tile_idx, tile_e, tile_ok, p["wg"], p["wu"], p["wd"], EXPERT_TILE)
    return _combine(h, gw, pos3, p["g2"], p["b2"], os, td)


def _layer_prompt(x, p, lb=256):
    b, seq, _ = x.shape
    x2d = x.reshape(b * seq, D_MODEL)
    proj = _in_proj(x2d, p["w_cat"], _tile(b * seq, 512)).reshape(b, seq, C_TOT)
    mix, s_fin = _mixer_prompt(proj, p["conv_w"], p["arow"], p["dtrow"], p["normw"], p["poolw"], p["pscale"],
                               min(lb, seq))
    y = _post_mixer(mix.reshape(b * seq, D_MODEL), x2d, p).reshape(b, seq, D_MODEL)
    conv_new = proj[:, seq - (CONV_W - 1):, 0:C_QKV]
    pool_new = proj[:, seq - POOL_BUF:, C_P:C_P + D_B]
    return y, s_fin, conv_new, pool_new


def _layer_sample(x, s0, conv0, pool0, start, p, ns=16):
    b, seq, _ = x.shape
    x2d = x.reshape(b * seq, D_MODEL)
    proj = _in_proj(x2d, p["w_cat"], _tile(b * seq, 512))
    cst = jnp.pad(conv0, ((0, 0), (seq - (CONV_W - 1), 0), (0, 0))).reshape(b * seq, C_QKV)
    pst = jnp.pad(pool0, ((0, 0), (1, 0), (0, 0))).reshape(b * 16, D_B)
    mix, s_new = _mixer_sample(proj, cst, pst, s0, p["conv_w"], p["arow"], p["dtrow"], p["normw"], p["poolw"],
                               p["pscale"], min(ns, b), seq, start)
    y = _post_mixer(mix, x2d, p).reshape(b, seq, D_MODEL)
    proj3 = proj.reshape(b, seq, C_TOT)
    conv_new = proj3[:, seq - (CONV_W - 1):, 0:C_QKV]
    pool_new = jnp.concatenate([pool0[:, seq:, :], proj3[:, :, C_P:C_P + D_B]], axis=1)
    return y, s_new, conv_new, pool_new


def kernel(x_prompt, x_sample, state_delta, state_conv, state_pool, w_in, conv_w, a_log, dt_bias, gdn_norm_w,
           pool_w, pool_scale, w_out, ln1_g, ln1_b, w_rg, b_rg, w_re, b_re, w_gate, w_up, w_down, ln2_g, ln2_b):
    depth = w_in.shape[0]
    past_len = 16384
    yp, ys = x_prompt, x_sample
    outs = [[] for _ in range(6)]
    for l in range(depth):
        p = _prep_weights(w_in[l], conv_w[l], a_log[l], dt_bias[l], gdn_norm_w[l], pool_w[l], pool_scale[l],
                          w_out[l], ln1_g[l], ln1_b[l], w_rg[l], b_rg[l], w_re[l], b_re[l], w_gate[l], w_up[l],
                          w_down[l], ln2_g[l], ln2_b[l])
        yp, dp, cp, pp = _layer_prompt(yp, p)
        ys, ds, cs, ps = _layer_sample(ys, state_delta[l], state_conv[l], state_pool[l], past_len, p)
        for lst, v in zip(outs, (dp, cp, pp, ds, cs, ps)):
            lst.append(v)
    return (yp, ys) + tuple(jnp.stack(v) for v in outs)
```

```python
import functools

import jax
import jax.numpy as jnp
from jax import lax
from jax.experimental import pallas as pl
from jax.experimental.pallas import tpu as pltpu

F32 = jnp.float32
BF16 = jnp.bfloat16

D_MODEL = 1024
D_A = 512
D_B = 512
HEAD_DIM = 128
N_HEADS = 4
CONV_W = 4
CHUNK_SHIFT = 6
POOL_WINDOWS = (2, 4, 8, 16)
POOL_BUF = 15
N_GROUPS = 4
E_PER_GROUP = 8
N_EXPERTS = N_GROUPS * E_PER_GROUP
D_EXPERT = 256
ALPHA = 2.0 ** 0.25
LN_EPS = 1e-5
RMS_EPS = 1e-6
L2_EPS = 1e-6

C_QKV = 3 * D_A
C_Z = 3 * D_A
C_P = 4 * D_A
C_BA = 4 * D_A + D_B
C_TOT = C_BA + 128
LANE_B = 0
LANE_A = N_HEADS

VMEM_LIMIT = 56 * 1024 * 1024


def _dot(a, b):
    return jnp.dot(a.astype(BF16), b.astype(BF16), preferred_element_type=F32)


def _dot_nt(a, b):
    return lax.dot_general(a.astype(BF16), b.astype(BF16), (((1,), (1,)), ((), ())), preferred_element_type=F32)


def _split3(x):
    hi = x.astype(BF16)
    r = x - hi.astype(F32)
    mid = r.astype(BF16)
    lo = (r - mid.astype(F32)).astype(BF16)
    return hi, mid, lo


def _dot01(m01, x):
    hi, mid, lo = _split3(x)
    f = lambda p: jnp.dot(m01, p, preferred_element_type=F32)
    return f(hi) + f(mid) + f(lo)


def _silu(x):
    return x * jax.nn.sigmoid(x)


def _softplus(x):
    return jnp.maximum(x, 0.0) + jnp.log1p(jnp.exp(-jnp.abs(x)))


def _iota2(n, m):
    return lax.broadcasted_iota(jnp.int32, (n, m), 0), lax.broadcasted_iota(jnp.int32, (n, m), 1)


def _proj_kernel(x_ref, w_ref, o_ref):
    o_ref[...] = jnp.dot(x_ref[...].astype(BF16), w_ref[...], preferred_element_type=F32)


def _in_proj(x2d, w_cat, tm):
    t = x2d.shape[0]
    return pl.pallas_call(
        _proj_kernel,
        out_shape=jax.ShapeDtypeStruct((t, C_TOT), F32),
        grid=(t // tm,),
        in_specs=[pl.BlockSpec((tm, D_MODEL), lambda i: (i, 0)),
                  pl.BlockSpec((D_MODEL, C_TOT), lambda i: (0, 0))],
        out_specs=pl.BlockSpec((tm, C_TOT), lambda i: (i, 0)),
        compiler_params=pltpu.CompilerParams(dimension_semantics=("parallel",), vmem_limit_bytes=VMEM_LIMIT),
        name="in_proj",
    )(x2d, w_cat)


def _unit_lower_inverse(a_list, r, c, chunk_shift):
    b0 = min(4, chunk_shift)
    eye = jnp.where(r == c, 1.0, 0.0).astype(F32)
    blk = (r >> b0) == (c >> b0)
    xs = [jnp.where(blk, a, 0.0) for a in a_list]
    ts = [eye - x for x in xs]
    for _ in range(b0 - 1):
        xs = [_dot(x, x) for x in xs]
        ts = [t + _dot(t, x) for t, x in zip(ts, xs)]
    for lvl in range(b0, chunk_shift):
        m = ((r >> (lvl + 1)) == (c >> (lvl + 1))) & ((r >> lvl) != (c >> lvl))
        tmp = [_dot(t, jnp.where(m, a, 0.0)) for t, a in zip(ts, a_list)]
        ts = [t - _dot(x, t) for t, x in zip(ts, tmp)]
    return ts


def _gate_slabs(ba, arow, dtrow, chunk_shift):
    n = ba.shape[0]
    beta = jax.nn.sigmoid(ba)
    g = -jnp.exp(arow) * _softplus(ba + dtrow)
    r, c = _iota2(n, n)
    same = (r >> chunk_shift) == (c >> chunk_shift)
    ltri = jnp.where(same & (r >= c), 1.0, 0.0).astype(BF16)
    lall = jnp.where(same, 1.0, 0.0).astype(BF16)
    cs = _dot01(jnp.concatenate([ltri, lall], axis=0), g)
    return beta, cs[:n], cs[n:]


def _heads_prepare(y, beta_s, gc_s, egc_s, chunk_shift):
    n = y.shape[0]
    r, c = _iota2(n, n)
    same = (r >> chunk_shift) == (c >> chunk_shift)
    incl = same & (r >= c)
    strict = same & (r > c)
    gc_t = gc_s.T
    qs, ks, a_list, rhs, decays = [], [], [], [], []
    for h in range(N_HEADS):
        q = _l2norm(y[:, h * HEAD_DIM:(h + 1) * HEAD_DIM]) * (HEAD_DIM ** -0.5)
        k = _l2norm(y[:, D_A + h * HEAD_DIM:D_A + (h + 1) * HEAD_DIM])
        v = y[:, 2 * D_A + h * HEAD_DIM:2 * D_A + (h + 1) * HEAD_DIM]
        la = LANE_A + h
        beta_c = beta_s[:, LANE_B + h:LANE_B + h + 1]
        decay = jnp.exp(jnp.where(incl, gc_s[:, la:la + 1] - gc_t[la:la + 1, :], -jnp.inf))
        kb = k * beta_c
        a_list.append(jnp.where(strict, _dot_nt(kb, k) * decay, 0.0))
        rhs.append(jnp.concatenate([v * beta_c, kb * egc_s[:, la:la + 1]], axis=1))
        qs.append(q)
        ks.append(k)
        decays.append(decay)
    ts = _unit_lower_inverse(a_list, r, c, chunk_shift)
    sols = [_dot(t, x) for t, x in zip(ts, rhs)]
    qkds = [_dot_nt(q, k) * d for q, k, d in zip(qs, ks, decays)]
    return qs, ks, [s[:, :HEAD_DIM] for s in sols], [s[:, HEAD_DIM:] for s in sols], qkds


def _l2norm(x):
    return x * lax.rsqrt(jnp.sum(x * x, axis=-1, keepdims=True) + L2_EPS)


def _gated_rmsnorm(o, z, normw):
    o = o * lax.rsqrt(jnp.mean(o * o, axis=-1, keepdims=True) + RMS_EPS) * normw
    return o * _silu(z)


def _pool_out(s, cnt, p_g, poolw_g, pscale_g):
    d = s / cnt - p_g
    return _dot(d, poolw_g) * pscale_g


def _mixer_prompt_kernel(proj_ref, convw_ref, arow_ref, dtrow_ref, normw_ref, poolw_ref, pscale_ref,
                         mix_ref, sfin_ref, cc_ref, pc_ref, s_ref, *, lb):
    l = pl.program_id(1)
    n = lb
    csz = 1 << CHUNK_SHIFT

    @pl.when(l == 0)
    def _init():
        cc_ref[...] = jnp.zeros_like(cc_ref)
        pc_ref[...] = jnp.zeros_like(pc_ref)
        s_ref[...] = jnp.zeros_like(s_ref)

    u = proj_ref[0, :, 0:C_QKV]
    ext = jnp.concatenate([cc_ref[...], u], axis=0)
    cw = convw_ref[...]
    acc = ext * cw[CONV_W - 1:CONV_W, :]
    for d in range(1, CONV_W):
        acc = acc + pltpu.roll(ext, d, 0) * cw[CONV_W - 1 - d:CONV_W - d, :]
    cc_ref[...] = u[n - 8:n, :]
    y = _silu(acc[8:, :])

    beta_s, gc_s, gl_s = _gate_slabs(proj_ref[0, :, C_BA:C_TOT], arow_ref[...], dtrow_ref[...], CHUNK_SHIFT)
    egc_s = jnp.exp(gc_s)
    ekg_s = jnp.exp(gl_s - gc_s)
    egl_s = jnp.exp(gl_s)

    qs, ks, us, ws, qkds = _heads_prepare(y, beta_s, gc_s, egc_s, CHUNK_SHIFT)

    zero = jnp.zeros((csz, 2 * HEAD_DIM), F32)
    n_chunks = n // csz
    qps, ops, kns, egl_reps = [], [], [], []
    for h in range(N_HEADS):
        la = LANE_A + h
        wu = jnp.concatenate([ws[h], us[h]], axis=1)
        qw = _dot(qkds[h], wu)
        qps.append(qs[h] * egc_s[:, la:la + 1] - qw[:, :HEAD_DIM])
        ops.append(qw[:, HEAD_DIM:])
        kg_t = (ks[h] * ekg_s[:, la:la + 1]).T
        kn = []
        for ci in range(n_chunks):
            rows = slice(ci * csz, (ci + 1) * csz)
            pair = slice((ci // 2) * 2 * csz, (ci // 2 + 1) * 2 * csz)
            half = jnp.concatenate([wu[rows], zero] if ci % 2 == 0 else [zero, wu[rows]], axis=0)
            kn.append(_dot(kg_t[:, pair], half))
        kns.append(kn)
        egl_reps.append(jnp.broadcast_to(egl_s[:, la:la + 1], (n, HEAD_DIM)))

    states = [s_ref[h] for h in range(N_HEADS)]
    outs = [[] for _ in range(N_HEADS)]
    for ci in range(n_chunks):
        rows = slice(ci * csz, (ci + 1) * csz)
        for h in range(N_HEADS):
            s = states[h]
            outs[h].append(_dot(qps[h][rows], s) + ops[h][rows])
            kn = kns[h][ci]
            states[h] = (s * egl_reps[h][ci * csz:ci * csz + 1, :] - _dot(kn[:, :HEAD_DIM], s)) + kn[:, HEAD_DIM:]
    for h in range(N_HEADS):
        hs = slice(h * HEAD_DIM, (h + 1) * HEAD_DIM)
        s_ref[h] = states[h]
        o = jnp.concatenate(outs[h], axis=0)
        mix_ref[0, :, hs] = _gated_rmsnorm(o, proj_ref[0, :, C_Z + h * HEAD_DIM:C_Z + (h + 1) * HEAD_DIM],
                                           normw_ref[...])

    sfin_ref[0] = s_ref[...]

    p = proj_ref[0, :, C_P:C_P + D_B]
    extp = jnp.concatenate([pc_ref[...], p], axis=0)
    pc_ref[...] = p[n - 16:n, :]
    r, c = _iota2(n, n + 16)
    lag = r + 16 - c
    pos = l * n + lax.broadcasted_iota(jnp.int32, (n, 1), 0)
    for gi, w in enumerate(POOL_WINDOWS):
        gs = slice(gi * HEAD_DIM, (gi + 1) * HEAD_DIM)
        band = jnp.where((lag >= 0) & (lag < w), 1.0, 0.0).astype(BF16)
        cnt = jnp.minimum(pos + 1, w).astype(F32)
        mix_ref[0, :, D_A + gi * HEAD_DIM:D_A + (gi + 1) * HEAD_DIM] = _pool_out(
            _dot01(band, extp[:, gs]), cnt, p[:, gs], poolw_ref[gi], pscale_ref[:, gs])


def _mixer_prompt(proj, conv_w, arow, dtrow, normw, poolw, pscale, lb):
    b, seq, _ = proj.shape
    const2 = lambda i, j: (0, 0)
    return pl.pallas_call(
        functools.partial(_mixer_prompt_kernel, lb=lb),
        out_shape=(jax.ShapeDtypeStruct((b, seq, D_MODEL), F32),
                   jax.ShapeDtypeStruct((b, N_HEADS, HEAD_DIM, HEAD_DIM), F32)),
        grid=(b, seq // lb),
        in_specs=[pl.BlockSpec((1, lb, C_TOT), lambda i, j: (i, j, 0)),
                  pl.BlockSpec((CONV_W, C_QKV), const2),
                  pl.BlockSpec((1, 128), const2),
                  pl.BlockSpec((1, 128), const2),
                  pl.BlockSpec((1, HEAD_DIM), const2),
                  pl.BlockSpec((N_GROUPS, HEAD_DIM, HEAD_DIM), lambda i, j: (0, 0, 0)),
                  pl.BlockSpec((1, D_B), const2)],
        out_specs=(pl.BlockSpec((1, lb, D_MODEL), lambda i, j: (i, j, 0)),
                   pl.BlockSpec((1, N_HEADS, HEAD_DIM, HEAD_DIM), lambda i, j: (i, 0, 0, 0))),
        scratch_shapes=[pltpu.VMEM((8, C_QKV), F32), pltpu.VMEM((16, D_B), F32),
                        pltpu.VMEM((N_HEADS, HEAD_DIM, HEAD_DIM), F32)],
        compiler_params=pltpu.CompilerParams(dimension_semantics=("parallel", "arbitrary"),
                                             vmem_limit_bytes=VMEM_LIMIT),
        name="mixer_prompt",
    )(proj, conv_w, arow, dtrow, normw, poolw, pscale)


def _mixer_sample_kernel(proj_ref, cst_ref, pst_ref, sin_ref, convw_ref, arow_ref, dtrow_ref, normw_ref,
                         poolw_ref, pscale_ref, mix_ref, sout_ref, *, ns, seq, start):
    n = ns * seq
    sshift = seq.bit_length() - 1
    rowi = lax.broadcasted_iota(jnp.int32, (n, 1), 0)
    tpos = rowi & (seq - 1)

    u = proj_ref[:, 0:C_QKV]
    st = cst_ref[...]
    cw = convw_ref[...]
    acc = u * cw[CONV_W - 1:CONV_W, :]
    for d in range(1, CONV_W):
        term = jnp.where(tpos >= d, pltpu.roll(u, d, 0), pltpu.roll(st, n - seq + d, 0))
        acc = acc + term * cw[CONV_W - 1 - d:CONV_W - d, :]
    y = _silu(acc)

    beta_s, gc_s, gl_s = _gate_slabs(proj_ref[:, C_BA:C_TOT], arow_ref[...], dtrow_ref[...], sshift)
    egc_s = jnp.exp(gc_s)
    ekg_s = jnp.exp(gl_s - gc_s)
    egl_s = jnp.exp(gl_s)
    qs, ks, us, ws_, qkds = _heads_prepare(y, beta_s, gc_s, egc_s, sshift)

    for h in range(N_HEADS):
        hs = slice(h * HEAD_DIM, (h + 1) * HEAD_DIM)
        la = LANE_A + h
        u_, w_, qkd = us[h], ws_[h], qkds[h]
        qg = qs[h] * egc_s[:, la:la + 1]
        kg_t = (ks[h] * ekg_s[:, la:la + 1]).T
        egl_rep = jnp.broadcast_to(egl_s[:, la:la + 1], (n, HEAD_DIM))
        ws_w, ws_q = [], []
        for si in range(ns):
            rows = slice(si * seq, (si + 1) * seq)
            ws = _dot(jnp.concatenate([w_[rows], qg[rows]], axis=0), sin_ref[si, h])
            ws_w.append(ws[:seq])
            ws_q.append(ws[seq:])
        vn = u_ - jnp.concatenate(ws_w, axis=0)
        o = jnp.concatenate(ws_q, axis=0) + _dot(qkd, vn)
        for si in range(ns):
            vmask = jnp.where((rowi >> sshift) == si, vn, 0.0)
            sout_ref[si, h] = sin_ref[si, h] * egl_rep[si * seq:si * seq + 1, :] + _dot(kg_t, vmask)
        mix_ref[:, hs] = _gated_rmsnorm(o, proj_ref[:, C_Z + h * HEAD_DIM:C_Z + (h + 1) * HEAD_DIM], normw_ref[...])

    p = proj_ref[:, C_P:C_P + D_B]
    pst = pst_ref[...]
    r, c = _iota2(n, n)
    band_new_base = ((r >> sshift) == (c >> sshift)) & (r >= c)
    r2, c2 = _iota2(n, ns * 16)
    same2 = (r2 >> sshift) == (c2 >> 4)
    t2 = r2 & (seq - 1)
    j2 = c2 & 15
    pos = start + tpos
    for gi, w in enumerate(POOL_WINDOWS):
        gs = slice(gi * HEAD_DIM, (gi + 1) * HEAD_DIM)
        band_new = jnp.where(band_new_base & ((r - c) < w), 1.0, 0.0).astype(BF16)
        band_st = jnp.where(same2 & (j2 >= 17 + t2 - w), 1.0, 0.0).astype(BF16)
        s = _dot01(band_new, p[:, gs]) + _dot01(band_st, pst[:, gs])
        cnt = jnp.minimum(pos + 1, w).astype(F32)
        mix_ref[:, D_A + gi * HEAD_DIM:D_A + (gi + 1) * HEAD_DIM] = _pool_out(
            s, cnt, p[:, gs], poolw_ref[gi], pscale_ref[:, gs])


def _mixer_sample(proj, cst, pst, sin, conv_w, arow, dtrow, normw, poolw, pscale, ns, seq, start):
    t = proj.shape[0]
    nb = t // seq
    n = ns * seq
    const1 = lambda i: (0, 0)
    return pl.pallas_call(
        functools.partial(_mixer_sample_kernel, ns=ns, seq=seq, start=start),
        out_shape=(jax.ShapeDtypeStruct((t, D_MODEL), F32),
                   jax.ShapeDtypeStruct((nb, N_HEADS, HEAD_DIM, HEAD_DIM), F32)),
        grid=(nb // ns,),
        in_specs=[pl.BlockSpec((n, C_TOT), lambda i: (i, 0)),
                  pl.BlockSpec((n, C_QKV), lambda i: (i, 0)),
                  pl.BlockSpec((ns * 16, D_B), lambda i: (i, 0)),
                  pl.BlockSpec((ns, N_HEADS, HEAD_DIM, HEAD_DIM), lambda i: (i, 0, 0, 0)),
                  pl.BlockSpec((CONV_W, C_QKV), const1),
                  pl.BlockSpec((1, 128), const1),
                  pl.BlockSpec((1, 128), const1),
                  pl.BlockSpec((1, HEAD_DIM), const1),
                  pl.BlockSpec((N_GROUPS, HEAD_DIM, HEAD_DIM), lambda i: (0, 0, 0)),
                  pl.BlockSpec((1, D_B), const1)],
        out_specs=(pl.BlockSpec((n, D_MODEL), lambda i: (i, 0)),
                   pl.BlockSpec((ns, N_HEADS, HEAD_DIM, HEAD_DIM), lambda i: (i, 0, 0, 0))),
        compiler_params=pltpu.CompilerParams(dimension_semantics=("parallel",), vmem_limit_bytes=VMEM_LIMIT),
        name="mixer_sample",
    )(proj, cst, pst, sin, conv_w, arow, dtrow, normw, poolw, pscale)


def _layer_norm(x, g, b):
    mu = jnp.mean(x, axis=-1, keepdims=True)
    xc = x - mu
    var = jnp.mean(xc * xc, axis=-1, keepdims=True)
    return xc * lax.rsqrt(var + LN_EPS) * g + b


def _outproj_router_kernel(mix_ref, x_ref, wout_ref, g1_ref, b1_ref, wrh_ref, wrl_ref, br_ref,
                           h_ref, sel_ref, gw_ref, cnt_ref, carry_ref):
    @pl.when(pl.program_id(0) == 0)
    def _init():
        carry_ref[...] = jnp.zeros_like(carry_ref)

    h = _layer_norm(ALPHA * x_ref[...] + _dot(mix_ref[...], wout_ref[...]), g1_ref[...], b1_ref[...])
    h_ref[...] = h
    hh, hm, _ = _split3(h)
    f = lambda a, b: jnp.dot(a, b, preferred_element_type=F32)
    logits = f(hh, wrh_ref[...]) + (f(hm, wrh_ref[...]) + f(hh, wrl_ref[...])) + br_ref[...]
    tm = logits.shape[0]
    lane = lax.broadcasted_iota(jnp.int32, (tm, 128), 1)
    big = jnp.int32(1 << 20)
    neg = -jnp.inf
    gmask = (lane >= N_EXPERTS) & (lane < N_EXPERTS + N_GROUPS)
    lg = jnp.where(gmask, logits, neg)
    gmax = jnp.max(lg, axis=1, keepdims=True)
    gidx = jnp.min(jnp.where(lg == gmax, lane - N_EXPERTS, big), axis=1, keepdims=True)
    pg = 1.0 / jnp.sum(jnp.where(gmask, jnp.exp(logits - gmax), 0.0), axis=1, keepdims=True)
    emask = (lane < N_EXPERTS) & ((lane >> 3) == gidx)
    le = jnp.where(emask, logits, neg)
    v1 = jnp.max(le, axis=1, keepdims=True)
    i1 = jnp.min(jnp.where((le == v1) & emask, lane, big), axis=1, keepdims=True)
    emask2 = emask & (lane != i1)
    le2 = jnp.where(emask2, logits, neg)
    v2 = jnp.max(le2, axis=1, keepdims=True)
    i2 = jnp.min(jnp.where((le2 == v2) & emask2, lane, big), axis=1, keepdims=True)
    e2 = jnp.exp(v2 - v1)
    den = 1.0 + e2
    gw_ref[...] = jnp.where(lane == 0, (1.0 / den) * pg, jnp.where(lane == 1, (e2 / den) * pg, 0.0))
    onehot = jnp.where((lane == i1) | (lane == i2), 1.0, 0.0)
    r, c = _iota2(tm, tm)
    before = jnp.dot(jnp.where(r > c, 1.0, 0.0).astype(BF16), onehot.astype(BF16), preferred_element_type=F32)
    before = before + carry_ref[...]
    r1 = jnp.sum(jnp.where(lane == i1, before, 0.0), axis=1, keepdims=True).astype(jnp.int32)
    r2 = jnp.sum(jnp.where(lane == i2, before, 0.0), axis=1, keepdims=True).astype(jnp.int32)
    carry_ref[...] += jnp.sum(onehot, axis=0, keepdims=True)
    cnt_ref[...] = carry_ref[...].astype(jnp.int32)
    sel_ref[...] = jnp.where(lane == 0, i1, jnp.where(lane == 1, i2, jnp.where(lane == 2, r1,
                                                                               jnp.where(lane == 3, r2, 0))))


def _outproj_router(mix, x2d, wout, g1, b1, wrh, wrl, br, tm):
    t = x2d.shape[0]
    row = lambda i: (i, 0)
    const = lambda i: (0, 0)
    return pl.pallas_call(
        _outproj_router_kernel,
        out_shape=(jax.ShapeDtypeStruct((t, D_MODEL), F32), jax.ShapeDtypeStruct((t, 128), jnp.int32),
                   jax.ShapeDtypeStruct((t, 128), F32), jax.ShapeDtypeStruct((1, 128), jnp.int32)),
        grid=(t // tm,),
        in_specs=[pl.BlockSpec((tm, D_MODEL), row), pl.BlockSpec((tm, D_MODEL), row),
                  pl.BlockSpec((D_MODEL, D_MODEL), const), pl.BlockSpec((1, D_MODEL), const),
                  pl.BlockSpec((1, D_MODEL), const), pl.BlockSpec((D_MODEL, 128), const),
                  pl.BlockSpec((D_MODEL, 128), const), pl.BlockSpec((1, 128), const)],
        out_specs=(pl.BlockSpec((tm, D_MODEL), row), pl.BlockSpec((tm, 128), row), pl.BlockSpec((tm, 128), row),
                   pl.BlockSpec((1, 128), const)),
        scratch_shapes=[pltpu.VMEM((1, 128), F32)],
        compiler_params=pltpu.CompilerParams(dimension_semantics=("arbitrary",), vmem_limit_bytes=VMEM_LIMIT),
        name="outproj_router",
    )(mix, x2d, wout, g1, b1, wrh, wrl, br)


ROW_SLAB = D_MODEL // 128
EXPERT_TILE = 256
ROUTE_TILE = 256


def _to_slabs(ref, x, n):
    for c in range(ROW_SLAB):
        ref[pl.ds(c, n, stride=ROW_SLAB), :] = x[:, c * 128:(c + 1) * 128]


def _from_slabs(ref, n):
    return jnp.concatenate([ref[pl.ds(c, n, stride=ROW_SLAB), :] for c in range(ROW_SLAB)], axis=1)


def _slab(ref, row):
    if isinstance(row, int):
        return ref.at[pl.ds(row * ROW_SLAB, ROW_SLAB)]
    return ref.at[pl.ds(pl.multiple_of(row * ROW_SLAB, ROW_SLAB), ROW_SLAB)]


def _dispatch_kernel(pad_start_ref, pad_cnt_ref, tail_ref, h_ref, pos_ref, xs_ref, stage_ref, sem, zsem, tsem,
                     *, td, tmx):
    @pl.when(pl.program_id(0) == 0)
    def _zero_unused_rows():
        stage_ref[...] = jnp.zeros_like(stage_ref)
        def zcopy(row, n_rows):
            return pltpu.make_async_copy(
                stage_ref.at[pl.ds(0, n_rows * ROW_SLAB)],
                xs_ref.at[pl.ds(pl.multiple_of(row * ROW_SLAB, ROW_SLAB), n_rows * ROW_SLAB)], zsem)

        tile_rows = tmx * ROW_SLAB
        tcopy = lambda tile: pltpu.make_async_copy(
            stage_ref.at[pl.ds(0, tile_rows)],
            xs_ref.at[pl.ds(pl.multiple_of(tile * tile_rows, tile_rows), tile_rows)], tsem)

        def pad_pieces(e, start_not_wait):
            start = pad_start_ref[e]
            n = pad_cnt_ref[e]
            piece = tmx // 2
            while piece >= 1:
                @pl.when((n & piece) != 0)
                def _(piece=piece):
                    cp = zcopy(start + (n & ~(2 * piece - 1)), piece)
                    cp.start() if start_not_wait else cp.wait()
                piece //= 2

        def start_pads(e, carry):
            pad_pieces(e, True)
            return carry

        def wait_pads(e, carry):
            pad_pieces(e, False)
            return carry

        lax.fori_loop(0, N_EXPERTS, start_pads, 0)

        def tail_start(r, carry):
            tcopy(tail_ref[0] + r).start()
            return carry

        lax.fori_loop(0, tail_ref[1], tail_start, 0)
        lax.fori_loop(0, N_EXPERTS, wait_pads, 0)

        def tail_wait(r, carry):
            tcopy(0).wait()
            return carry

        lax.fori_loop(0, tail_ref[1], tail_wait, 0)

    _to_slabs(stage_ref, h_ref[...], td)
    copies = []
    for t in range(td):
        for k in range(2):
            cp = pltpu.make_async_copy(_slab(stage_ref, t), _slab(xs_ref, pos_ref[0, 0, 2 * t + k]), sem)
            cp.start(priority=k)
            copies.append(cp)
    for cp in copies:
        cp.wait()


def _dispatch(h, pos3, pad_start, pad_cnt, tail, n_rows, td, tmx):
    t = h.shape[0]
    assert td >= tmx
    return pl.pallas_call(
        functools.partial(_dispatch_kernel, td=td, tmx=tmx),
        out_shape=jax.ShapeDtypeStruct((n_rows * ROW_SLAB, 128), F32),
        grid_spec=pltpu.PrefetchScalarGridSpec(
            num_scalar_prefetch=3,
            grid=(t // td,),
            in_specs=[pl.BlockSpec((td, D_MODEL), lambda i, *_: (i, 0)),
                      pl.BlockSpec((1, 1, 2 * td), lambda i, *_: (i, 0, 0), memory_space=pltpu.SMEM)],
            out_specs=pl.BlockSpec(memory_space=pl.ANY),
            scratch_shapes=[pltpu.VMEM((td * ROW_SLAB, 128), F32),
                            pltpu.SemaphoreType.DMA, pltpu.SemaphoreType.DMA, pltpu.SemaphoreType.DMA]),
        compiler_params=pltpu.CompilerParams(dimension_semantics=("arbitrary",), vmem_limit_bytes=VMEM_LIMIT),
        name="moe_dispatch",
    )(pad_start, pad_cnt, tail, h, pos3)


def _experts_kernel(tile_idx_ref, tile_e_ref, tile_ok_ref, x_ref, wg_ref, wu_ref, wd_ref, o_ref, *, tmx):
    ok = tile_ok_ref[pl.program_id(0)] != 0

    @pl.when(ok)
    def _compute():
        x = _from_slabs(x_ref, tmx).astype(BF16)
        a = jnp.dot(x, wg_ref[0].astype(BF16), preferred_element_type=F32)
        b = jnp.dot(x, wu_ref[0].astype(BF16), preferred_element_type=F32)
        act = (_silu(a) * b).astype(BF16)
        _to_slabs(o_ref, jnp.dot(act, wd_ref[0].astype(BF16), preferred_element_type=F32), tmx)

    @pl.when(jnp.logical_not(ok))
    def _unused_tile():
        o_ref[...] = jnp.zeros_like(o_ref)


def _experts(xs, tile_idx, tile_e, tile_ok, wg, wu, wd, tmx):
    nt = tile_idx.shape[0]
    rows = lambda j, ti, te, ok: (ti[j], 0)
    wsel = lambda j, ti, te, ok: (te[j], 0, 0)
    own = lambda j, ti, te, ok: (j, 0)
    return pl.pallas_call(
        functools.partial(_experts_kernel, tmx=tmx),
        out_shape=jax.ShapeDtypeStruct(xs.shape, F32),
        grid_spec=pltpu.PrefetchScalarGridSpec(
            num_scalar_prefetch=3,
            grid=(nt,),
            in_specs=[pl.BlockSpec((tmx * ROW_SLAB, 128), rows),
                      pl.BlockSpec((1, D_MODEL, D_EXPERT), wsel),
                      pl.BlockSpec((1, D_MODEL, D_EXPERT), wsel),
                      pl.BlockSpec((1, D_EXPERT, D_MODEL), wsel)],
            out_specs=pl.BlockSpec((tmx * ROW_SLAB, 128), own)),
        compiler_params=pltpu.CompilerParams(dimension_semantics=("arbitrary",), vmem_limit_bytes=VMEM_LIMIT),
        name="moe_experts",
    )(tile_idx, tile_e, tile_ok, xs, wg, wu, wd)


def _combine_kernel(h_ref, gw_ref, pos_ref, g2_ref, b2_ref, os_ref, y_ref, stage_ref, sem, *, td):
    copies = []
    for t in range(td):
        for k in range(2):
            cp = pltpu.make_async_copy(_slab(os_ref, pos_ref[0, 0, 2 * t + k]), _slab(stage_ref.at[k], t), sem)
            cp.start(priority=k)
            copies.append(cp)
    for cp in copies:
        cp.wait()
    gw = gw_ref[...]
    moe = gw[:, 0:1] * _from_slabs(stage_ref.at[0], td) + gw[:, 1:2] * _from_slabs(stage_ref.at[1], td)
    y_ref[...] = _layer_norm(ALPHA * h_ref[...] + moe, g2_ref[...], b2_ref[...])


def _combine(h, gw, pos3, g2, b2, os, td):
    t = h.shape[0]
    row = lambda i: (i, 0)
    const = lambda i: (0, 0)
    return pl.pallas_call(
        functools.partial(_combine_kernel, td=td),
        out_shape=jax.ShapeDtypeStruct((t, D_MODEL), F32),
        grid=(t // td,),
        in_specs=[pl.BlockSpec((td, D_MODEL), row), pl.BlockSpec((td, 128), row),
                  pl.BlockSpec((1, 1, 2 * td), lambda i: (i, 0, 0), memory_space=pltpu.SMEM),
                  pl.BlockSpec((1, D_MODEL), const), pl.BlockSpec((1, D_MODEL), const),
                  pl.BlockSpec(memory_space=pl.ANY)],
        out_specs=pl.BlockSpec((td, D_MODEL), row),
        scratch_shapes=[pltpu.VMEM((2, td * ROW_SLAB, 128), F32), pltpu.SemaphoreType.DMA],
        compiler_params=pltpu.CompilerParams(dimension_semantics=("arbitrary",), vmem_limit_bytes=VMEM_LIMIT),
        name="moe_combine",
    )(h, gw, pos3, g2, b2, os)


def _route_plan(sel, cnt, t, tmx, td):
    i32 = jnp.int32
    counts = cnt[0, :N_EXPERTS]
    padded = ((counts + tmx - 1) // tmx) * tmx
    ex = jnp.arange(N_EXPERTS, dtype=i32)
    ends = jnp.sum(jnp.where(ex[None, :] <= ex[:, None], padded[None, :], 0), axis=1).astype(i32)
    offs = ends - padded
    pos = (jnp.sum(jnp.where(sel[:, 0:2, None] == ex, offs, 0), axis=-1) + sel[:, 2:4]).astype(i32)
    nt = 2 * t // tmx + N_EXPERTS
    n_used = ends[-1] // tmx
    tile = jnp.arange(nt, dtype=i32)
    tile_idx = jnp.minimum(tile, jnp.maximum(n_used - 1, 0))
    tile_e = jnp.minimum(jnp.sum((ends[None, :] <= (tile_idx * tmx)[:, None]).astype(i32), axis=1), N_EXPERTS - 1)
    tile_ok = (tile < n_used).astype(i32)
    tail = jnp.stack([n_used, nt - n_used]).astype(i32)
    return (pos.reshape(t // td, 1, 2 * td), tile_idx, tile_e, tile_ok, (offs + counts).astype(i32),
            (padded - counts).astype(i32), tail, nt * tmx)


def _tile(t, want):
    tm = min(want, t)
    while t % tm:
        tm //= 2
    return tm


def _prep_weights(w_in, conv_w, a_log, dt_bias, gdn_norm_w, pool_w, pool_scale, w_out, ln1_g, ln1_b,
                  w_rg, b_rg, w_re, b_re, w_gate, w_up, w_down, ln2_g, ln2_b):
    col_b = 4 * D_A
    col_p = 4 * D_A + 2 * N_HEADS
    w_cat = jnp.concatenate([w_in[:, :col_b], w_in[:, col_p:], w_in[:, col_b:col_p],
                             jnp.zeros((D_MODEL, 128 - 2 * N_HEADS), w_in.dtype)], axis=1).astype(BF16)
    lane_pad = lambda v, off: jnp.zeros((1, 128), F32).at[0, off:off + v.shape[0]].set(v.astype(F32))
    w_r = jnp.concatenate([w_re, w_rg, jnp.zeros((D_MODEL, 128 - N_EXPERTS - N_GROUPS), F32)], axis=1)
    wrh = w_r.astype(BF16)
    wrl = (w_r - wrh.astype(F32)).astype(BF16)
    b_r = jnp.zeros((1, 128), F32).at[0, :N_EXPERTS].set(b_re).at[0, N_EXPERTS:N_EXPERTS + N_GROUPS].set(b_rg)
    return dict(
        w_cat=w_cat, conv_w=conv_w, arow=lane_pad(a_log, LANE_A), dtrow=lane_pad(dt_bias, LANE_A),
        normw=gdn_norm_w.reshape(1, HEAD_DIM), poolw=pool_w.astype(BF16), pscale=pool_scale.reshape(1, D_B),
        wout=w_out.astype(BF16), g1=ln1_g.reshape(1, D_MODEL), b1=ln1_b.reshape(1, D_MODEL),
        wrh=wrh, wrl=wrl, br=b_r,
        wg=w_gate.reshape(N_EXPERTS, D_MODEL, D_EXPERT), wu=w_up.reshape(N_EXPERTS, D_MODEL, D_EXPERT),
        wd=w_down.reshape(N_EXPERTS, D_EXPERT, D_MODEL),
        g2=ln2_g.reshape(1, D_MODEL), b2=ln2_b.reshape(1, D_MODEL))


def _post_mixer(mix2d, x2d, p):
    t = x2d.shape[0]
    h, sel, gw, cnt = _outproj_router(mix2d, x2d, p["wout"], p["g1"], p["b1"], p["wrh"], p["wrl"], p["br"],
                                      _tile(t, 512))
    td = _tile(t, ROUTE_TILE)
    pos3, tile_idx, tile_e, tile_ok, pad_start, pad_cnt, tail, n_rows = _route_plan(sel, cnt, t, EXPERT_TILE, td)
    xs = _dispatch(h, pos3, pad_start, pad_cnt, tail, n_rows, td, EXPERT_TILE)
    os = _experts(xs, tile_idx, tile_e, tile_ok, p["wg"], p["wu"], p["wd"], EXPERT_TILE)
    return _combine(h, gw, pos3, p["g2"], p["b2"], os, td)


def _layer_prompt(x, p, lb=256):
    b, seq, _ = x.shape
    x2d = x.reshape(b * seq, D_MODEL)
    proj = _in_proj(x2d, p["w_cat"], _tile(b * seq, 512)).reshape(b, seq, C_TOT)
    mix, s_fin = _mixer_prompt(proj, p["conv_w"], p["arow"], p["dtrow"], p["normw"], p["poolw"], p["pscale"],
                               min(lb, seq))
    y = _post_mixer(mix.reshape(b * seq, D_MODEL), x2d, p).reshape(b, seq, D_MODEL)
    conv_new = proj[:, seq - (CONV_W - 1):, 0:C_QKV]
    pool_new = proj[:, seq - POOL_BUF:, C_P:C_P + D_B]
    return y, s_fin, conv_new, pool_new


def _layer_sample(x, s0, conv0, pool0, start, p, ns=16):
    b, seq, _ = x.shape
    x2d = x.reshape(b * seq, D_MODEL)
    proj = _in_proj(x2d, p["w_cat"], _tile(b * seq, 512))
    cst = jnp.pad(conv0, ((0, 0), (seq - (CONV_W - 1), 0), (0, 0))).reshape(b * seq, C_QKV)
    pst = jnp.pad(pool0, ((0, 0), (1, 0), (0, 0))).reshape(b * 16, D_B)
    mix, s_new = _mixer_sample(proj, cst, pst, s0, p["conv_w"], p["arow"], p["dtrow"], p["normw"], p["poolw"],
                               p["pscale"], min(ns, b), seq, start)
    y = _post_mixer(mix, x2d, p).reshape(b, seq, D_MODEL)
    proj3 = proj.reshape(b, seq, C_TOT)
    conv_new = proj3[:, seq - (CONV_W - 1):, 0:C_QKV]
    pool_new = jnp.concatenate([pool0[:, seq:, :], proj3[:, :, C_P:C_P + D_B]], axis=1)
    return y, s_new, conv_new, pool_new


def kernel(x_prompt, x_sample, state_delta, state_conv, state_pool, w_in, conv_w, a_log, dt_bias, gdn_norm_w,
           pool_w, pool_scale, w_out, ln1_g, ln1_b, w_rg, b_rg, w_re, b_re, w_gate, w_up, w_down, ln2_g, ln2_b):
    depth = w_in.shape[0]
    past_len = 16384
    yp, ys = x_prompt, x_sample
    outs = [[] for _ in range(6)]
    for l in range(depth):
        p = _prep_weights(w_in[l], conv_w[l], a_log[l], dt_bias[l], gdn_norm_w[l], pool_w[l], pool_scale[l],
                          w_out[l], ln1_g[l], ln1_b[l], w_rg[l], b_rg[l], w_re[l], b_re[l], w_gate[l], w_up[l],
                          w_down[l], ln2_g[l], ln2_b[l])
        yp, dp, cp, pp = _layer_prompt(yp, p)
        ys, ds, cs, ps = _layer_sample(ys, state_delta[l], state_conv[l], state_pool[l], past_len, p)
        for lst, v in zip(outs, (dp, cp, pp, ds, cs, ps)):
            lst.append(v)
    return (yp, ys) + tuple(jnp.stack(v) for v in outs)
```

```python
import functools
import math

import jax
import jax.numpy as jnp
from jax import lax
from jax.experimental import pallas as pl
from jax.experimental.pallas import tpu as pltpu

F32 = jnp.float32
BF16 = jnp.bfloat16

D_MODEL = 1024
D_A = 512
D_B = 512
HEAD_DIM = 128
N_HEADS = 4
CONV_W = 4
CHUNK_SHIFT = 6
GDN_BLOCK = 128
POOL_WINDOWS = (2, 4, 8, 16)
POOL_BUF = 15
N_GROUPS = 4
E_PER_GROUP = 8
N_EXPERTS = N_GROUPS * E_PER_GROUP
D_EXPERT = 256
ALPHA = 2.0 ** 0.25
LN_EPS = 1e-5
RMS_EPS = 1e-6
L2_EPS = 1e-6

C_QKV = 3 * D_A
C_Z = 3 * D_A
C_P = 4 * D_A
C_BA = 4 * D_A + D_B
C_TOT = C_BA + 128
LANE_B = 0
LANE_A = N_HEADS

VMEM_LIMIT = 56 * 1024 * 1024


def _dot(a, b):
    return jnp.dot(a.astype(BF16), b.astype(BF16), preferred_element_type=F32)


def _dot_nt(a, b):
    return lax.dot_general(a.astype(BF16), b.astype(BF16), (((1,), (1,)), ((), ())), preferred_element_type=F32)


def _split3(x):
    hi = x.astype(BF16)
    r = x - hi.astype(F32)
    mid = r.astype(BF16)
    lo = (r - mid.astype(F32)).astype(BF16)
    return hi, mid, lo


def _dot01(m01, x):
    hi, mid, lo = _split3(x)
    f = lambda p: jnp.dot(m01, p, preferred_element_type=F32)
    return f(hi) + f(mid) + f(lo)


def _silu(x):
    return x * jax.nn.sigmoid(x)


def _softplus(x):
    return jnp.maximum(x, 0.0) + jnp.log1p(jnp.exp(-jnp.abs(x)))


def _iota2(n, m):
    return lax.broadcasted_iota(jnp.int32, (n, m), 0), lax.broadcasted_iota(jnp.int32, (n, m), 1)


def _proj_kernel(x_ref, w_ref, o_ref):
    o_ref[...] = jnp.dot(x_ref[...].astype(BF16), w_ref[...], preferred_element_type=F32)


def _in_proj(x2d, w_cat, tm):
    t = x2d.shape[0]
    return pl.pallas_call(
        _proj_kernel,
        out_shape=jax.ShapeDtypeStruct((t, C_TOT), F32),
        grid=(t // tm,),
        in_specs=[pl.BlockSpec((tm, D_MODEL), lambda i: (i, 0)),
                  pl.BlockSpec((D_MODEL, C_TOT), lambda i: (0, 0))],
        out_specs=pl.BlockSpec((tm, C_TOT), lambda i: (i, 0)),
        compiler_params=pltpu.CompilerParams(dimension_semantics=("parallel",), vmem_limit_bytes=VMEM_LIMIT),
        name="in_proj",
    )(x2d, w_cat)


def _unit_lower_inverse(a_list, r, c, chunk_shift):
    b0 = min(4, chunk_shift)
    eye = jnp.where(r == c, 1.0, 0.0).astype(F32)
    blk = (r >> b0) == (c >> b0)
    xs = [jnp.where(blk, a, 0.0) for a in a_list]
    ts = [eye - x for x in xs]
    for _ in range(b0 - 1):
        xs = [_dot(x, x) for x in xs]
        ts = [t + _dot(t, x) for t, x in zip(ts, xs)]
    for lvl in range(b0, chunk_shift):
        m = ((r >> (lvl + 1)) == (c >> (lvl + 1))) & ((r >> lvl) != (c >> lvl))
        tmp = [_dot(t, jnp.where(m, a, 0.0)) for t, a in zip(ts, a_list)]
        ts = [t - _dot(x, t) for t, x in zip(ts, tmp)]
    return ts


def _gate_slabs(ba, arow, dtrow, chunk_shift):
    n = ba.shape[0]
    beta = jax.nn.sigmoid(ba)
    g = -jnp.exp(arow) * _softplus(ba + dtrow)
    r, c = _iota2(n, n)
    same = (r >> chunk_shift) == (c >> chunk_shift)
    ltri = jnp.where(same & (r >= c), 1.0, 0.0).astype(BF16)
    lall = jnp.where(same, 1.0, 0.0).astype(BF16)
    cs = _dot01(jnp.concatenate([ltri, lall], axis=0), g)
    return beta, cs[:n], cs[n:]


def _heads_prepare(y, beta_s, gc_s, egc_s, chunk_shift):
    n = y.shape[0]
    nb = n // GDN_BLOCK
    r, c = _iota2(GDN_BLOCK, GDN_BLOCK)
    same = (r >> chunk_shift) == (c >> chunk_shift)
    incl = same & (r >= c)
    strict = same & (r > c)
    gc_t = gc_s.T
    qs, ks, a_list, rhs, decays = [], [], [], [], []
    for h in range(N_HEADS):
        q = _l2norm(y[:, h * HEAD_DIM:(h + 1) * HEAD_DIM]) * (HEAD_DIM ** -0.5)
        k = _l2norm(y[:, D_A + h * HEAD_DIM:D_A + (h + 1) * HEAD_DIM])
        v = y[:, 2 * D_A + h * HEAD_DIM:2 * D_A + (h + 1) * HEAD_DIM]
        la = LANE_A + h
        beta_c = beta_s[:, LANE_B + h:LANE_B + h + 1]
        kb = k * beta_c
        rhs_h = jnp.concatenate([v * beta_c, kb * egc_s[:, la:la + 1]], axis=1)
        for bi in range(nb):
            blk = slice(bi * GDN_BLOCK, (bi + 1) * GDN_BLOCK)
            decay = jnp.exp(jnp.where(incl, gc_s[blk, la:la + 1] - gc_t[la:la + 1, blk], -jnp.inf))
            a_list.append(jnp.where(strict, _dot_nt(kb[blk], k[blk]) * decay, 0.0))
            rhs.append(rhs_h[blk])
            decays.append(decay)
        qs.append(q)
        ks.append(k)
    ts = _unit_lower_inverse(a_list, r, c, chunk_shift)
    sols = [_dot(t, x) for t, x in zip(ts, rhs)]
    us, ws, qkds = [], [], []
    for h in range(N_HEADS):
        sol = jnp.concatenate(sols[h * nb:(h + 1) * nb], axis=0) if nb > 1 else sols[h]
        us.append(sol[:, :HEAD_DIM])
        ws.append(sol[:, HEAD_DIM:])
        qkds.append([_dot_nt(qs[h][bi * GDN_BLOCK:(bi + 1) * GDN_BLOCK], ks[h][bi * GDN_BLOCK:(bi + 1) * GDN_BLOCK])
                     * decays[h * nb + bi] for bi in range(nb)])
    return qs, ks, us, ws, qkds


def _l2norm(x):
    return x * lax.rsqrt(jnp.sum(x * x, axis=-1, keepdims=True) + L2_EPS)


def _gated_rmsnorm(o, z, normw):
    o = o * lax.rsqrt(jnp.mean(o * o, axis=-1, keepdims=True) + RMS_EPS) * normw
    return o * _silu(z)


def _pool_out(s, cnt, p_g, poolw_g, pscale_g):
    d = s / cnt - p_g
    return _dot(d, poolw_g) * pscale_g


def _mixer_prompt_kernel(proj_ref, convw_ref, arow_ref, dtrow_ref, normw_ref, poolw_ref, pscale_ref,
                         mix_ref, sfin_ref, cc_ref, pc_ref, s_ref, *, lb):
    l = pl.program_id(1)
    n = lb
    csz = 1 << CHUNK_SHIFT

    @pl.when(l == 0)
    def _init():
        cc_ref[...] = jnp.zeros_like(cc_ref)
        pc_ref[...] = jnp.zeros_like(pc_ref)
        s_ref[...] = jnp.zeros_like(s_ref)

    u = proj_ref[0, :, 0:C_QKV]
    ext = jnp.concatenate([cc_ref[...], u], axis=0)
    cw = convw_ref[...]
    acc = ext * cw[CONV_W - 1:CONV_W, :]
    for d in range(1, CONV_W):
        acc = acc + pltpu.roll(ext, d, 0) * cw[CONV_W - 1 - d:CONV_W - d, :]
    cc_ref[...] = u[n - 8:n, :]
    y = _silu(acc[8:, :])

    beta_s, gc_s, gl_s = _gate_slabs(proj_ref[0, :, C_BA:C_TOT], arow_ref[...], dtrow_ref[...], CHUNK_SHIFT)
    egc_s = jnp.exp(gc_s)
    ekg_s = jnp.exp(gl_s - gc_s)
    egl_s = jnp.exp(gl_s)

    qs, ks, us, ws, qkds = _heads_prepare(y, beta_s, gc_s, egc_s, CHUNK_SHIFT)

    zero = jnp.zeros((csz, 2 * HEAD_DIM), F32)
    n_chunks = n // csz
    qps, ops, kns, egl_reps = [], [], [], []
    for h in range(N_HEADS):
        la = LANE_A + h
        wu = jnp.concatenate([ws[h], us[h]], axis=1)
        qw = jnp.concatenate([_dot(qkd, wu[bi * GDN_BLOCK:(bi + 1) * GDN_BLOCK])
                              for bi, qkd in enumerate(qkds[h])], axis=0)
        qps.append(qs[h] * egc_s[:, la:la + 1] - qw[:, :HEAD_DIM])
        ops.append(qw[:, HEAD_DIM:])
        kg_t = (ks[h] * ekg_s[:, la:la + 1]).T
        kn = []
        for ci in range(n_chunks):
            rows = slice(ci * csz, (ci + 1) * csz)
            pair = slice((ci // 2) * 2 * csz, (ci // 2 + 1) * 2 * csz)
            half = jnp.concatenate([wu[rows], zero] if ci % 2 == 0 else [zero, wu[rows]], axis=0)
            kn.append(_dot(kg_t[:, pair], half))
        kns.append(kn)
        egl_reps.append(jnp.broadcast_to(egl_s[:, la:la + 1], (n, HEAD_DIM)))

    states = [s_ref[h] for h in range(N_HEADS)]
    outs = [[] for _ in range(N_HEADS)]
    for ci in range(n_chunks):
        rows = slice(ci * csz, (ci + 1) * csz)
        for h in range(N_HEADS):
            s = states[h]
            outs[h].append(_dot(qps[h][rows], s) + ops[h][rows])
            kn = kns[h][ci]
            states[h] = (s * egl_reps[h][ci * csz:ci * csz + 1, :] - _dot(kn[:, :HEAD_DIM], s)) + kn[:, HEAD_DIM:]
    for h in range(N_HEADS):
        hs = slice(h * HEAD_DIM, (h + 1) * HEAD_DIM)
        s_ref[h] = states[h]
        o = jnp.concatenate(outs[h], axis=0)
        mix_ref[0, :, hs] = _gated_rmsnorm(o, proj_ref[0, :, C_Z + h * HEAD_DIM:C_Z + (h + 1) * HEAD_DIM],
                                           normw_ref[...])

    sfin_ref[0] = s_ref[...]

    p = proj_ref[0, :, C_P:C_P + D_B]
    extp = jnp.concatenate([pc_ref[...], p], axis=0)
    pc_ref[...] = p[n - 16:n, :]
    r, c = _iota2(n, n + 16)
    lag = r + 16 - c
    pos = l * n + lax.broadcasted_iota(jnp.int32, (n, 1), 0)
    for gi, w in enumerate(POOL_WINDOWS):
        gs = slice(gi * HEAD_DIM, (gi + 1) * HEAD_DIM)
        band = jnp.where((lag >= 0) & (lag < w), 1.0, 0.0).astype(BF16)
        cnt = jnp.minimum(pos + 1, w).astype(F32)
        mix_ref[0, :, D_A + gi * HEAD_DIM:D_A + (gi + 1) * HEAD_DIM] = _pool_out(
            _dot01(band, extp[:, gs]), cnt, p[:, gs], poolw_ref[gi], pscale_ref[:, gs])


def _mixer_prompt(proj, conv_w, arow, dtrow, normw, poolw, pscale, lb):
    b, seq, _ = proj.shape
    const2 = lambda i, j: (0, 0)
    return pl.pallas_call(
        functools.partial(_mixer_prompt_kernel, lb=lb),
        out_shape=(jax.ShapeDtypeStruct((b, seq, D_MODEL), F32),
                   jax.ShapeDtypeStruct((b, N_HEADS, HEAD_DIM, HEAD_DIM), F32)),
        grid=(b, seq // lb),
        in_specs=[pl.BlockSpec((1, lb, C_TOT), lambda i, j: (i, j, 0)),
                  pl.BlockSpec((CONV_W, C_QKV), const2),
                  pl.BlockSpec((1, 128), const2),
                  pl.BlockSpec((1, 128), const2),
                  pl.BlockSpec((1, HEAD_DIM), const2),
                  pl.BlockSpec((N_GROUPS, HEAD_DIM, HEAD_DIM), lambda i, j: (0, 0, 0)),
                  pl.BlockSpec((1, D_B), const2)],
        out_specs=(pl.BlockSpec((1, lb, D_MODEL), lambda i, j: (i, j, 0)),
                   pl.BlockSpec((1, N_HEADS, HEAD_DIM, HEAD_DIM), lambda i, j: (i, 0, 0, 0))),
        scratch_shapes=[pltpu.VMEM((8, C_QKV), F32), pltpu.VMEM((16, D_B), F32),
                        pltpu.VMEM((N_HEADS, HEAD_DIM, HEAD_DIM), F32)],
        compiler_params=pltpu.CompilerParams(dimension_semantics=("parallel", "arbitrary"),
                                             vmem_limit_bytes=VMEM_LIMIT),
        name="mixer_prompt",
    )(proj, conv_w, arow, dtrow, normw, poolw, pscale)


def _mixer_sample_kernel(proj_ref, cst_ref, pst_ref, sin_ref, convw_ref, arow_ref, dtrow_ref, normw_ref,
                         poolw_ref, pscale_ref, mix_ref, sout_ref, *, ns, seq, start):
    n = ns * seq
    sshift = seq.bit_length() - 1
    rowi = lax.broadcasted_iota(jnp.int32, (n, 1), 0)
    tpos = rowi & (seq - 1)

    u = proj_ref[:, 0:C_QKV]
    st = cst_ref[...]
    cw = convw_ref[...]
    acc = u * cw[CONV_W - 1:CONV_W, :]
    for d in range(1, CONV_W):
        term = jnp.where(tpos >= d, pltpu.roll(u, d, 0), pltpu.roll(st, n - seq + d, 0))
        acc = acc + term * cw[CONV_W - 1 - d:CONV_W - d, :]
    y = _silu(acc)

    beta_s, gc_s, gl_s = _gate_slabs(proj_ref[:, C_BA:C_TOT], arow_ref[...], dtrow_ref[...], sshift)
    egc_s = jnp.exp(gc_s)
    ekg_s = jnp.exp(gl_s - gc_s)
    egl_s = jnp.exp(gl_s)
    qs, ks, us, ws_, qkds = _heads_prepare(y, beta_s, gc_s, egc_s, sshift)

    for h in range(N_HEADS):
        hs = slice(h * HEAD_DIM, (h + 1) * HEAD_DIM)
        la = LANE_A + h
        u_, w_, qkd = us[h], ws_[h], qkds[h][0]
        qg = qs[h] * egc_s[:, la:la + 1]
        kg_t = (ks[h] * ekg_s[:, la:la + 1]).T
        egl_rep = jnp.broadcast_to(egl_s[:, la:la + 1], (n, HEAD_DIM))
        ws_w, ws_q = [], []
        for si in range(ns):
            rows = slice(si * seq, (si + 1) * seq)
            ws = _dot(jnp.concatenate([w_[rows], qg[rows]], axis=0), sin_ref[si, h])
            ws_w.append(ws[:seq])
            ws_q.append(ws[seq:])
        vn = u_ - jnp.concatenate(ws_w, axis=0)
        o = jnp.concatenate(ws_q, axis=0) + _dot(qkd, vn)
        for si in range(ns):
            vmask = jnp.where((rowi >> sshift) == si, vn, 0.0)
            sout_ref[si, h] = sin_ref[si, h] * egl_rep[si * seq:si * seq + 1, :] + _dot(kg_t, vmask)
        mix_ref[:, hs] = _gated_rmsnorm(o, proj_ref[:, C_Z + h * HEAD_DIM:C_Z + (h + 1) * HEAD_DIM], normw_ref[...])

    p = proj_ref[:, C_P:C_P + D_B]
    pst = pst_ref[...]
    r, c = _iota2(n, n)
    band_new_base = ((r >> sshift) == (c >> sshift)) & (r >= c)
    r2, c2 = _iota2(n, ns * 16)
    same2 = (r2 >> sshift) == (c2 >> 4)
    t2 = r2 & (seq - 1)
    j2 = c2 & 15
    pos = start + tpos
    for gi, w in enumerate(POOL_WINDOWS):
        gs = slice(gi * HEAD_DIM, (gi + 1) * HEAD_DIM)
        band_new = jnp.where(band_new_base & ((r - c) < w), 1.0, 0.0).astype(BF16)
        band_st = jnp.where(same2 & (j2 >= 17 + t2 - w), 1.0, 0.0).astype(BF16)
        s = _dot01(band_new, p[:, gs]) + _dot01(band_st, pst[:, gs])
        cnt = jnp.minimum(pos + 1, w).astype(F32)
        mix_ref[:, D_A + gi * HEAD_DIM:D_A + (gi + 1) * HEAD_DIM] = _pool_out(
            s, cnt, p[:, gs], poolw_ref[gi], pscale_ref[:, gs])


def _mixer_sample(proj, cst, pst, sin, conv_w, arow, dtrow, normw, poolw, pscale, ns, seq, start):
    t = proj.shape[0]
    nb = t // seq
    n = ns * seq
    assert n == GDN_BLOCK and seq & (seq - 1) == 0 and seq >= CONV_W - 1
    const1 = lambda i: (0, 0)
    return pl.pallas_call(
        functools.partial(_mixer_sample_kernel, ns=ns, seq=seq, start=start),
        out_shape=(jax.ShapeDtypeStruct((t, D_MODEL), F32),
                   jax.ShapeDtypeStruct((nb, N_HEADS, HEAD_DIM, HEAD_DIM), F32)),
        grid=(nb // ns,),
        in_specs=[pl.BlockSpec((n, C_TOT), lambda i: (i, 0)),
                  pl.BlockSpec((n, C_QKV), lambda i: (i, 0)),
                  pl.BlockSpec((ns * 16, D_B), lambda i: (i, 0)),
                  pl.BlockSpec((ns, N_HEADS, HEAD_DIM, HEAD_DIM), lambda i: (i, 0, 0, 0)),
                  pl.BlockSpec((CONV_W, C_QKV), const1),
                  pl.BlockSpec((1, 128), const1),
                  pl.BlockSpec((1, 128), const1),
                  pl.BlockSpec((1, HEAD_DIM), const1),
                  pl.BlockSpec((N_GROUPS, HEAD_DIM, HEAD_DIM), lambda i: (0, 0, 0)),
                  pl.BlockSpec((1, D_B), const1)],
        out_specs=(pl.BlockSpec((n, D_MODEL), lambda i: (i, 0)),
                   pl.BlockSpec((ns, N_HEADS, HEAD_DIM, HEAD_DIM), lambda i: (i, 0, 0, 0))),
        compiler_params=pltpu.CompilerParams(dimension_semantics=("parallel",), vmem_limit_bytes=VMEM_LIMIT),
        name="mixer_sample",
    )(proj, cst, pst, sin, conv_w, arow, dtrow, normw, poolw, pscale)


def _layer_norm(x, g, b):
    mu = jnp.mean(x, axis=-1, keepdims=True)
    xc = x - mu
    var = jnp.mean(xc * xc, axis=-1, keepdims=True)
    return xc * lax.rsqrt(var + LN_EPS) * g + b


def _outproj_router_kernel(mixp_ref, xp_ref, mixs_ref, xs_ref, wout_ref, g1_ref, b1_ref, wrh_ref, wrl_ref, br_ref,
                           h_ref, sel_ref, gw_ref, cnt_ref, carry_ref, *, n_prompt_tiles):
    @pl.when(pl.program_id(0) == 0)
    def _init():
        carry_ref[...] = jnp.zeros_like(carry_ref)

    is_prompt = pl.program_id(0) < n_prompt_tiles
    mix = jnp.where(is_prompt, mixp_ref[...], mixs_ref[...])
    x = jnp.where(is_prompt, xp_ref[...], xs_ref[...])
    h = _layer_norm(ALPHA * x + _dot(mix, wout_ref[...]), g1_ref[...], b1_ref[...])
    h_ref[...] = h
    hh, hm, _ = _split3(h)
    f = lambda a, b: jnp.dot(a, b, preferred_element_type=F32)
    logits = f(hh, wrh_ref[...]) + (f(hm, wrh_ref[...]) + f(hh, wrl_ref[...])) + br_ref[...]
    tm = logits.shape[0]
    lane = lax.broadcasted_iota(jnp.int32, (tm, 128), 1)
    big = jnp.int32(1 << 20)
    neg = -jnp.inf
    gmask = (lane >= N_EXPERTS) & (lane < N_EXPERTS + N_GROUPS)
    lg = jnp.where(gmask, logits, neg)
    gmax = jnp.max(lg, axis=1, keepdims=True)
    gidx = jnp.min(jnp.where(lg == gmax, lane - N_EXPERTS, big), axis=1, keepdims=True)
    pg = 1.0 / jnp.sum(jnp.where(gmask, jnp.exp(logits - gmax), 0.0), axis=1, keepdims=True)
    emask = (lane < N_EXPERTS) & ((lane >> 3) == gidx)
    le = jnp.where(emask, logits, neg)
    v1 = jnp.max(le, axis=1, keepdims=True)
    i1 = jnp.min(jnp.where((le == v1) & emask, lane, big), axis=1, keepdims=True)
    emask2 = emask & (lane != i1)
    le2 = jnp.where(emask2, logits, neg)
    v2 = jnp.max(le2, axis=1, keepdims=True)
    i2 = jnp.min(jnp.where((le2 == v2) & emask2, lane, big), axis=1, keepdims=True)
    e2 = jnp.exp(v2 - v1)
    den = 1.0 + e2
    gw_ref[...] = jnp.where(lane == 0, (1.0 / den) * pg, jnp.where(lane == 1, (e2 / den) * pg, 0.0))
    onehot = jnp.where((lane == i1) | (lane == i2), 1.0, 0.0)
    r, c = _iota2(tm, tm)
    before = jnp.dot(jnp.where(r > c, 1.0, 0.0).astype(BF16), onehot.astype(BF16), preferred_element_type=F32)
    before = before + carry_ref[...]
    r1 = jnp.sum(jnp.where(lane == i1, before, 0.0), axis=1, keepdims=True).astype(jnp.int32)
    r2 = jnp.sum(jnp.where(lane == i2, before, 0.0), axis=1, keepdims=True).astype(jnp.int32)
    carry_ref[...] += jnp.sum(onehot, axis=0, keepdims=True)
    cnt_ref[...] = carry_ref[...].astype(jnp.int32)
    sel_ref[...] = jnp.where(lane == 0, i1, jnp.where(lane == 1, i2, jnp.where(lane == 2, r1,
                                                                               jnp.where(lane == 3, r2, 0))))


def _outproj_router(mix_p, x_p, mix_s, x_s, wout, g1, b1, wrh, wrl, br, tm):
    tp, ts = x_p.shape[0], x_s.shape[0]
    t = tp + ts
    npt = tp // tm
    row = lambda i: (i, 0)
    prow = lambda i: (jnp.minimum(i, npt - 1), 0)
    srow = lambda i: (jnp.maximum(i - npt, 0), 0)
    const = lambda i: (0, 0)
    return pl.pallas_call(
        functools.partial(_outproj_router_kernel, n_prompt_tiles=npt),
        out_shape=(jax.ShapeDtypeStruct((t, D_MODEL), F32), jax.ShapeDtypeStruct((t, 128), jnp.int32),
                   jax.ShapeDtypeStruct((t, 128), F32), jax.ShapeDtypeStruct((1, 128), jnp.int32)),
        grid=(t // tm,),
        in_specs=[pl.BlockSpec((tm, D_MODEL), prow), pl.BlockSpec((tm, D_MODEL), prow),
                  pl.BlockSpec((tm, D_MODEL), srow), pl.BlockSpec((tm, D_MODEL), srow),
                  pl.BlockSpec((D_MODEL, D_MODEL), const), pl.BlockSpec((1, D_MODEL), const),
                  pl.BlockSpec((1, D_MODEL), const), pl.BlockSpec((D_MODEL, 128), const),
                  pl.BlockSpec((D_MODEL, 128), const), pl.BlockSpec((1, 128), const)],
        out_specs=(pl.BlockSpec((tm, D_MODEL), row), pl.BlockSpec((tm, 128), row), pl.BlockSpec((tm, 128), row),
                   pl.BlockSpec((1, 128), const)),
        scratch_shapes=[pltpu.VMEM((1, 128), F32)],
        compiler_params=pltpu.CompilerParams(dimension_semantics=("arbitrary",), vmem_limit_bytes=VMEM_LIMIT),
        name="outproj_router",
    )(mix_p, x_p, mix_s, x_s, wout, g1, b1, wrh, wrl, br)


ROW_SLAB = D_MODEL // 128
EXPERT_TILE = 512
ROUTE_TILE = 256


def _to_slabs(ref, x, n):
    for c in range(ROW_SLAB):
        ref[pl.ds(c, n, stride=ROW_SLAB), :] = x[:, c * 128:(c + 1) * 128]


def _from_slabs(ref, n):
    return jnp.concatenate([ref[pl.ds(c, n, stride=ROW_SLAB), :] for c in range(ROW_SLAB)], axis=1)


def _slab(ref, row):
    if isinstance(row, int):
        return ref.at[pl.ds(row * ROW_SLAB, ROW_SLAB)]
    return ref.at[pl.ds(pl.multiple_of(row * ROW_SLAB, ROW_SLAB), ROW_SLAB)]


def _dispatch_kernel(pad_start_ref, pad_cnt_ref, tail_ref, h_ref, pos_ref, xs_ref, stage_ref, sem, zsem, tsem,
                     *, td, tmx):
    @pl.when(pl.program_id(0) == 0)
    def _zero_unused_rows():
        stage_ref[...] = jnp.zeros_like(stage_ref)
        def zcopy(row, n_rows):
            return pltpu.make_async_copy(
                stage_ref.at[pl.ds(0, n_rows * ROW_SLAB)],
                xs_ref.at[pl.ds(pl.multiple_of(row * ROW_SLAB, ROW_SLAB), n_rows * ROW_SLAB)], zsem)

        tile_rows = min(td, tmx) * ROW_SLAB
        tcopy = lambda tile: pltpu.make_async_copy(
            stage_ref.at[pl.ds(0, tile_rows)],
            xs_ref.at[pl.ds(pl.multiple_of(tile * tile_rows, tile_rows), tile_rows)], tsem)

        def pad_pieces(e, start_not_wait):
            start = pad_start_ref[e]
            n = pad_cnt_ref[e]
            piece = tmx // 2
            while piece >= 1:
                @pl.when((n & piece) != 0)
                def _(piece=piece):
                    cp = zcopy(start + (n & ~(2 * piece - 1)), piece)
                    cp.start() if start_not_wait else cp.wait()
                piece //= 2

        def start_pads(e, carry):
            pad_pieces(e, True)
            return carry

        def wait_pads(e, carry):
            pad_pieces(e, False)
            return carry

        lax.fori_loop(0, N_EXPERTS, start_pads, 0)

        def tail_start(r, carry):
            tcopy(tail_ref[0] + r).start()
            return carry

        lax.fori_loop(0, tail_ref[1], tail_start, 0)
        lax.fori_loop(0, N_EXPERTS, wait_pads, 0)

        def tail_wait(r, carry):
            tcopy(0).wait()
            return carry

        lax.fori_loop(0, tail_ref[1], tail_wait, 0)

    _to_slabs(stage_ref, h_ref[...], td)
    copies = []
    for t in range(td):
        for k in range(2):
            cp = pltpu.make_async_copy(_slab(stage_ref, t), _slab(xs_ref, pos_ref[0, 0, 2 * t + k]), sem)
            cp.start(priority=k)
            copies.append(cp)
    for cp in copies:
        cp.wait()


def _dispatch(h, pos3, pad_start, pad_cnt, tail, n_rows, td, tmx):
    t = h.shape[0]
    assert td % min(td, tmx) == 0 and tmx % min(td, tmx) == 0 and tmx // 2 <= td
    return pl.pallas_call(
        functools.partial(_dispatch_kernel, td=td, tmx=tmx),
        out_shape=jax.ShapeDtypeStruct((n_rows * ROW_SLAB, 128), F32),
        grid_spec=pltpu.PrefetchScalarGridSpec(
            num_scalar_prefetch=3,
            grid=(t // td,),
            in_specs=[pl.BlockSpec((td, D_MODEL), lambda i, *_: (i, 0)),
                      pl.BlockSpec((1, 1, 2 * td), lambda i, *_: (i, 0, 0), memory_space=pltpu.SMEM)],
            out_specs=pl.BlockSpec(memory_space=pl.ANY),
            scratch_shapes=[pltpu.VMEM((td * ROW_SLAB, 128), F32),
                            pltpu.SemaphoreType.DMA, pltpu.SemaphoreType.DMA, pltpu.SemaphoreType.DMA]),
        compiler_params=pltpu.CompilerParams(dimension_semantics=("arbitrary",), vmem_limit_bytes=VMEM_LIMIT),
        name="moe_dispatch",
    )(pad_start, pad_cnt, tail, h, pos3)


def _experts_kernel(tile_idx_ref, tile_e_ref, tile_ok_ref, x_ref, wg_ref, wu_ref, wd_ref, o_ref, *, tmx):
    ok = tile_ok_ref[pl.program_id(0)] != 0

    @pl.when(ok)
    def _compute():
        x = _from_slabs(x_ref, tmx).astype(BF16)
        a = jnp.dot(x, wg_ref[0].astype(BF16), preferred_element_type=F32)
        b = jnp.dot(x, wu_ref[0].astype(BF16), preferred_element_type=F32)
        act = (_silu(a) * b).astype(BF16)
        _to_slabs(o_ref, jnp.dot(act, wd_ref[0].astype(BF16), preferred_element_type=F32), tmx)

    @pl.when(jnp.logical_not(ok))
    def _unused_tile():
        o_ref[...] = jnp.zeros_like(o_ref)


def _experts(xs, tile_idx, tile_e, tile_ok, wg, wu, wd, tmx):
    nt = tile_idx.shape[0]
    rows = lambda j, ti, te, ok: (ti[j], 0)
    wsel = lambda j, ti, te, ok: (te[j], 0, 0)
    own = lambda j, ti, te, ok: (j, 0)
    return pl.pallas_call(
        functools.partial(_experts_kernel, tmx=tmx),
        out_shape=jax.ShapeDtypeStruct(xs.shape, F32),
        grid_spec=pltpu.PrefetchScalarGridSpec(
            num_scalar_prefetch=3,
            grid=(nt,),
            in_specs=[pl.BlockSpec((tmx * ROW_SLAB, 128), rows),
                      pl.BlockSpec((1, D_MODEL, D_EXPERT), wsel),
                      pl.BlockSpec((1, D_MODEL, D_EXPERT), wsel),
                      pl.BlockSpec((1, D_EXPERT, D_MODEL), wsel)],
            out_specs=pl.BlockSpec((tmx * ROW_SLAB, 128), own)),
        compiler_params=pltpu.CompilerParams(dimension_semantics=("arbitrary",), vmem_limit_bytes=VMEM_LIMIT),
        name="moe_experts",
    )(tile_idx, tile_e, tile_ok, xs, wg, wu, wd)


def _combine_kernel(h_ref, gw_ref, pos_ref, g2_ref, b2_ref, os_ref, y_ref, stage_ref, sem, *, td):
    copies = []
    for t in range(td):
        for k in range(2):
            cp = pltpu.make_async_copy(_slab(os_ref, pos_ref[0, 0, 2 * t + k]), _slab(stage_ref.at[k], t), sem)
            cp.start(priority=k)
            copies.append(cp)
    for cp in copies:
        cp.wait()
    gw = gw_ref[...]
    moe = gw[:, 0:1] * _from_slabs(stage_ref.at[0], td) + gw[:, 1:2] * _from_slabs(stage_ref.at[1], td)
    y_ref[...] = _layer_norm(ALPHA * h_ref[...] + moe, g2_ref[...], b2_ref[...])


def _combine(h, gw, pos3, g2, b2, os, td, first_token, n_tokens):
    off = first_token // td
    row = lambda i: (i + off, 0)
    const = lambda i: (0, 0)
    return pl.pallas_call(
        functools.partial(_combine_kernel, td=td),
        out_shape=jax.ShapeDtypeStruct((n_tokens, D_MODEL), F32),
        grid=(n_tokens // td,),
        in_specs=[pl.BlockSpec((td, D_MODEL), row), pl.BlockSpec((td, 128), row),
                  pl.BlockSpec((1, 1, 2 * td), lambda i: (i + off, 0, 0), memory_space=pltpu.SMEM),
                  pl.BlockSpec((1, D_MODEL), const), pl.BlockSpec((1, D_MODEL), const),
                  pl.BlockSpec(memory_space=pl.ANY)],
        out_specs=pl.BlockSpec((td, D_MODEL), lambda i: (i, 0)),
        scratch_shapes=[pltpu.VMEM((2, td * ROW_SLAB, 128), F32), pltpu.SemaphoreType.DMA],
        compiler_params=pltpu.CompilerParams(dimension_semantics=("arbitrary",), vmem_limit_bytes=VMEM_LIMIT),
        name="moe_combine",
    )(h, gw, pos3, g2, b2, os)


def _route_plan(sel, cnt, t, tmx, td):
    i32 = jnp.int32
    counts = cnt[0, :N_EXPERTS]
    padded = ((counts + tmx - 1) // tmx) * tmx
    ex = jnp.arange(N_EXPERTS, dtype=i32)
    ends = jnp.sum(jnp.where(ex[None, :] <= ex[:, None], padded[None, :], 0), axis=1).astype(i32)
    offs = ends - padded
    pos = (jnp.sum(jnp.where(sel[:, 0:2, None] == ex, offs, 0), axis=-1) + sel[:, 2:4]).astype(i32)
    nt = 2 * t // tmx + N_EXPERTS
    n_used = ends[-1] // tmx
    tile = jnp.arange(nt, dtype=i32)
    tile_idx = jnp.minimum(tile, jnp.maximum(n_used - 1, 0))
    tile_e = jnp.minimum(jnp.sum((ends[None, :] <= (tile_idx * tmx)[:, None]).astype(i32), axis=1), N_EXPERTS - 1)
    tile_ok = (tile < n_used).astype(i32)
    pieces = tmx // min(td, tmx)
    tail = jnp.stack([n_used * pieces, (nt - n_used) * pieces]).astype(i32)
    return (pos.reshape(t // td, 1, 2 * td), tile_idx, tile_e, tile_ok, (offs + counts).astype(i32),
            (padded - counts).astype(i32), tail, nt * tmx)


def _tile(t, want):
    tm = min(want, t)
    while t % tm:
        tm //= 2
    return tm


def _prep_weights(w_in, conv_w, a_log, dt_bias, gdn_norm_w, pool_w, pool_scale, w_out, ln1_g, ln1_b,
                  w_rg, b_rg, w_re, b_re, w_gate, w_up, w_down, ln2_g, ln2_b):
    col_b = 4 * D_A
    col_p = 4 * D_A + 2 * N_HEADS
    w_cat = jnp.concatenate([w_in[:, :col_b], w_in[:, col_p:], w_in[:, col_b:col_p],
                             jnp.zeros((D_MODEL, 128 - 2 * N_HEADS), w_in.dtype)], axis=1).astype(BF16)
    lane_pad = lambda v, off: jnp.zeros((1, 128), F32).at[0, off:off + v.shape[0]].set(v.astype(F32))
    w_r = jnp.concatenate([w_re, w_rg, jnp.zeros((D_MODEL, 128 - N_EXPERTS - N_GROUPS), F32)], axis=1)
    wrh = w_r.astype(BF16)
    wrl = (w_r - wrh.astype(F32)).astype(BF16)
    b_r = jnp.zeros((1, 128), F32).at[0, :N_EXPERTS].set(b_re).at[0, N_EXPERTS:N_EXPERTS + N_GROUPS].set(b_rg)
    return dict(
        w_cat=w_cat, conv_w=conv_w, arow=lane_pad(a_log, LANE_A), dtrow=lane_pad(dt_bias, LANE_A),
        normw=gdn_norm_w.reshape(1, HEAD_DIM), poolw=pool_w.astype(BF16), pscale=pool_scale.reshape(1, D_B),
        wout=w_out.astype(BF16), g1=ln1_g.reshape(1, D_MODEL), b1=ln1_b.reshape(1, D_MODEL),
        wrh=wrh, wrl=wrl, br=b_r,
        wg=w_gate.reshape(N_EXPERTS, D_MODEL, D_EXPERT), wu=w_up.reshape(N_EXPERTS, D_MODEL, D_EXPERT),
        wd=w_down.reshape(N_EXPERTS, D_EXPERT, D_MODEL),
        g2=ln2_g.reshape(1, D_MODEL), b2=ln2_b.reshape(1, D_MODEL))


def _post_mixer(mix_p, x_p, mix_s, x_s, p):
    tp, ts = x_p.shape[0], x_s.shape[0]
    t = tp + ts
    tm = math.gcd(_tile(tp, 512), _tile(ts, 512))
    h, sel, gw, cnt = _outproj_router(mix_p, x_p, mix_s, x_s, p["wout"], p["g1"], p["b1"], p["wrh"], p["wrl"],
                                      p["br"], tm)
    td = math.gcd(_tile(tp, ROUTE_TILE), _tile(ts, ROUTE_TILE))
    pos3, tile_idx, tile_e, tile_ok, pad_start, pad_cnt, tail, n_rows = _route_plan(sel, cnt, t, EXPERT_TILE, td)
    xs = _dispatch(h, pos3, pad_start, pad_cnt, tail, n_rows, td, EXPERT_TILE)
    os = _experts(xs, tile_idx, tile_e, tile_ok, p["wg"], p["wu"], p["wd"], EXPERT_TILE)
    y_p = _combine(h, gw, pos3, p["g2"], p["b2"], os, td, 0, tp)
    y_s = _combine(h, gw, pos3, p["g2"], p["b2"], os, td, tp, ts)
    return y_p, y_s


def _mix_prompt(x, p, lb=256):
    b, seq, _ = x.shape
    x2d = x.reshape(b * seq, D_MODEL)
    proj = _in_proj(x2d, p["w_cat"], _tile(b * seq, 1024)).reshape(b, seq, C_TOT)
    mix, s_fin = _mixer_prompt(proj, p["conv_w"], p["arow"], p["dtrow"], p["normw"], p["poolw"], p["pscale"],
                               min(lb, seq))
    conv_new = proj[:, seq - (CONV_W - 1):, 0:C_QKV]
    pool_new = proj[:, seq - POOL_BUF:, C_P:C_P + D_B]
    return x2d, mix.reshape(b * seq, D_MODEL), s_fin, conv_new, pool_new


def _mix_sample(x, s0, conv0, pool0, start, p, ns=16):
    b, seq, _ = x.shape
    x2d = x.reshape(b * seq, D_MODEL)
    proj = _in_proj(x2d, p["w_cat"], _tile(b * seq, 1024))
    cst = jnp.pad(conv0, ((0, 0), (seq - (CONV_W - 1), 0), (0, 0))).reshape(b * seq, C_QKV)
    pst = jnp.pad(pool0, ((0, 0), (1, 0), (0, 0))).reshape(b * 16, D_B)
    mix, s_new = _mixer_sample(proj, cst, pst, s0, p["conv_w"], p["arow"], p["dtrow"], p["normw"], p["poolw"],
                               p["pscale"], min(ns, b), seq, start)
    proj3 = proj.reshape(b, seq, C_TOT)
    conv_new = proj3[:, seq - (CONV_W - 1):, 0:C_QKV]
    pool_new = jnp.concatenate([pool0[:, seq:, :], proj3[:, :, C_P:C_P + D_B]], axis=1)
    return x2d, mix, s_new, conv_new, pool_new


def _layer(x_prompt, x_sample, s0, conv0, pool0, start, p):
    xp2d, mix_p, dp, cp, pp = _mix_prompt(x_prompt, p)
    xs2d, mix_s, ds, cs, ps = _mix_sample(x_sample, s0, conv0, pool0, start, p)
    y_p, y_s = _post_mixer(mix_p, xp2d, mix_s, xs2d, p)
    return y_p.reshape(x_prompt.shape), y_s.reshape(x_sample.shape), (dp, cp, pp), (ds, cs, ps)


def kernel(x_prompt, x_sample, state_delta, state_conv, state_pool, w_in, conv_w, a_log, dt_bias, gdn_norm_w,
           pool_w, pool_scale, w_out, ln1_g, ln1_b, w_rg, b_rg, w_re, b_re, w_gate, w_up, w_down, ln2_g, ln2_b):
    depth = w_in.shape[0]
    past_len = 16384
    yp, ys = x_prompt, x_sample
    outs = [[] for _ in range(6)]
    for l in range(depth):
        p = _prep_weights(w_in[l], conv_w[l], a_log[l], dt_bias[l], gdn_norm_w[l], pool_w[l], pool_scale[l],
                          w_out[l], ln1_g[l], ln1_b[l], w_rg[l], b_rg[l], w_re[l], b_re[l], w_gate[l], w_up[l],
                          w_down[l], ln2_g[l], ln2_b[l])
        yp, ys, st_p, st_s = _layer(yp, ys, state_delta[l], state_conv[l], state_pool[l], past_len, p)
        for lst, v in zip(outs, st_p + st_s):
            lst.append(v)
    return (yp, ys) + tuple(jnp.stack(v) for v in outs)
```

```python
import functools
import math

import jax
import jax.numpy as jnp
from jax import lax
from jax.experimental import pallas as pl
from jax.experimental.pallas import tpu as pltpu

F32 = jnp.float32
BF16 = jnp.bfloat16

D_MODEL = 1024
D_A = 512
D_B = 512
HEAD_DIM = 128
N_HEADS = 4
CONV_W = 4
CHUNK_SHIFT = 6
GDN_BLOCK = 128
POOL_WINDOWS = (2, 4, 8, 16)
POOL_BUF = 15
N_GROUPS = 4
E_PER_GROUP = 8
N_EXPERTS = N_GROUPS * E_PER_GROUP
D_EXPERT = 256
ALPHA = 2.0 ** 0.25
LN_EPS = 1e-5
RMS_EPS = 1e-6
L2_EPS = 1e-6

C_QKV = 3 * D_A
C_Z = 3 * D_A
C_P = 4 * D_A
C_BA = 4 * D_A + D_B
C_TOT = C_BA + 128
LANE_B = 0
LANE_A = N_HEADS

VMEM_LIMIT = 56 * 1024 * 1024


def _dot(a, b):
    return jnp.dot(a.astype(BF16), b.astype(BF16), preferred_element_type=F32)


def _dot_nt(a, b):
    return lax.dot_general(a.astype(BF16), b.astype(BF16), (((1,), (1,)), ((), ())), preferred_element_type=F32)


def _split3(x):
    hi = x.astype(BF16)
    r = x - hi.astype(F32)
    mid = r.astype(BF16)
    lo = (r - mid.astype(F32)).astype(BF16)
    return hi, mid, lo


def _dot01(m01, x):
    hi, mid, lo = _split3(x)
    f = lambda p: jnp.dot(m01, p, preferred_element_type=F32)
    return f(hi) + f(mid) + f(lo)


def _silu(x):
    return x * jax.nn.sigmoid(x)


def _softplus(x):
    return jnp.maximum(x, 0.0) + jnp.log1p(jnp.exp(-jnp.abs(x)))


def _iota2(n, m):
    return lax.broadcasted_iota(jnp.int32, (n, m), 0), lax.broadcasted_iota(jnp.int32, (n, m), 1)


def _proj_kernel(x_ref, w_ref, o_ref):
    o_ref[...] = jnp.dot(x_ref[...].astype(BF16), w_ref[...], preferred_element_type=F32)


def _in_proj(x2d, w_cat, tm):
    t = x2d.shape[0]
    return pl.pallas_call(
        _proj_kernel,
        out_shape=jax.ShapeDtypeStruct((t, C_TOT), F32),
        grid=(t // tm,),
        in_specs=[pl.BlockSpec((tm, D_MODEL), lambda i: (i, 0)),
                  pl.BlockSpec((D_MODEL, C_TOT), lambda i: (0, 0))],
        out_specs=pl.BlockSpec((tm, C_TOT), lambda i: (i, 0)),
        compiler_params=pltpu.CompilerParams(dimension_semantics=("parallel",), vmem_limit_bytes=VMEM_LIMIT),
        name="in_proj",
    )(x2d, w_cat)


def _unit_lower_inverse(a_list, r, c, chunk_shift):
    b0 = min(4, chunk_shift)
    eye = jnp.where(r == c, 1.0, 0.0).astype(F32)
    blk = (r >> b0) == (c >> b0)
    xs = [jnp.where(blk, a, 0.0) for a in a_list]
    ts = [eye - x for x in xs]
    for _ in range(b0 - 1):
        xs = [_dot(x, x) for x in xs]
        ts = [t + _dot(t, x) for t, x in zip(ts, xs)]
    for lvl in range(b0, chunk_shift):
        m = ((r >> (lvl + 1)) == (c >> (lvl + 1))) & ((r >> lvl) != (c >> lvl))
        tmp = [_dot(t, jnp.where(m, a, 0.0)) for t, a in zip(ts, a_list)]
        ts = [t - _dot(x, t) for t, x in zip(ts, tmp)]
    return ts


def _gate_slabs(ba, arow, dtrow, chunk_shift):
    n = ba.shape[0]
    beta = jax.nn.sigmoid(ba)
    g = -jnp.exp(arow) * _softplus(ba + dtrow)
    r, c = _iota2(n, n)
    same = (r >> chunk_shift) == (c >> chunk_shift)
    ltri = jnp.where(same & (r >= c), 1.0, 0.0).astype(BF16)
    lall = jnp.where(same, 1.0, 0.0).astype(BF16)
    cs = _dot01(jnp.concatenate([ltri, lall], axis=0), g)
    return beta, cs[:n], cs[n:]


def _heads_prepare(y, beta_s, gc_s, egc_s, chunk_shift):
    n = y.shape[0]
    nb = n // GDN_BLOCK
    r, c = _iota2(GDN_BLOCK, GDN_BLOCK)
    same = (r >> chunk_shift) == (c >> chunk_shift)
    incl = same & (r >= c)
    strict = same & (r > c)
    gc_t = gc_s.T
    qs, ks, a_list, rhs, decays = [], [], [], [], []
    for h in range(N_HEADS):
        q = _l2norm(y[:, h * HEAD_DIM:(h + 1) * HEAD_DIM]) * (HEAD_DIM ** -0.5)
        k = _l2norm(y[:, D_A + h * HEAD_DIM:D_A + (h + 1) * HEAD_DIM])
        v = y[:, 2 * D_A + h * HEAD_DIM:2 * D_A + (h + 1) * HEAD_DIM]
        la = LANE_A + h
        beta_c = beta_s[:, LANE_B + h:LANE_B + h + 1]
        kb = k * beta_c
        rhs_h = jnp.concatenate([v * beta_c, kb * egc_s[:, la:la + 1]], axis=1)
        for bi in range(nb):
            blk = slice(bi * GDN_BLOCK, (bi + 1) * GDN_BLOCK)
            decay = jnp.exp(jnp.where(incl, gc_s[blk, la:la + 1] - gc_t[la:la + 1, blk], -jnp.inf))
            a_list.append(jnp.where(strict, _dot_nt(kb[blk], k[blk]) * decay, 0.0))
            rhs.append(rhs_h[blk])
            decays.append(decay)
        qs.append(q)
        ks.append(k)
    ts = _unit_lower_inverse(a_list, r, c, chunk_shift)
    sols = [_dot(t, x) for t, x in zip(ts, rhs)]
    us, ws, qkds = [], [], []
    for h in range(N_HEADS):
        sol = jnp.concatenate(sols[h * nb:(h + 1) * nb], axis=0) if nb > 1 else sols[h]
        us.append(sol[:, :HEAD_DIM])
        ws.append(sol[:, HEAD_DIM:])
        qkds.append([_dot_nt(qs[h][bi * GDN_BLOCK:(bi + 1) * GDN_BLOCK], ks[h][bi * GDN_BLOCK:(bi + 1) * GDN_BLOCK])
                     * decays[h * nb + bi] for bi in range(nb)])
    return qs, ks, us, ws, qkds


def _l2norm(x):
    return x * lax.rsqrt(jnp.sum(x * x, axis=-1, keepdims=True) + L2_EPS)


def _gated_rmsnorm(o, z, normw):
    o = o * lax.rsqrt(jnp.mean(o * o, axis=-1, keepdims=True) + RMS_EPS) * normw
    return o * _silu(z)


def _pool_out(s, cnt, p_g, poolw_g, pscale_g):
    d = s / cnt - p_g
    return _dot(d, poolw_g) * pscale_g


def _mixer_prompt_kernel(proj_ref, convw_ref, arow_ref, dtrow_ref, normw_ref, poolw_ref, pscale_ref,
                         mix_ref, sfin_ref, cc_ref, pc_ref, s_ref, *, lb):
    l = pl.program_id(1)
    n = lb
    csz = 1 << CHUNK_SHIFT

    @pl.when(l == 0)
    def _init():
        cc_ref[...] = jnp.zeros_like(cc_ref)
        pc_ref[...] = jnp.zeros_like(pc_ref)
        s_ref[...] = jnp.zeros_like(s_ref)

    u = proj_ref[0, :, 0:C_QKV]
    ext = jnp.concatenate([cc_ref[...], u], axis=0)
    cw = convw_ref[...]
    acc = ext * cw[CONV_W - 1:CONV_W, :]
    for d in range(1, CONV_W):
        acc = acc + pltpu.roll(ext, d, 0) * cw[CONV_W - 1 - d:CONV_W - d, :]
    cc_ref[...] = u[n - 8:n, :]
    y = _silu(acc[8:, :])

    beta_s, gc_s, gl_s = _gate_slabs(proj_ref[0, :, C_BA:C_TOT], arow_ref[...], dtrow_ref[...], CHUNK_SHIFT)
    egc_s = jnp.exp(gc_s)
    ekg_s = jnp.exp(gl_s - gc_s)
    egl_s = jnp.exp(gl_s)

    qs, ks, us, ws, qkds = _heads_prepare(y, beta_s, gc_s, egc_s, CHUNK_SHIFT)

    zero = jnp.zeros((csz, 2 * HEAD_DIM), F32)
    n_chunks = n // csz
    qps, ops, kns, egl_reps = [], [], [], []
    for h in range(N_HEADS):
        la = LANE_A + h
        wu = jnp.concatenate([ws[h], us[h]], axis=1)
        qw = jnp.concatenate([_dot(qkd, wu[bi * GDN_BLOCK:(bi + 1) * GDN_BLOCK])
                              for bi, qkd in enumerate(qkds[h])], axis=0)
        qps.append(qs[h] * egc_s[:, la:la + 1] - qw[:, :HEAD_DIM])
        ops.append(qw[:, HEAD_DIM:])
        kg_t = (ks[h] * ekg_s[:, la:la + 1]).T
        kn = []
        for ci in range(n_chunks):
            rows = slice(ci * csz, (ci + 1) * csz)
            pair = slice((ci // 2) * 2 * csz, (ci // 2 + 1) * 2 * csz)
            half = jnp.concatenate([wu[rows], zero] if ci % 2 == 0 else [zero, wu[rows]], axis=0)
            kn.append(_dot(kg_t[:, pair], half))
        kns.append(kn)
        egl_reps.append(jnp.broadcast_to(egl_s[:, la:la + 1], (n, HEAD_DIM)))

    states = [s_ref[h] for h in range(N_HEADS)]
    outs = [[] for _ in range(N_HEADS)]
    for ci in range(n_chunks):
        rows = slice(ci * csz, (ci + 1) * csz)
        for h in range(N_HEADS):
            s = states[h]
            outs[h].append(_dot(qps[h][rows], s) + ops[h][rows])
            kn = kns[h][ci]
            states[h] = (s * egl_reps[h][ci * csz:ci * csz + 1, :] - _dot(kn[:, :HEAD_DIM], s)) + kn[:, HEAD_DIM:]
    for h in range(N_HEADS):
        hs = slice(h * HEAD_DIM, (h + 1) * HEAD_DIM)
        s_ref[h] = states[h]
        o = jnp.concatenate(outs[h], axis=0)
        mix_ref[0, :, hs] = _gated_rmsnorm(o, proj_ref[0, :, C_Z + h * HEAD_DIM:C_Z + (h + 1) * HEAD_DIM],
                                           normw_ref[...])

    sfin_ref[0] = s_ref[...]

    p = proj_ref[0, :, C_P:C_P + D_B]
    extp = jnp.concatenate([pc_ref[...], p], axis=0)
    pc_ref[...] = p[n - 16:n, :]
    r, c = _iota2(n, n + 16)
    lag = r + 16 - c
    pos = l * n + lax.broadcasted_iota(jnp.int32, (n, 1), 0)
    for gi, w in enumerate(POOL_WINDOWS):
        gs = slice(gi * HEAD_DIM, (gi + 1) * HEAD_DIM)
        band = jnp.where((lag >= 0) & (lag < w), 1.0, 0.0).astype(BF16)
        cnt = jnp.minimum(pos + 1, w).astype(F32)
        mix_ref[0, :, D_A + gi * HEAD_DIM:D_A + (gi + 1) * HEAD_DIM] = _pool_out(
            _dot01(band, extp[:, gs]), cnt, p[:, gs], poolw_ref[gi], pscale_ref[:, gs])


def _mixer_prompt(proj, conv_w, arow, dtrow, normw, poolw, pscale, lb):
    b, seq, _ = proj.shape
    const2 = lambda i, j: (0, 0)
    return pl.pallas_call(
        functools.partial(_mixer_prompt_kernel, lb=lb),
        out_shape=(jax.ShapeDtypeStruct((b, seq, D_MODEL), F32),
                   jax.ShapeDtypeStruct((b, N_HEADS, HEAD_DIM, HEAD_DIM), F32)),
        grid=(b, seq // lb),
        in_specs=[pl.BlockSpec((1, lb, C_TOT), lambda i, j: (i, j, 0)),
                  pl.BlockSpec((CONV_W, C_QKV), const2),
                  pl.BlockSpec((1, 128), const2),
                  pl.BlockSpec((1, 128), const2),
                  pl.BlockSpec((1, HEAD_DIM), const2),
                  pl.BlockSpec((N_GROUPS, HEAD_DIM, HEAD_DIM), lambda i, j: (0, 0, 0)),
                  pl.BlockSpec((1, D_B), const2)],
        out_specs=(pl.BlockSpec((1, lb, D_MODEL), lambda i, j: (i, j, 0)),
                   pl.BlockSpec((1, N_HEADS, HEAD_DIM, HEAD_DIM), lambda i, j: (i, 0, 0, 0))),
        scratch_shapes=[pltpu.VMEM((8, C_QKV), F32), pltpu.VMEM((16, D_B), F32),
                        pltpu.VMEM((N_HEADS, HEAD_DIM, HEAD_DIM), F32)],
        compiler_params=pltpu.CompilerParams(dimension_semantics=("parallel", "arbitrary"),
                                             vmem_limit_bytes=VMEM_LIMIT),
        name="mixer_prompt",
    )(proj, conv_w, arow, dtrow, normw, poolw, pscale)


def _mixer_sample_kernel(proj_ref, cst_ref, pst_ref, sin_ref, convw_ref, arow_ref, dtrow_ref, normw_ref,
                         poolw_ref, pscale_ref, mix_ref, sout_ref, *, ns, seq, start):
    n = ns * seq
    sshift = seq.bit_length() - 1
    rowi = lax.broadcasted_iota(jnp.int32, (n, 1), 0)
    tpos = rowi & (seq - 1)

    u = proj_ref[:, 0:C_QKV]
    st = cst_ref[...]
    cw = convw_ref[...]
    acc = u * cw[CONV_W - 1:CONV_W, :]
    for d in range(1, CONV_W):
        term = jnp.where(tpos >= d, pltpu.roll(u, d, 0), pltpu.roll(st, n - seq + d, 0))
        acc = acc + term * cw[CONV_W - 1 - d:CONV_W - d, :]
    y = _silu(acc)

    beta_s, gc_s, gl_s = _gate_slabs(proj_ref[:, C_BA:C_TOT], arow_ref[...], dtrow_ref[...], sshift)
    egc_s = jnp.exp(gc_s)
    ekg_s = jnp.exp(gl_s - gc_s)
    egl_s = jnp.exp(gl_s)
    qs, ks, us, ws_, qkds = _heads_prepare(y, beta_s, gc_s, egc_s, sshift)

    for h in range(N_HEADS):
        hs = slice(h * HEAD_DIM, (h + 1) * HEAD_DIM)
        la = LANE_A + h
        u_, w_, qkd = us[h], ws_[h], qkds[h][0]
        qg = qs[h] * egc_s[:, la:la + 1]
        kg_t = (ks[h] * ekg_s[:, la:la + 1]).T
        egl_rep = jnp.broadcast_to(egl_s[:, la:la + 1], (n, HEAD_DIM))
        ws_w, ws_q = [], []
        for si in range(ns):
            rows = slice(si * seq, (si + 1) * seq)
            ws = _dot(jnp.concatenate([w_[rows], qg[rows]], axis=0), sin_ref[si, h])
            ws_w.append(ws[:seq])
            ws_q.append(ws[seq:])
        vn = u_ - jnp.concatenate(ws_w, axis=0)
        o = jnp.concatenate(ws_q, axis=0) + _dot(qkd, vn)
        for si in range(ns):
            vmask = jnp.where((rowi >> sshift) == si, vn, 0.0)
            sout_ref[si, h] = sin_ref[si, h] * egl_rep[si * seq:si * seq + 1, :] + _dot(kg_t, vmask)
        mix_ref[:, hs] = _gated_rmsnorm(o, proj_ref[:, C_Z + h * HEAD_DIM:C_Z + (h + 1) * HEAD_DIM], normw_ref[...])

    p = proj_ref[:, C_P:C_P + D_B]
    pst = pst_ref[...]
    r, c = _iota2(n, n)
    band_new_base = ((r >> sshift) == (c >> sshift)) & (r >= c)
    r2, c2 = _iota2(n, ns * 16)
    same2 = (r2 >> sshift) == (c2 >> 4)
    t2 = r2 & (seq - 1)
    j2 = c2 & 15
    pos = start + tpos
    for gi, w in enumerate(POOL_WINDOWS):
        gs = slice(gi * HEAD_DIM, (gi + 1) * HEAD_DIM)
        band_new = jnp.where(band_new_base & ((r - c) < w), 1.0, 0.0).astype(BF16)
        band_st = jnp.where(same2 & (j2 >= 17 + t2 - w), 1.0, 0.0).astype(BF16)
        s = _dot01(band_new, p[:, gs]) + _dot01(band_st, pst[:, gs])
        cnt = jnp.minimum(pos + 1, w).astype(F32)
        mix_ref[:, D_A + gi * HEAD_DIM:D_A + (gi + 1) * HEAD_DIM] = _pool_out(
            s, cnt, p[:, gs], poolw_ref[gi], pscale_ref[:, gs])


def _mixer_sample(proj, cst, pst, sin, conv_w, arow, dtrow, normw, poolw, pscale, ns, seq, start):
    t = proj.shape[0]
    nb = t // seq
    n = ns * seq
    assert n == GDN_BLOCK and seq & (seq - 1) == 0 and seq >= CONV_W - 1
    const1 = lambda i: (0, 0)
    return pl.pallas_call(
        functools.partial(_mixer_sample_kernel, ns=ns, seq=seq, start=start),
        out_shape=(jax.ShapeDtypeStruct((t, D_MODEL), F32),
                   jax.ShapeDtypeStruct((nb, N_HEADS, HEAD_DIM, HEAD_DIM), F32)),
        grid=(nb // ns,),
        in_specs=[pl.BlockSpec((n, C_TOT), lambda i: (i, 0)),
                  pl.BlockSpec((n, C_QKV), lambda i: (i, 0)),
                  pl.BlockSpec((ns * 16, D_B), lambda i: (i, 0)),
                  pl.BlockSpec((ns, N_HEADS, HEAD_DIM, HEAD_DIM), lambda i: (i, 0, 0, 0)),
                  pl.BlockSpec((CONV_W, C_QKV), const1),
                  pl.BlockSpec((1, 128), const1),
                  pl.BlockSpec((1, 128), const1),
                  pl.BlockSpec((1, HEAD_DIM), const1),
                  pl.BlockSpec((N_GROUPS, HEAD_DIM, HEAD_DIM), lambda i: (0, 0, 0)),
                  pl.BlockSpec((1, D_B), const1)],
        out_specs=(pl.BlockSpec((n, D_MODEL), lambda i: (i, 0)),
                   pl.BlockSpec((ns, N_HEADS, HEAD_DIM, HEAD_DIM), lambda i: (i, 0, 0, 0))),
        compiler_params=pltpu.CompilerParams(dimension_semantics=("parallel",), vmem_limit_bytes=VMEM_LIMIT),
        name="mixer_sample",
    )(proj, cst, pst, sin, conv_w, arow, dtrow, normw, poolw, pscale)


ROW_SLAB = D_MODEL // 128


def _to_slabs(ref, x, n):
    for c in range(ROW_SLAB):
        ref[pl.ds(c, n, stride=ROW_SLAB), :] = x[:, c * 128:(c + 1) * 128]


def _from_slabs(ref, n):
    return jnp.concatenate([ref[pl.ds(c, n, stride=ROW_SLAB), :] for c in range(ROW_SLAB)], axis=1)


def _slab(ref, row):
    if isinstance(row, int):
        return ref.at[pl.ds(row * ROW_SLAB, ROW_SLAB)]
    return ref.at[pl.ds(pl.multiple_of(row * ROW_SLAB, ROW_SLAB), ROW_SLAB)]


def _layer_norm(x, g, b):
    mu = jnp.mean(x, axis=-1, keepdims=True)
    xc = x - mu
    var = jnp.mean(xc * xc, axis=-1, keepdims=True)
    return xc * lax.rsqrt(var + LN_EPS) * g + b


def _outproj_router_kernel(mixp_ref, xp_ref, mixs_ref, xs_ref, wout_ref, g1_ref, b1_ref, wrh_ref, wrl_ref, br_ref,
                           h_ref, sel_ref, gw_ref, cnt_ref, carry_ref, *, n_prompt_tiles):
    @pl.when(pl.program_id(0) == 0)
    def _init():
        carry_ref[...] = jnp.zeros_like(carry_ref)

    is_prompt = pl.program_id(0) < n_prompt_tiles
    mix = jnp.where(is_prompt, mixp_ref[...], mixs_ref[...])
    x = jnp.where(is_prompt, xp_ref[...], xs_ref[...])
    h = _layer_norm(ALPHA * x + _dot(mix, wout_ref[...]), g1_ref[...], b1_ref[...])
    tm = h.shape[0]
    _to_slabs(h_ref, h, tm)
    hh, hm, _ = _split3(h)
    f = lambda a, b: jnp.dot(a, b, preferred_element_type=F32)
    logits = f(hh, wrh_ref[...]) + (f(hm, wrh_ref[...]) + f(hh, wrl_ref[...])) + br_ref[...]
    lane = lax.broadcasted_iota(jnp.int32, (tm, 128), 1)
    big = jnp.int32(1 << 20)
    neg = -jnp.inf
    gmask = (lane >= N_EXPERTS) & (lane < N_EXPERTS + N_GROUPS)
    lg = jnp.where(gmask, logits, neg)
    gmax = jnp.max(lg, axis=1, keepdims=True)
    gidx = jnp.min(jnp.where(lg == gmax, lane - N_EXPERTS, big), axis=1, keepdims=True)
    pg = 1.0 / jnp.sum(jnp.where(gmask, jnp.exp(logits - gmax), 0.0), axis=1, keepdims=True)
    emask = (lane < N_EXPERTS) & ((lane >> 3) == gidx)
    le = jnp.where(emask, logits, neg)
    v1 = jnp.max(le, axis=1, keepdims=True)
    i1 = jnp.min(jnp.where((le == v1) & emask, lane, big), axis=1, keepdims=True)
    emask2 = emask & (lane != i1)
    le2 = jnp.where(emask2, logits, neg)
    v2 = jnp.max(le2, axis=1, keepdims=True)
    i2 = jnp.min(jnp.where((le2 == v2) & emask2, lane, big), axis=1, keepdims=True)
    e2 = jnp.exp(v2 - v1)
    den = 1.0 + e2
    gw_ref[...] = jnp.where(lane == 0, (1.0 / den) * pg, jnp.where(lane == 1, (e2 / den) * pg, 0.0))
    onehot = jnp.where((lane == i1) | (lane == i2), 1.0, 0.0)
    r, c = _iota2(tm, tm)
    before = jnp.dot(jnp.where(r > c, 1.0, 0.0).astype(BF16), onehot.astype(BF16), preferred_element_type=F32)
    before = before + carry_ref[...]
    r1 = jnp.sum(jnp.where(lane == i1, before, 0.0), axis=1, keepdims=True).astype(jnp.int32)
    r2 = jnp.sum(jnp.where(lane == i2, before, 0.0), axis=1, keepdims=True).astype(jnp.int32)
    carry_ref[...] += jnp.sum(onehot, axis=0, keepdims=True)
    cnt_ref[...] = carry_ref[...].astype(jnp.int32)
    sel_ref[...] = jnp.where(lane == 0, i1, jnp.where(lane == 1, i2, jnp.where(lane == 2, r1,
                                                                               jnp.where(lane == 3, r2, 0))))


def _outproj_router(mix_p, x_p, mix_s, x_s, wout, g1, b1, wrh, wrl, br, tm):
    tp, ts = x_p.shape[0], x_s.shape[0]
    t = tp + ts
    npt = tp // tm
    row = lambda i: (i, 0)
    prow = lambda i: (jnp.minimum(i, npt - 1), 0)
    srow = lambda i: (jnp.maximum(i - npt, 0), 0)
    const = lambda i: (0, 0)
    return pl.pallas_call(
        functools.partial(_outproj_router_kernel, n_prompt_tiles=npt),
        out_shape=(jax.ShapeDtypeStruct((t * ROW_SLAB, 128), F32), jax.ShapeDtypeStruct((t, 128), jnp.int32),
                   jax.ShapeDtypeStruct((t, 128), F32), jax.ShapeDtypeStruct((1, 128), jnp.int32)),
        grid=(t // tm,),
        in_specs=[pl.BlockSpec((tm, D_MODEL), prow), pl.BlockSpec((tm, D_MODEL), prow),
                  pl.BlockSpec((tm, D_MODEL), srow), pl.BlockSpec((tm, D_MODEL), srow),
                  pl.BlockSpec((D_MODEL, D_MODEL), const), pl.BlockSpec((1, D_MODEL), const),
                  pl.BlockSpec((1, D_MODEL), const), pl.BlockSpec((D_MODEL, 128), const),
                  pl.BlockSpec((D_MODEL, 128), const), pl.BlockSpec((1, 128), const)],
        out_specs=(pl.BlockSpec((tm * ROW_SLAB, 128), row), pl.BlockSpec((tm, 128), row),
                   pl.BlockSpec((tm, 128), row), pl.BlockSpec((1, 128), const)),
        scratch_shapes=[pltpu.VMEM((1, 128), F32)],
        compiler_params=pltpu.CompilerParams(dimension_semantics=("arbitrary",), vmem_limit_bytes=VMEM_LIMIT),
        name="outproj_router",
    )(mix_p, x_p, mix_s, x_s, wout, g1, b1, wrh, wrl, br)


EXPERT_TILE = 512
ROUTE_TILE = 256


def _experts_kernel(tile_e_ref, tile_ok_ref, src_ref, nsrc_ref, h_ref, wg_ref, wu_ref, wd_ref, o_ref,
                    stage_ref, sems, *, tmx, nt):
    j = pl.program_id(0)
    slot = j & 1
    ok = tile_ok_ref[j] != 0
    has_next = jnp.logical_and(j + 1 < nt, tile_ok_ref[jnp.minimum(j + 1, nt - 1)] != 0)

    def gather(idx_ref, s, start):
        for r in range(tmx):
            cp = pltpu.make_async_copy(_slab(h_ref, idx_ref[0, 0, r]), _slab(stage_ref.at[s], r), sems.at[s])
            if start:
                cp.start(priority=r & 1)
            else:
                cp.wait()

    @pl.when(j == 0)
    def _first():
        gather(src_ref, 0, True)

    @pl.when(has_next)
    def _prefetch():
        gather(nsrc_ref, 1 - slot, True)

    @pl.when(ok)
    def _compute():
        gather(src_ref, slot, False)
        x = _from_slabs(stage_ref.at[slot], tmx).astype(BF16)
        a = jnp.dot(x, wg_ref[0].astype(BF16), preferred_element_type=F32)
        b = jnp.dot(x, wu_ref[0].astype(BF16), preferred_element_type=F32)
        act = (_silu(a) * b).astype(BF16)
        _to_slabs(o_ref, jnp.dot(act, wd_ref[0].astype(BF16), preferred_element_type=F32), tmx)

    @pl.when(jnp.logical_not(ok))
    def _unused_tile():
        o_ref[...] = jnp.zeros_like(o_ref)


def _experts(h, src3, tile_e, tile_ok, wg, wu, wd, tmx):
    nt = tile_e.shape[0]
    wsel = lambda j, te, ok: (te[j], 0, 0)
    return pl.pallas_call(
        functools.partial(_experts_kernel, tmx=tmx, nt=nt),
        out_shape=jax.ShapeDtypeStruct((nt * tmx * ROW_SLAB, 128), F32),
        grid_spec=pltpu.PrefetchScalarGridSpec(
            num_scalar_prefetch=2,
            grid=(nt,),
            in_specs=[pl.BlockSpec((1, 1, tmx), lambda j, te, ok: (j, 0, 0), memory_space=pltpu.SMEM),
                      pl.BlockSpec((1, 1, tmx), lambda j, te, ok: (jnp.minimum(j + 1, nt - 1), 0, 0),
                                   memory_space=pltpu.SMEM),
                      pl.BlockSpec(memory_space=pl.ANY),
                      pl.BlockSpec((1, D_MODEL, D_EXPERT), wsel),
                      pl.BlockSpec((1, D_MODEL, D_EXPERT), wsel),
                      pl.BlockSpec((1, D_EXPERT, D_MODEL), wsel)],
            out_specs=pl.BlockSpec((tmx * ROW_SLAB, 128), lambda j, te, ok: (j, 0)),
            scratch_shapes=[pltpu.VMEM((2, tmx * ROW_SLAB, 128), F32), pltpu.SemaphoreType.DMA((2,))]),
        compiler_params=pltpu.CompilerParams(dimension_semantics=("arbitrary",), vmem_limit_bytes=VMEM_LIMIT),
        name="moe_experts",
    )(tile_e, tile_ok, src3, src3, h, wg, wu, wd)


def _combine_kernel(h_ref, gw_ref, pos_ref, g2_ref, b2_ref, os_ref, y_ref, stage_ref, sem, *, td):
    copies = []
    for t in range(td):
        for k in range(2):
            cp = pltpu.make_async_copy(_slab(os_ref, pos_ref[0, 0, 2 * t + k]), _slab(stage_ref.at[k], t), sem)
            cp.start(priority=k)
            copies.append(cp)
    for cp in copies:
        cp.wait()
    gw = gw_ref[...]
    moe = gw[:, 0:1] * _from_slabs(stage_ref.at[0], td) + gw[:, 1:2] * _from_slabs(stage_ref.at[1], td)
    y_ref[...] = _layer_norm(ALPHA * _from_slabs(h_ref, td) + moe, g2_ref[...], b2_ref[...])


def _combine(h, gw, pos3, g2, b2, os, td, first_token, n_tokens):
    off = first_token // td
    row = lambda i: (i + off, 0)
    const = lambda i: (0, 0)
    return pl.pallas_call(
        functools.partial(_combine_kernel, td=td),
        out_shape=jax.ShapeDtypeStruct((n_tokens, D_MODEL), F32),
        grid=(n_tokens // td,),
        in_specs=[pl.BlockSpec((td * ROW_SLAB, 128), row), pl.BlockSpec((td, 128), row),
                  pl.BlockSpec((1, 1, 2 * td), lambda i: (i + off, 0, 0), memory_space=pltpu.SMEM),
                  pl.BlockSpec((1, D_MODEL), const), pl.BlockSpec((1, D_MODEL), const),
                  pl.BlockSpec(memory_space=pl.ANY)],
        out_specs=pl.BlockSpec((td, D_MODEL), lambda i: (i, 0)),
        scratch_shapes=[pltpu.VMEM((2, td * ROW_SLAB, 128), F32), pltpu.SemaphoreType.DMA],
        compiler_params=pltpu.CompilerParams(dimension_semantics=("arbitrary",), vmem_limit_bytes=VMEM_LIMIT),
        name="moe_combine",
    )(h, gw, pos3, g2, b2, os)


def _route_plan(sel, cnt, t, tmx, td):
    i32 = jnp.int32
    counts = cnt[0, :N_EXPERTS]
    padded = ((counts + tmx - 1) // tmx) * tmx
    ex = jnp.arange(N_EXPERTS, dtype=i32)
    lower = ex[None, :] <= ex[:, None]
    ends = jnp.sum(jnp.where(lower, padded[None, :], 0), axis=1).astype(i32)
    offs = ends - padded
    first = (jnp.sum(jnp.where(lower, counts[None, :], 0), axis=1) - counts).astype(i32)
    pos = (jnp.sum(jnp.where(sel[:, 0:2, None] == ex, offs, 0), axis=-1) + sel[:, 2:4]).astype(i32)
    nt = 2 * t // tmx + N_EXPERTS
    n_used = ends[-1] // tmx
    tile = jnp.arange(nt, dtype=i32)
    tile_e = jnp.minimum(jnp.sum((ends[None, :] <= (jnp.minimum(tile, n_used - 1) * tmx)[:, None]).astype(i32),
                                 axis=1), N_EXPERTS - 1)
    tile_ok = (tile < n_used).astype(i32)
    pair_id = jnp.arange(2 * t, dtype=i32).reshape(t, 2)
    sorted_tok = (jnp.sort((sel[:, 0:2] * 65536 + pair_id).reshape(-1)) & 0xFFFF) >> 1
    onehot_e = tile_e[:, None] == ex[None, :]
    t_offs = jnp.sum(jnp.where(onehot_e, offs[None, :], 0), axis=1)
    t_first = jnp.sum(jnp.where(onehot_e, first[None, :], 0), axis=1)
    t_cnt = jnp.sum(jnp.where(onehot_e, counts[None, :], 0), axis=1)
    rank = (tile * tmx - t_offs)[:, None] + jnp.arange(tmx, dtype=i32)[None, :]
    valid = (rank < t_cnt[:, None]) & (tile_ok[:, None] != 0)
    src = jnp.where(valid, jnp.take(sorted_tok, jnp.clip(t_first[:, None] + rank, 0, 2 * t - 1)), 0).astype(i32)
    return pos.reshape(t // td, 1, 2 * td), src.reshape(nt, 1, tmx), tile_e.astype(i32), tile_ok


def _tile(t, want):
    tm = min(want, t)
    while t % tm:
        tm //= 2
    return tm


def _prep_weights(w_in, conv_w, a_log, dt_bias, gdn_norm_w, pool_w, pool_scale, w_out, ln1_g, ln1_b,
                  w_rg, b_rg, w_re, b_re, w_gate, w_up, w_down, ln2_g, ln2_b):
    col_b = 4 * D_A
    col_p = 4 * D_A + 2 * N_HEADS
    w_cat = jnp.concatenate([w_in[:, :col_b], w_in[:, col_p:], w_in[:, col_b:col_p],
                             jnp.zeros((D_MODEL, 128 - 2 * N_HEADS), w_in.dtype)], axis=1).astype(BF16)
    lane_pad = lambda v, off: jnp.zeros((1, 128), F32).at[0, off:off + v.shape[0]].set(v.astype(F32))
    w_r = jnp.concatenate([w_re, w_rg, jnp.zeros((D_MODEL, 128 - N_EXPERTS - N_GROUPS), F32)], axis=1)
    wrh = w_r.astype(BF16)
    wrl = (w_r - wrh.astype(F32)).astype(BF16)
    b_r = jnp.zeros((1, 128), F32).at[0, :N_EXPERTS].set(b_re).at[0, N_EXPERTS:N_EXPERTS + N_GROUPS].set(b_rg)
    return dict(
        w_cat=w_cat, conv_w=conv_w, arow=lane_pad(a_log, LANE_A), dtrow=lane_pad(dt_bias, LANE_A),
        normw=gdn_norm_w.reshape(1, HEAD_DIM), poolw=pool_w.astype(BF16), pscale=pool_scale.reshape(1, D_B),
        wout=w_out.astype(BF16), g1=ln1_g.reshape(1, D_MODEL), b1=ln1_b.reshape(1, D_MODEL),
        wrh=wrh, wrl=wrl, br=b_r,
        wg=w_gate.reshape(N_EXPERTS, D_MODEL, D_EXPERT), wu=w_up.reshape(N_EXPERTS, D_MODEL, D_EXPERT),
        wd=w_down.reshape(N_EXPERTS, D_EXPERT, D_MODEL),
        g2=ln2_g.reshape(1, D_MODEL), b2=ln2_b.reshape(1, D_MODEL))


def _post_mixer(mix_p, x_p, mix_s, x_s, p):
    tp, ts = x_p.shape[0], x_s.shape[0]
    t = tp + ts
    tm = math.gcd(_tile(tp, 512), _tile(ts, 512))
    h, sel, gw, cnt = _outproj_router(mix_p, x_p, mix_s, x_s, p["wout"], p["g1"], p["b1"], p["wrh"], p["wrl"],
                                      p["br"], tm)
    td = math.gcd(_tile(tp, ROUTE_TILE), _tile(ts, ROUTE_TILE))
    pos3, src3, tile_e, tile_ok = _route_plan(sel, cnt, t, EXPERT_TILE, td)
    os = _experts(h, src3, tile_e, tile_ok, p["wg"], p["wu"], p["wd"], EXPERT_TILE)
    y_p = _combine(h, gw, pos3, p["g2"], p["b2"], os, td, 0, tp)
    y_s = _combine(h, gw, pos3, p["g2"], p["b2"], os, td, tp, ts)
    return y_p, y_s


def _mix_prompt(x, p, lb=256):
    b, seq, _ = x.shape
    x2d = x.reshape(b * seq, D_MODEL)
    proj = _in_proj(x2d, p["w_cat"], _tile(b * seq, 1024)).reshape(b, seq, C_TOT)
    mix, s_fin = _mixer_prompt(proj, p["conv_w"], p["arow"], p["dtrow"], p["normw"], p["poolw"], p["pscale"],
                               min(lb, seq))
    conv_new = proj[:, seq - (CONV_W - 1):, 0:C_QKV]
    pool_new = proj[:, seq - POOL_BUF:, C_P:C_P + D_B]
    return x2d, mix.reshape(b * seq, D_MODEL), s_fin, conv_new, pool_new


def _mix_sample(x, s0, conv0, pool0, start, p, ns=16):
    b, seq, _ = x.shape
    x2d = x.reshape(b * seq, D_MODEL)
    proj = _in_proj(x2d, p["w_cat"], _tile(b * seq, 1024))
    cst = jnp.pad(conv0, ((0, 0), (seq - (CONV_W - 1), 0), (0, 0))).reshape(b * seq, C_QKV)
    pst = jnp.pad(pool0, ((0, 0), (1, 0), (0, 0))).reshape(b * 16, D_B)
    mix, s_new = _mixer_sample(proj, cst, pst, s0, p["conv_w"], p["arow"], p["dtrow"], p["normw"], p["poolw"],
                               p["pscale"], min(ns, b), seq, start)
    proj3 = proj.reshape(b, seq, C_TOT)
    conv_new = proj3[:, seq - (CONV_W - 1):, 0:C_QKV]
    pool_new = jnp.concatenate([pool0[:, seq:, :], proj3[:, :, C_P:C_P + D_B]], axis=1)
    return x2d, mix, s_new, conv_new, pool_new


def _layer(x_prompt, x_sample, s0, conv0, pool0, start, p):
    xp2d, mix_p, dp, cp, pp = _mix_prompt(x_prompt, p)
    xs2d, mix_s, ds, cs, ps = _mix_sample(x_sample, s0, conv0, pool0, start, p)
    y_p, y_s = _post_mixer(mix_p, xp2d, mix_s, xs2d, p)
    return y_p.reshape(x_prompt.shape), y_s.reshape(x_sample.shape), (dp, cp, pp), (ds, cs, ps)


def kernel(x_prompt, x_sample, state_delta, state_conv, state_pool, w_in, conv_w, a_log, dt_bias, gdn_norm_w,
           pool_w, pool_scale, w_out, ln1_g, ln1_b, w_rg, b_rg, w_re, b_re, w_gate, w_up, w_down, ln2_g, ln2_b):
    depth = w_in.shape[0]
    past_len = 16384
    yp, ys = x_prompt, x_sample
    outs = [[] for _ in range(6)]
    for l in range(depth):
        p = _prep_weights(w_in[l], conv_w[l], a_log[l], dt_bias[l], gdn_norm_w[l], pool_w[l], pool_scale[l],
                          w_out[l], ln1_g[l], ln1_b[l], w_rg[l], b_rg[l], w_re[l], b_re[l], w_gate[l], w_up[l],
                          w_down[l], ln2_g[l], ln2_b[l])
        yp, ys, st_p, st_s = _layer(yp, ys, state_delta[l], state_conv[l], state_pool[l], past_len, p)
        for lst, v in zip(outs, st_p + st_s):
            lst.append(v)
    return (yp, ys) + tuple(jnp.stack(v) for v in outs)
```

```python
import functools
import math

import jax
import jax.numpy as jnp
from jax import lax
from jax.experimental import pallas as pl
from jax.experimental.pallas import tpu as pltpu

F32 = jnp.float32
BF16 = jnp.bfloat16

D_MODEL = 1024
D_A = 512
D_B = 512
HEAD_DIM = 128
N_HEADS = 4
CONV_W = 4
CHUNK_SHIFT = 6
GDN_BLOCK = 128
POOL_WINDOWS = (2, 4, 8, 16)
POOL_BUF = 15
N_GROUPS = 4
E_PER_GROUP = 8
N_EXPERTS = N_GROUPS * E_PER_GROUP
D_EXPERT = 256
ALPHA = 2.0 ** 0.25
LN_EPS = 1e-5
RMS_EPS = 1e-6
L2_EPS = 1e-6

C_QKV = 3 * D_A
C_Z = 3 * D_A
C_P = 4 * D_A
C_BA = 4 * D_A + D_B
C_TOT = C_BA + 128
LANE_B = 0
LANE_A = N_HEADS

VMEM_LIMIT = 56 * 1024 * 1024


def _dot(a, b):
    return jnp.dot(a.astype(BF16), b.astype(BF16), preferred_element_type=F32)


def _dot_nt(a, b):
    return lax.dot_general(a.astype(BF16), b.astype(BF16), (((1,), (1,)), ((), ())), preferred_element_type=F32)


def _split3(x):
    hi = x.astype(BF16)
    r = x - hi.astype(F32)
    mid = r.astype(BF16)
    lo = (r - mid.astype(F32)).astype(BF16)
    return hi, mid, lo


def _dot01(m01, x):
    hi, mid, lo = _split3(x)
    f = lambda p: jnp.dot(m01, p, preferred_element_type=F32)
    return f(hi) + f(mid) + f(lo)


def _silu(x):
    return x * jax.nn.sigmoid(x)


def _softplus(x):
    return jnp.maximum(x, 0.0) + jnp.log1p(jnp.exp(-jnp.abs(x)))


def _iota2(n, m):
    return lax.broadcasted_iota(jnp.int32, (n, m), 0), lax.broadcasted_iota(jnp.int32, (n, m), 1)


def _proj_kernel(x_ref, w_ref, o_ref):
    o_ref[...] = jnp.dot(x_ref[...].astype(BF16), w_ref[...], preferred_element_type=F32)


def _in_proj(x2d, w_cat, tm):
    t = x2d.shape[0]
    return pl.pallas_call(
        _proj_kernel,
        out_shape=jax.ShapeDtypeStruct((t, C_TOT), F32),
        grid=(t // tm,),
        in_specs=[pl.BlockSpec((tm, D_MODEL), lambda i: (i, 0)),
                  pl.BlockSpec((D_MODEL, C_TOT), lambda i: (0, 0))],
        out_specs=pl.BlockSpec((tm, C_TOT), lambda i: (i, 0)),
        compiler_params=pltpu.CompilerParams(dimension_semantics=("parallel",), vmem_limit_bytes=VMEM_LIMIT),
        name="in_proj",
    )(x2d, w_cat)


def _unit_lower_inverse(a_list, r, c, chunk_shift):
    b0 = min(4, chunk_shift)
    eye = jnp.where(r == c, 1.0, 0.0).astype(F32)
    blk = (r >> b0) == (c >> b0)
    xs = [jnp.where(blk, a, 0.0) for a in a_list]
    ts = [eye - x for x in xs]
    for _ in range(b0 - 1):
        xs = [_dot(x, x) for x in xs]
        ts = [t + _dot(t, x) for t, x in zip(ts, xs)]
    for lvl in range(b0, chunk_shift):
        m = ((r >> (lvl + 1)) == (c >> (lvl + 1))) & ((r >> lvl) != (c >> lvl))
        tmp = [_dot(t, jnp.where(m, a, 0.0)) for t, a in zip(ts, a_list)]
        ts = [t - _dot(x, t) for t, x in zip(ts, tmp)]
    return ts


def _gate_slabs(ba, arow, dtrow, chunk_shift):
    n = ba.shape[0]
    beta = jax.nn.sigmoid(ba)
    g = -jnp.exp(arow) * _softplus(ba + dtrow)
    r, c = _iota2(n, n)
    same = (r >> chunk_shift) == (c >> chunk_shift)
    ltri = jnp.where(same & (r >= c), 1.0, 0.0).astype(BF16)
    lall = jnp.where(same, 1.0, 0.0).astype(BF16)
    cs = _dot01(jnp.concatenate([ltri, lall], axis=0), g)
    return beta, cs[:n], cs[n:]


def _heads_prepare(y, beta_s, gc_s, egc_s, chunk_shift):
    n = y.shape[0]
    nb = n // GDN_BLOCK
    r, c = _iota2(GDN_BLOCK, GDN_BLOCK)
    same = (r >> chunk_shift) == (c >> chunk_shift)
    incl = same & (r >= c)
    strict = same & (r > c)
    gc_t = gc_s.T
    qs, ks, a_list, rhs, decays = [], [], [], [], []
    for h in range(N_HEADS):
        q = _l2norm(y[:, h * HEAD_DIM:(h + 1) * HEAD_DIM]) * (HEAD_DIM ** -0.5)
        k = _l2norm(y[:, D_A + h * HEAD_DIM:D_A + (h + 1) * HEAD_DIM])
        v = y[:, 2 * D_A + h * HEAD_DIM:2 * D_A + (h + 1) * HEAD_DIM]
        la = LANE_A + h
        beta_c = beta_s[:, LANE_B + h:LANE_B + h + 1]
        kb = k * beta_c
        rhs_h = jnp.concatenate([v * beta_c, kb * egc_s[:, la:la + 1]], axis=1)
        for bi in range(nb):
            blk = slice(bi * GDN_BLOCK, (bi + 1) * GDN_BLOCK)
            decay = jnp.exp(jnp.where(incl, gc_s[blk, la:la + 1] - gc_t[la:la + 1, blk], -jnp.inf))
            a_list.append(jnp.where(strict, _dot_nt(kb[blk], k[blk]) * decay, 0.0))
            rhs.append(rhs_h[blk])
            decays.append(decay)
        qs.append(q)
        ks.append(k)
    ts = _unit_lower_inverse(a_list, r, c, chunk_shift)
    sols = [_dot(t, x) for t, x in zip(ts, rhs)]
    us, ws, qkds = [], [], []
    for h in range(N_HEADS):
        sol = jnp.concatenate(sols[h * nb:(h + 1) * nb], axis=0) if nb > 1 else sols[h]
        us.append(sol[:, :HEAD_DIM])
        ws.append(sol[:, HEAD_DIM:])
        qkds.append([_dot_nt(qs[h][bi * GDN_BLOCK:(bi + 1) * GDN_BLOCK], ks[h][bi * GDN_BLOCK:(bi + 1) * GDN_BLOCK])
                     * decays[h * nb + bi] for bi in range(nb)])
    return qs, ks, us, ws, qkds


def _l2norm(x):
    return x * lax.rsqrt(jnp.sum(x * x, axis=-1, keepdims=True) + L2_EPS)


def _gated_rmsnorm(o, z, normw):
    o = o * lax.rsqrt(jnp.mean(o * o, axis=-1, keepdims=True) + RMS_EPS) * normw
    return o * _silu(z)


def _pool_out(s, cnt, p_g, poolw_g, pscale_g):
    d = s / cnt - p_g
    return _dot(d, poolw_g) * pscale_g


def _mixer_prompt_kernel(proj_ref, convw_ref, arow_ref, dtrow_ref, normw_ref, poolw_ref, pscale_ref,
                         mix_ref, sfin_ref, cc_ref, pc_ref, s_ref, *, lb):
    l = pl.program_id(1)
    n = lb
    csz = 1 << CHUNK_SHIFT

    @pl.when(l == 0)
    def _init():
        cc_ref[...] = jnp.zeros_like(cc_ref)
        pc_ref[...] = jnp.zeros_like(pc_ref)
        s_ref[...] = jnp.zeros_like(s_ref)

    u = proj_ref[0, :, 0:C_QKV]
    ext = jnp.concatenate([cc_ref[...], u], axis=0)
    cw = convw_ref[...]
    acc = ext * cw[CONV_W - 1:CONV_W, :]
    for d in range(1, CONV_W):
        acc = acc + pltpu.roll(ext, d, 0) * cw[CONV_W - 1 - d:CONV_W - d, :]
    cc_ref[...] = u[n - 8:n, :]
    y = _silu(acc[8:, :])

    beta_s, gc_s, gl_s = _gate_slabs(proj_ref[0, :, C_BA:C_TOT], arow_ref[...], dtrow_ref[...], CHUNK_SHIFT)
    egc_s = jnp.exp(gc_s)
    ekg_s = jnp.exp(gl_s - gc_s)
    egl_s = jnp.exp(gl_s)

    qs, ks, us, ws, qkds = _heads_prepare(y, beta_s, gc_s, egc_s, CHUNK_SHIFT)

    zero = jnp.zeros((csz, 2 * HEAD_DIM), F32)
    n_chunks = n // csz
    qps, ops, kns, egl_reps = [], [], [], []
    for h in range(N_HEADS):
        la = LANE_A + h
        wu = jnp.concatenate([ws[h], us[h]], axis=1)
        qw = jnp.concatenate([_dot(qkd, wu[bi * GDN_BLOCK:(bi + 1) * GDN_BLOCK])
                              for bi, qkd in enumerate(qkds[h])], axis=0)
        qps.append(qs[h] * egc_s[:, la:la + 1] - qw[:, :HEAD_DIM])
        ops.append(qw[:, HEAD_DIM:])
        kg_t = (ks[h] * ekg_s[:, la:la + 1]).T
        kn = []
        for ci in range(n_chunks):
            rows = slice(ci * csz, (ci + 1) * csz)
            pair = slice((ci // 2) * 2 * csz, (ci // 2 + 1) * 2 * csz)
            half = jnp.concatenate([wu[rows], zero] if ci % 2 == 0 else [zero, wu[rows]], axis=0)
            kn.append(_dot(kg_t[:, pair], half))
        kns.append(kn)
        egl_reps.append(jnp.broadcast_to(egl_s[:, la:la + 1], (n, HEAD_DIM)))

    states = [s_ref[h] for h in range(N_HEADS)]
    outs = [[] for _ in range(N_HEADS)]
    for ci in range(n_chunks):
        rows = slice(ci * csz, (ci + 1) * csz)
        for h in range(N_HEADS):
            s = states[h]
            outs[h].append(_dot(qps[h][rows], s) + ops[h][rows])
            kn = kns[h][ci]
            states[h] = (s * egl_reps[h][ci * csz:ci * csz + 1, :] - _dot(kn[:, :HEAD_DIM], s)) + kn[:, HEAD_DIM:]
    for h in range(N_HEADS):
        hs = slice(h * HEAD_DIM, (h + 1) * HEAD_DIM)
        s_ref[h] = states[h]
        o = jnp.concatenate(outs[h], axis=0)
        mix_ref[0, :, hs] = _gated_rmsnorm(o, proj_ref[0, :, C_Z + h * HEAD_DIM:C_Z + (h + 1) * HEAD_DIM],
                                           normw_ref[...])

    sfin_ref[0] = s_ref[...]

    p = proj_ref[0, :, C_P:C_P + D_B]
    extp = jnp.concatenate([pc_ref[...], p], axis=0)
    pc_ref[...] = p[n - 16:n, :]
    r, c = _iota2(n, n + 16)
    lag = r + 16 - c
    pos = l * n + lax.broadcasted_iota(jnp.int32, (n, 1), 0)
    for gi, w in enumerate(POOL_WINDOWS):
        gs = slice(gi * HEAD_DIM, (gi + 1) * HEAD_DIM)
        band = jnp.where((lag >= 0) & (lag < w), 1.0, 0.0).astype(BF16)
        cnt = jnp.minimum(pos + 1, w).astype(F32)
        mix_ref[0, :, D_A + gi * HEAD_DIM:D_A + (gi + 1) * HEAD_DIM] = _pool_out(
            _dot01(band, extp[:, gs]), cnt, p[:, gs], poolw_ref[gi], pscale_ref[:, gs])


def _mixer_prompt(proj, conv_w, arow, dtrow, normw, poolw, pscale, lb):
    b, seq, _ = proj.shape
    const2 = lambda i, j: (0, 0)
    return pl.pallas_call(
        functools.partial(_mixer_prompt_kernel, lb=lb),
        out_shape=(jax.ShapeDtypeStruct((b, seq, D_MODEL), F32),
                   jax.ShapeDtypeStruct((b, N_HEADS, HEAD_DIM, HEAD_DIM), F32)),
        grid=(b, seq // lb),
        in_specs=[pl.BlockSpec((1, lb, C_TOT), lambda i, j: (i, j, 0)),
                  pl.BlockSpec((CONV_W, C_QKV), const2),
                  pl.BlockSpec((1, 128), const2),
                  pl.BlockSpec((1, 128), const2),
                  pl.BlockSpec((1, HEAD_DIM), const2),
                  pl.BlockSpec((N_GROUPS, HEAD_DIM, HEAD_DIM), lambda i, j: (0, 0, 0)),
                  pl.BlockSpec((1, D_B), const2)],
        out_specs=(pl.BlockSpec((1, lb, D_MODEL), lambda i, j: (i, j, 0)),
                   pl.BlockSpec((1, N_HEADS, HEAD_DIM, HEAD_DIM), lambda i, j: (i, 0, 0, 0))),
        scratch_shapes=[pltpu.VMEM((8, C_QKV), F32), pltpu.VMEM((16, D_B), F32),
                        pltpu.VMEM((N_HEADS, HEAD_DIM, HEAD_DIM), F32)],
        compiler_params=pltpu.CompilerParams(dimension_semantics=("parallel", "arbitrary"),
                                             vmem_limit_bytes=VMEM_LIMIT),
        name="mixer_prompt",
    )(proj, conv_w, arow, dtrow, normw, poolw, pscale)


def _mixer_sample_kernel(proj_ref, cst_ref, pst_ref, sin_ref, convw_ref, arow_ref, dtrow_ref, normw_ref,
                         poolw_ref, pscale_ref, mix_ref, sout_ref, *, ns, seq, start):
    n = ns * seq
    sshift = seq.bit_length() - 1
    rowi = lax.broadcasted_iota(jnp.int32, (n, 1), 0)
    tpos = rowi & (seq - 1)

    u = proj_ref[:, 0:C_QKV]
    st = cst_ref[...]
    cw = convw_ref[...]
    acc = u * cw[CONV_W - 1:CONV_W, :]
    for d in range(1, CONV_W):
        term = jnp.where(tpos >= d, pltpu.roll(u, d, 0), pltpu.roll(st, n - seq + d, 0))
        acc = acc + term * cw[CONV_W - 1 - d:CONV_W - d, :]
    y = _silu(acc)

    beta_s, gc_s, gl_s = _gate_slabs(proj_ref[:, C_BA:C_TOT], arow_ref[...], dtrow_ref[...], sshift)
    egc_s = jnp.exp(gc_s)
    ekg_s = jnp.exp(gl_s - gc_s)
    egl_s = jnp.exp(gl_s)
    qs, ks, us, ws_, qkds = _heads_prepare(y, beta_s, gc_s, egc_s, sshift)

    for h in range(N_HEADS):
        hs = slice(h * HEAD_DIM, (h + 1) * HEAD_DIM)
        la = LANE_A + h
        u_, w_, qkd = us[h], ws_[h], qkds[h][0]
        qg = qs[h] * egc_s[:, la:la + 1]
        kg_t = (ks[h] * ekg_s[:, la:la + 1]).T
        egl_rep = jnp.broadcast_to(egl_s[:, la:la + 1], (n, HEAD_DIM))
        ws_w, ws_q = [], []
        for si in range(ns):
            rows = slice(si * seq, (si + 1) * seq)
            ws = _dot(jnp.concatenate([w_[rows], qg[rows]], axis=0), sin_ref[si, h])
            ws_w.append(ws[:seq])
            ws_q.append(ws[seq:])
        vn = u_ - jnp.concatenate(ws_w, axis=0)
        o = jnp.concatenate(ws_q, axis=0) + _dot(qkd, vn)
        for si in range(ns):
            vmask = jnp.where((rowi >> sshift) == si, vn, 0.0)
            sout_ref[si, h] = sin_ref[si, h] * egl_rep[si * seq:si * seq + 1, :] + _dot(kg_t, vmask)
        mix_ref[:, hs] = _gated_rmsnorm(o, proj_ref[:, C_Z + h * HEAD_DIM:C_Z + (h + 1) * HEAD_DIM], normw_ref[...])

    p = proj_ref[:, C_P:C_P + D_B]
    pst = pst_ref[...]
    r, c = _iota2(n, n)
    band_new_base = ((r >> sshift) == (c >> sshift)) & (r >= c)
    r2, c2 = _iota2(n, ns * 16)
    same2 = (r2 >> sshift) == (c2 >> 4)
    t2 = r2 & (seq - 1)
    j2 = c2 & 15
    pos = start + tpos
    for gi, w in enumerate(POOL_WINDOWS):
        gs = slice(gi * HEAD_DIM, (gi + 1) * HEAD_DIM)
        band_new = jnp.where(band_new_base & ((r - c) < w), 1.0, 0.0).astype(BF16)
        band_st = jnp.where(same2 & (j2 >= 17 + t2 - w), 1.0, 0.0).astype(BF16)
        s = _dot01(band_new, p[:, gs]) + _dot01(band_st, pst[:, gs])
        cnt = jnp.minimum(pos + 1, w).astype(F32)
        mix_ref[:, D_A + gi * HEAD_DIM:D_A + (gi + 1) * HEAD_DIM] = _pool_out(
            s, cnt, p[:, gs], poolw_ref[gi], pscale_ref[:, gs])


def _mixer_sample(proj, cst, pst, sin, conv_w, arow, dtrow, normw, poolw, pscale, ns, seq, start):
    t = proj.shape[0]
    nb = t // seq
    n = ns * seq
    assert n == GDN_BLOCK and seq & (seq - 1) == 0 and seq >= CONV_W - 1
    const1 = lambda i: (0, 0)
    return pl.pallas_call(
        functools.partial(_mixer_sample_kernel, ns=ns, seq=seq, start=start),
        out_shape=(jax.ShapeDtypeStruct((t, D_MODEL), F32),
                   jax.ShapeDtypeStruct((nb, N_HEADS, HEAD_DIM, HEAD_DIM), F32)),
        grid=(nb // ns,),
        in_specs=[pl.BlockSpec((n, C_TOT), lambda i: (i, 0)),
                  pl.BlockSpec((n, C_QKV), lambda i: (i, 0)),
                  pl.BlockSpec((ns * 16, D_B), lambda i: (i, 0)),
                  pl.BlockSpec((ns, N_HEADS, HEAD_DIM, HEAD_DIM), lambda i: (i, 0, 0, 0)),
                  pl.BlockSpec((CONV_W, C_QKV), const1),
                  pl.BlockSpec((1, 128), const1),
                  pl.BlockSpec((1, 128), const1),
                  pl.BlockSpec((1, HEAD_DIM), const1),
                  pl.BlockSpec((N_GROUPS, HEAD_DIM, HEAD_DIM), lambda i: (0, 0, 0)),
                  pl.BlockSpec((1, D_B), const1)],
        out_specs=(pl.BlockSpec((n, D_MODEL), lambda i: (i, 0)),
                   pl.BlockSpec((ns, N_HEADS, HEAD_DIM, HEAD_DIM), lambda i: (i, 0, 0, 0))),
        compiler_params=pltpu.CompilerParams(dimension_semantics=("parallel",), vmem_limit_bytes=VMEM_LIMIT),
        name="mixer_sample",
    )(proj, cst, pst, sin, conv_w, arow, dtrow, normw, poolw, pscale)


ROW_SLAB = D_MODEL // 128


def _to_slabs(ref, x, n):
    for c in range(ROW_SLAB):
        ref[pl.ds(c, n, stride=ROW_SLAB), :] = x[:, c * 128:(c + 1) * 128]


def _from_slabs(ref, n):
    return jnp.concatenate([ref[pl.ds(c, n, stride=ROW_SLAB), :] for c in range(ROW_SLAB)], axis=1)


def _slab(ref, row):
    if isinstance(row, int):
        return ref.at[pl.ds(row * ROW_SLAB, ROW_SLAB)]
    return ref.at[pl.ds(pl.multiple_of(row * ROW_SLAB, ROW_SLAB), ROW_SLAB)]


def _layer_norm(x, g, b):
    mu = jnp.mean(x, axis=-1, keepdims=True)
    xc = x - mu
    var = jnp.mean(xc * xc, axis=-1, keepdims=True)
    return xc * lax.rsqrt(var + LN_EPS) * g + b


def _outproj_router_kernel(mixp_ref, xp_ref, mixs_ref, xs_ref, wout_ref, g1_ref, b1_ref, wrh_ref, wrl_ref, br_ref,
                           h_ref, sel_ref, gw_ref, cnt_ref, carry_ref, *, n_prompt_tiles):
    @pl.when(pl.program_id(0) == 0)
    def _init():
        carry_ref[...] = jnp.zeros_like(carry_ref)

    is_prompt = pl.program_id(0) < n_prompt_tiles
    tm = h_ref.shape[0] // ROW_SLAB
    n_parts = 2 if tm % 16 == 0 else 1
    pm = tm // n_parts
    parts = [slice(i * pm, (i + 1) * pm) for i in range(n_parts)]
    each = lambda fn, *lists: [fn(*args) for args in zip(*lists)]
    lane = lax.broadcasted_iota(jnp.int32, (pm, 128), 1)
    big = jnp.int32(1 << 20)
    neg = -jnp.inf
    lsum = lambda v: jnp.sum(v, axis=1, keepdims=True)
    lmax = lambda v: jnp.max(v, axis=1, keepdims=True)
    lmin = lambda v: jnp.min(v, axis=1, keepdims=True)

    mix = [jnp.where(is_prompt, mixp_ref[rs, :], mixs_ref[rs, :]) for rs in parts]
    x = [jnp.where(is_prompt, xp_ref[rs, :], xs_ref[rs, :]) for rs in parts]
    proj = each(lambda m: _dot(m, wout_ref[...]), mix)
    h = each(lambda xi, pi: _layer_norm(ALPHA * xi + pi, g1_ref[...], b1_ref[...]), x, proj)
    for i, hi in enumerate(h):
        for c in range(ROW_SLAB):
            h_ref[pl.ds(i * pm * ROW_SLAB + c, pm, stride=ROW_SLAB), :] = hi[:, c * 128:(c + 1) * 128]
    f = lambda a, b: jnp.dot(a, b, preferred_element_type=F32)
    split = each(_split3, h)
    logits = each(lambda s: f(s[0], wrh_ref[...]) + (f(s[1], wrh_ref[...]) + f(s[0], wrl_ref[...])) + br_ref[...],
                  split)
    gmask = (lane >= N_EXPERTS) & (lane < N_EXPERTS + N_GROUPS)
    lg = each(lambda l: jnp.where(gmask, l, neg), logits)
    gmax = each(lmax, lg)
    gidx = each(lambda l, m: lmin(jnp.where(l == m, lane - N_EXPERTS, big)), lg, gmax)
    pg = each(lambda l, m: 1.0 / lsum(jnp.where(gmask, jnp.exp(l - m), 0.0)), logits, gmax)
    emask = each(lambda g: (lane < N_EXPERTS) & ((lane >> 3) == g), gidx)
    le = each(lambda m, l: jnp.where(m, l, neg), emask, logits)
    v1 = each(lmax, le)
    i1 = each(lambda l, v, m: lmin(jnp.where((l == v) & m, lane, big)), le, v1, emask)
    emask2 = each(lambda m, i: m & (lane != i), emask, i1)
    le2 = each(lambda m, l: jnp.where(m, l, neg), emask2, logits)
    v2 = each(lmax, le2)
    i2 = each(lambda l, v, m: lmin(jnp.where((l == v) & m, lane, big)), le2, v2, emask2)
    e2 = each(lambda a, b: jnp.exp(a - b), v2, v1)
    for rs, e, g in zip(parts, e2, pg):
        den = 1.0 + e
        gw_ref[rs, :] = jnp.where(lane == 0, (1.0 / den) * g, jnp.where(lane == 1, (e / den) * g, 0.0))
    onehot = each(lambda a, b: jnp.where((lane == a) | (lane == b), 1.0, 0.0), i1, i2)
    r, c = _iota2(pm, pm)
    tri = jnp.where(r > c, 1.0, 0.0).astype(BF16)
    inside = each(lambda o: jnp.dot(tri, o.astype(BF16), preferred_element_type=F32), onehot)
    carry = carry_ref[...]
    for rs, o, ins, a, b in zip(parts, onehot, inside, i1, i2):
        before = ins + carry
        r1 = lsum(jnp.where(lane == a, before, 0.0)).astype(jnp.int32)
        r2 = lsum(jnp.where(lane == b, before, 0.0)).astype(jnp.int32)
        sel_ref[rs, :] = jnp.where(lane == 0, a, jnp.where(lane == 1, b, jnp.where(lane == 2, r1,
                                                                                    jnp.where(lane == 3, r2, 0))))
        carry = carry + jnp.sum(o, axis=0, keepdims=True)
    carry_ref[...] = carry
    cnt_ref[...] = carry.astype(jnp.int32)


def _outproj_router(mix_p, x_p, mix_s, x_s, wout, g1, b1, wrh, wrl, br, tm):
    tp, ts = x_p.shape[0], x_s.shape[0]
    t = tp + ts
    npt = tp // tm
    row = lambda i: (i, 0)
    prow = lambda i: (jnp.minimum(i, npt - 1), 0)
    srow = lambda i: (jnp.maximum(i - npt, 0), 0)
    const = lambda i: (0, 0)
    return pl.pallas_call(
        functools.partial(_outproj_router_kernel, n_prompt_tiles=npt),
        out_shape=(jax.ShapeDtypeStruct((t * ROW_SLAB, 128), F32), jax.ShapeDtypeStruct((t, 128), jnp.int32),
                   jax.ShapeDtypeStruct((t, 128), F32), jax.ShapeDtypeStruct((1, 128), jnp.int32)),
        grid=(t // tm,),
        in_specs=[pl.BlockSpec((tm, D_MODEL), prow), pl.BlockSpec((tm, D_MODEL), prow),
                  pl.BlockSpec((tm, D_MODEL), srow), pl.BlockSpec((tm, D_MODEL), srow),
                  pl.BlockSpec((D_MODEL, D_MODEL), const), pl.BlockSpec((1, D_MODEL), const),
                  pl.BlockSpec((1, D_MODEL), const), pl.BlockSpec((D_MODEL, 128), const),
                  pl.BlockSpec((D_MODEL, 128), const), pl.BlockSpec((1, 128), const)],
        out_specs=(pl.BlockSpec((tm * ROW_SLAB, 128), row), pl.BlockSpec((tm, 128), row),
                   pl.BlockSpec((tm, 128), row), pl.BlockSpec((1, 128), const)),
        scratch_shapes=[pltpu.VMEM((1, 128), F32)],
        compiler_params=pltpu.CompilerParams(dimension_semantics=("arbitrary",), vmem_limit_bytes=VMEM_LIMIT),
        name="outproj_router",
    )(mix_p, x_p, mix_s, x_s, wout, g1, b1, wrh, wrl, br)


EXPERT_TILE = 512
ROUTE_TILE = 256


def _dispatch_kernel(pad_start_ref, pad_cnt_ref, tail_ref, h_ref, pos_ref, xs_ref, zero_ref, sem, zsem, tsem,
                     *, td, tmx):
    @pl.when(pl.program_id(0) == 0)
    def _zero_unused_rows():
        zero_ref[...] = jnp.zeros_like(zero_ref)

        def zcopy(row, n_rows):
            return pltpu.make_async_copy(
                zero_ref.at[pl.ds(0, n_rows * ROW_SLAB)],
                xs_ref.at[pl.ds(pl.multiple_of(row * ROW_SLAB, ROW_SLAB), n_rows * ROW_SLAB)], zsem)

        tile_rows = min(td, tmx) * ROW_SLAB
        tcopy = lambda tile: pltpu.make_async_copy(
            zero_ref.at[pl.ds(0, tile_rows)],
            xs_ref.at[pl.ds(pl.multiple_of(tile * tile_rows, tile_rows), tile_rows)], tsem)

        def pad_pieces(e, start_not_wait):
            start = pad_start_ref[e]
            n = pad_cnt_ref[e]
            piece = tmx // 2
            while piece >= 1:
                @pl.when((n & piece) != 0)
                def _(piece=piece):
                    cp = zcopy(start + (n & ~(2 * piece - 1)), piece)
                    cp.start() if start_not_wait else cp.wait()
                piece //= 2

        def start_pads(e, carry):
            pad_pieces(e, True)
            return carry

        def wait_pads(e, carry):
            pad_pieces(e, False)
            return carry

        lax.fori_loop(0, N_EXPERTS, start_pads, 0)

        def tail_start(r, carry):
            tcopy(tail_ref[0] + r).start()
            return carry

        lax.fori_loop(0, tail_ref[1], tail_start, 0)
        lax.fori_loop(0, N_EXPERTS, wait_pads, 0)

        def tail_wait(r, carry):
            tcopy(0).wait()
            return carry

        lax.fori_loop(0, tail_ref[1], tail_wait, 0)

    copies = []
    for t in range(td):
        for k in range(2):
            cp = pltpu.make_async_copy(_slab(h_ref, t), _slab(xs_ref, pos_ref[0, 0, 2 * t + k]), sem)
            cp.start(priority=k)
            copies.append(cp)
    for cp in copies:
        cp.wait()


def _dispatch(h, pos3, pad_start, pad_cnt, tail, n_rows, td, tmx):
    t = h.shape[0] // ROW_SLAB
    assert td % min(td, tmx) == 0 and tmx % min(td, tmx) == 0 and tmx // 2 <= td
    return pl.pallas_call(
        functools.partial(_dispatch_kernel, td=td, tmx=tmx),
        out_shape=jax.ShapeDtypeStruct((n_rows * ROW_SLAB, 128), F32),
        grid_spec=pltpu.PrefetchScalarGridSpec(
            num_scalar_prefetch=3,
            grid=(t // td,),
            in_specs=[pl.BlockSpec((td * ROW_SLAB, 128), lambda i, *_: (i, 0)),
                      pl.BlockSpec((1, 1, 2 * td), lambda i, *_: (i, 0, 0), memory_space=pltpu.SMEM)],
            out_specs=pl.BlockSpec(memory_space=pl.ANY),
            scratch_shapes=[pltpu.VMEM((td * ROW_SLAB, 128), F32),
                            pltpu.SemaphoreType.DMA, pltpu.SemaphoreType.DMA, pltpu.SemaphoreType.DMA]),
        compiler_params=pltpu.CompilerParams(dimension_semantics=("arbitrary",), vmem_limit_bytes=VMEM_LIMIT),
        name="moe_dispatch",
    )(pad_start, pad_cnt, tail, h, pos3)


def _experts_kernel(tile_idx_ref, tile_e_ref, tile_ok_ref, x_ref, wg_ref, wu_ref, wd_ref, o_ref, *, tmx):
    ok = tile_ok_ref[pl.program_id(0)] != 0

    @pl.when(ok)
    def _compute():
        x = _from_slabs(x_ref, tmx).astype(BF16)
        a = jnp.dot(x, wg_ref[0].astype(BF16), preferred_element_type=F32)
        b = jnp.dot(x, wu_ref[0].astype(BF16), preferred_element_type=F32)
        act = (_silu(a) * b).astype(BF16)
        _to_slabs(o_ref, jnp.dot(act, wd_ref[0].astype(BF16), preferred_element_type=F32), tmx)

    @pl.when(jnp.logical_not(ok))
    def _unused_tile():
        o_ref[...] = jnp.zeros_like(o_ref)


def _experts(xs, tile_idx, tile_e, tile_ok, wg, wu, wd, tmx):
    nt = tile_idx.shape[0]
    rows = lambda j, ti, te, ok: (ti[j], 0)
    wsel = lambda j, ti, te, ok: (te[j], 0, 0)
    own = lambda j, ti, te, ok: (j, 0)
    return pl.pallas_call(
        functools.partial(_experts_kernel, tmx=tmx),
        out_shape=jax.ShapeDtypeStruct(xs.shape, F32),
        grid_spec=pltpu.PrefetchScalarGridSpec(
            num_scalar_prefetch=3,
            grid=(nt,),
            in_specs=[pl.BlockSpec((tmx * ROW_SLAB, 128), rows),
                      pl.BlockSpec((1, D_MODEL, D_EXPERT), wsel),
                      pl.BlockSpec((1, D_MODEL, D_EXPERT), wsel),
                      pl.BlockSpec((1, D_EXPERT, D_MODEL), wsel)],
            out_specs=pl.BlockSpec((tmx * ROW_SLAB, 128), own)),
        compiler_params=pltpu.CompilerParams(dimension_semantics=("arbitrary",), vmem_limit_bytes=VMEM_LIMIT),
        name="moe_experts",
    )(tile_idx, tile_e, tile_ok, xs, wg, wu, wd)


def _combine_kernel(h_ref, gw_ref, pos_ref, g2_ref, b2_ref, os_ref, y_ref, stage_ref, sem, *, td):
    copies = []
    for t in range(td):
        for k in range(2):
            cp = pltpu.make_async_copy(_slab(os_ref, pos_ref[0, 0, 2 * t + k]), _slab(stage_ref.at[k], t), sem)
            cp.start(priority=k)
            copies.append(cp)
    for cp in copies:
        cp.wait()
    gw = gw_ref[...]
    moe = gw[:, 0:1] * _from_slabs(stage_ref.at[0], td) + gw[:, 1:2] * _from_slabs(stage_ref.at[1], td)
    y_ref[...] = _layer_norm(ALPHA * _from_slabs(h_ref, td) + moe, g2_ref[...], b2_ref[...])


def _combine(h, gw, pos3, g2, b2, os, td, first_token, n_tokens):
    off = first_token // td
    row = lambda i: (i + off, 0)
    const = lambda i: (0, 0)
    return pl.pallas_call(
        functools.partial(_combine_kernel, td=td),
        out_shape=jax.ShapeDtypeStruct((n_tokens, D_MODEL), F32),
        grid=(n_tokens // td,),
        in_specs=[pl.BlockSpec((td * ROW_SLAB, 128), row), pl.BlockSpec((td, 128), row),
                  pl.BlockSpec((1, 1, 2 * td), lambda i: (i + off, 0, 0), memory_space=pltpu.SMEM),
                  pl.BlockSpec((1, D_MODEL), const), pl.BlockSpec((1, D_MODEL), const),
                  pl.BlockSpec(memory_space=pl.ANY)],
        out_specs=pl.BlockSpec((td, D_MODEL), lambda i: (i, 0)),
        scratch_shapes=[pltpu.VMEM((2, td * ROW_SLAB, 128), F32), pltpu.SemaphoreType.DMA],
        compiler_params=pltpu.CompilerParams(dimension_semantics=("arbitrary",), vmem_limit_bytes=VMEM_LIMIT),
        name="moe_combine",
    )(h, gw, pos3, g2, b2, os)


def _route_plan(sel, cnt, t, tmx, td):
    i32 = jnp.int32
    counts = cnt[0, :N_EXPERTS]
    padded = ((counts + tmx - 1) // tmx) * tmx
    ex = jnp.arange(N_EXPERTS, dtype=i32)
    ends = jnp.sum(jnp.where(ex[None, :] <= ex[:, None], padded[None, :], 0), axis=1).astype(i32)
    offs = ends - padded
    pos = (jnp.sum(jnp.where(sel[:, 0:2, None] == ex, offs, 0), axis=-1) + sel[:, 2:4]).astype(i32)
    nt = 2 * t // tmx + N_EXPERTS
    n_used = ends[-1] // tmx
    tile = jnp.arange(nt, dtype=i32)
    tile_idx = jnp.minimum(tile, jnp.maximum(n_used - 1, 0))
    tile_e = jnp.minimum(jnp.sum((ends[None, :] <= (tile_idx * tmx)[:, None]).astype(i32), axis=1), N_EXPERTS - 1)
    tile_ok = (tile < n_used).astype(i32)
    pieces = tmx // min(td, tmx)
    tail = jnp.stack([n_used * pieces, (nt - n_used) * pieces]).astype(i32)
    return (pos.reshape(t // td, 1, 2 * td), tile_idx, tile_e, tile_ok, (offs + counts).astype(i32),
            (padded - counts).astype(i32), tail, nt * tmx)


def _tile(t, want):
    tm = min(want, t)
    while t % tm:
        tm //= 2
    return tm


def _prep_weights(w_in, conv_w, a_log, dt_bias, gdn_norm_w, pool_w, pool_scale, w_out, ln1_g, ln1_b,
                  w_rg, b_rg, w_re, b_re, w_gate, w_up, w_down, ln2_g, ln2_b):
    col_b = 4 * D_A
    col_p = 4 * D_A + 2 * N_HEADS
    w_cat = jnp.concatenate([w_in[:, :col_b], w_in[:, col_p:], w_in[:, col_b:col_p],
                             jnp.zeros((D_MODEL, 128 - 2 * N_HEADS), w_in.dtype)], axis=1).astype(BF16)
    lane_pad = lambda v, off: jnp.zeros((1, 128), F32).at[0, off:off + v.shape[0]].set(v.astype(F32))
    w_r = jnp.concatenate([w_re, w_rg, jnp.zeros((D_MODEL, 128 - N_EXPERTS - N_GROUPS), F32)], axis=1)
    wrh = w_r.astype(BF16)
    wrl = (w_r - wrh.astype(F32)).astype(BF16)
    b_r = jnp.zeros((1, 128), F32).at[0, :N_EXPERTS].set(b_re).at[0, N_EXPERTS:N_EXPERTS + N_GROUPS].set(b_rg)
    return dict(
        w_cat=w_cat, conv_w=conv_w, arow=lane_pad(a_log, LANE_A), dtrow=lane_pad(dt_bias, LANE_A),
        normw=gdn_norm_w.reshape(1, HEAD_DIM), poolw=pool_w.astype(BF16), pscale=pool_scale.reshape(1, D_B),
        wout=w_out.astype(BF16), g1=ln1_g.reshape(1, D_MODEL), b1=ln1_b.reshape(1, D_MODEL),
        wrh=wrh, wrl=wrl, br=b_r,
        wg=w_gate.reshape(N_EXPERTS, D_MODEL, D_EXPERT), wu=w_up.reshape(N_EXPERTS, D_MODEL, D_EXPERT),
        wd=w_down.reshape(N_EXPERTS, D_EXPERT, D_MODEL),
        g2=ln2_g.reshape(1, D_MODEL), b2=ln2_b.reshape(1, D_MODEL))


def _post_mixer(mix_p, x_p, mix_s, x_s, p):
    tp, ts = x_p.shape[0], x_s.shape[0]
    t = tp + ts
    tm = math.gcd(_tile(tp, 512), _tile(ts, 512))
    h, sel, gw, cnt = _outproj_router(mix_p, x_p, mix_s, x_s, p["wout"], p["g1"], p["b1"], p["wrh"], p["wrl"],
                                      p["br"], tm)
    td = math.gcd(_tile(tp, ROUTE_TILE), _tile(ts, ROUTE_TILE))
    pos3, tile_idx, tile_e, tile_ok, pad_start, pad_cnt, tail, n_rows = _route_plan(sel, cnt, t, EXPERT_TILE, td)
    xs = _dispatch(h, pos3, pad_start, pad_cnt, tail, n_rows, td, EXPERT_TILE)
    os = _experts(xs, tile_idx, tile_e, tile_ok, p["wg"], p["wu"], p["wd"], EXPERT_TILE)
    y_p = _combine(h, gw, pos3, p["g2"], p["b2"], os, td, 0, tp)
    y_s = _combine(h, gw, pos3, p["g2"], p["b2"], os, td, tp, ts)
    return y_p, y_s


def _mix_prompt(x, p, lb=256):
    b, seq, _ = x.shape
    x2d = x.reshape(b * seq, D_MODEL)
    proj = _in_proj(x2d, p["w_cat"], _tile(b * seq, 1024)).reshape(b, seq, C_TOT)
    mix, s_fin = _mixer_prompt(proj, p["conv_w"], p["arow"], p["dtrow"], p["normw"], p["poolw"], p["pscale"],
                               min(lb, seq))
    conv_new = proj[:, seq - (CONV_W - 1):, 0:C_QKV]
    pool_new = proj[:, seq - POOL_BUF:, C_P:C_P + D_B]
    return x2d, mix.reshape(b * seq, D_MODEL), s_fin, conv_new, pool_new


def _mix_sample(x, s0, conv0, pool0, start, p, ns=16):
    b, seq, _ = x.shape
    x2d = x.reshape(b * seq, D_MODEL)
    proj = _in_proj(x2d, p["w_cat"], _tile(b * seq, 1024))
    cst = jnp.pad(conv0, ((0, 0), (seq - (CONV_W - 1), 0), (0, 0))).reshape(b * seq, C_QKV)
    pst = jnp.pad(pool0, ((0, 0), (1, 0), (0, 0))).reshape(b * 16, D_B)
    mix, s_new = _mixer_sample(proj, cst, pst, s0, p["conv_w"], p["arow"], p["dtrow"], p["normw"], p["poolw"],
                               p["pscale"], min(ns, b), seq, start)
    proj3 = proj.reshape(b, seq, C_TOT)
    conv_new = proj3[:, seq - (CONV_W - 1):, 0:C_QKV]
    pool_new = jnp.concatenate([pool0[:, seq:, :], proj3[:, :, C_P:C_P + D_B]], axis=1)
    return x2d, mix, s_new, conv_new, pool_new


def _layer(x_prompt, x_sample, s0, conv0, pool0, start, p):
    xp2d, mix_p, dp, cp, pp = _mix_prompt(x_prompt, p)
    xs2d, mix_s, ds, cs, ps = _mix_sample(x_sample, s0, conv0, pool0, start, p)
    y_p, y_s = _post_mixer(mix_p, xp2d, mix_s, xs2d, p)
    return y_p.reshape(x_prompt.shape), y_s.reshape(x_sample.shape), (dp, cp, pp), (ds, cs, ps)


def kernel(x_prompt, x_sample, state_delta, state_conv, state_pool, w_in, conv_w, a_log, dt_bias, gdn_norm_w,
           pool_w, pool_scale, w_out, ln1_g, ln1_b, w_rg, b_rg, w_re, b_re, w_gate, w_up, w_down, ln2_g, ln2_b):
    depth = w_in.shape[0]
    past_len = 16384
    yp, ys = x_prompt, x_sample
    outs = [[] for _ in range(6)]
    for l in range(depth):
        p = _prep_weights(w_in[l], conv_w[l], a_log[l], dt_bias[l], gdn_norm_w[l], pool_w[l], pool_scale[l],
                          w_out[l], ln1_g[l], ln1_b[l], w_rg[l], b_rg[l], w_re[l], b_re[l], w_gate[l], w_up[l],
                          w_down[l], ln2_g[l], ln2_b[l])
        yp, ys, st_p, st_s = _layer(yp, ys, state_delta[l], state_conv[l], state_pool[l], past_len, p)
        for lst, v in zip(outs, st_p + st_s):
            lst.append(v)
    return (yp, ys) + tuple(jnp.stack(v) for v in outs)
```

```python
import functools
import math

import jax
import jax.numpy as jnp
from jax import lax
from jax.experimental import pallas as pl
from jax.experimental.pallas import tpu as pltpu

F32 = jnp.float32
BF16 = jnp.bfloat16

D_MODEL = 1024
D_A = 512
D_B = 512
HEAD_DIM = 128
N_HEADS = 4
CONV_W = 4
CHUNK_SHIFT = 6
GDN_BLOCK = 128
POOL_WINDOWS = (2, 4, 8, 16)
POOL_BUF = 15
N_GROUPS = 4
E_PER_GROUP = 8
N_EXPERTS = N_GROUPS * E_PER_GROUP
D_EXPERT = 256
ALPHA = 2.0 ** 0.25
LN_EPS = 1e-5
RMS_EPS = 1e-6
L2_EPS = 1e-6

C_QKV = 3 * D_A
C_Z = 3 * D_A
C_P = 4 * D_A
C_BA = 4 * D_A + D_B
C_TOT = C_BA + 128
LANE_B = 0
LANE_A = N_HEADS

VMEM_LIMIT = 56 * 1024 * 1024


def _dot(a, b):
    return jnp.dot(a.astype(BF16), b.astype(BF16), preferred_element_type=F32)


def _dot_nt(a, b):
    return lax.dot_general(a.astype(BF16), b.astype(BF16), (((1,), (1,)), ((), ())), preferred_element_type=F32)


def _split3(x):
    hi = x.astype(BF16)
    r = x - hi.astype(F32)
    mid = r.astype(BF16)
    lo = (r - mid.astype(F32)).astype(BF16)
    return hi, mid, lo


def _dot01(m01, x):
    hi, mid, lo = _split3(x)
    f = lambda p: jnp.dot(m01, p, preferred_element_type=F32)
    return f(hi) + f(mid) + f(lo)


def _silu(x):
    return x * jax.nn.sigmoid(x)


def _softplus(x):
    return jnp.maximum(x, 0.0) + jnp.log1p(jnp.exp(-jnp.abs(x)))


def _iota2(n, m):
    return lax.broadcasted_iota(jnp.int32, (n, m), 0), lax.broadcasted_iota(jnp.int32, (n, m), 1)


def _proj_kernel(x_ref, w_ref, o_ref):
    o_ref[...] = jnp.dot(x_ref[...].astype(BF16), w_ref[...], preferred_element_type=F32)


def _in_proj(x2d, w_cat, tm):
    t = x2d.shape[0]
    return pl.pallas_call(
        _proj_kernel,
        out_shape=jax.ShapeDtypeStruct((t, C_TOT), F32),
        grid=(t // tm,),
        in_specs=[pl.BlockSpec((tm, D_MODEL), lambda i: (i, 0)),
                  pl.BlockSpec((D_MODEL, C_TOT), lambda i: (0, 0))],
        out_specs=pl.BlockSpec((tm, C_TOT), lambda i: (i, 0)),
        compiler_params=pltpu.CompilerParams(dimension_semantics=("parallel",), vmem_limit_bytes=VMEM_LIMIT),
        name="in_proj",
    )(x2d, w_cat)


def _unit_lower_inverse(a_list, r, c, chunk_shift):
    b0 = min(4, chunk_shift)
    eye = jnp.where(r == c, 1.0, 0.0).astype(F32)
    blk = (r >> b0) == (c >> b0)
    xs = [jnp.where(blk, a, 0.0) for a in a_list]
    ts = [eye - x for x in xs]
    for _ in range(b0 - 1):
        xs = [_dot(x, x) for x in xs]
        ts = [t + _dot(t, x) for t, x in zip(ts, xs)]
    for lvl in range(b0, chunk_shift):
        m = ((r >> (lvl + 1)) == (c >> (lvl + 1))) & ((r >> lvl) != (c >> lvl))
        tmp = [_dot(t, jnp.where(m, a, 0.0)) for t, a in zip(ts, a_list)]
        ts = [t - _dot(x, t) for t, x in zip(ts, tmp)]
    return ts


def _gate_slabs(ba, arow, dtrow, chunk_shift):
    n = ba.shape[0]
    beta = jax.nn.sigmoid(ba)
    g = -jnp.exp(arow) * _softplus(ba + dtrow)
    r, c = _iota2(n, n)
    same = (r >> chunk_shift) == (c >> chunk_shift)
    ltri = jnp.where(same & (r >= c), 1.0, 0.0).astype(BF16)
    lall = jnp.where(same, 1.0, 0.0).astype(BF16)
    cs = _dot01(jnp.concatenate([ltri, lall], axis=0), g)
    return beta, cs[:n], cs[n:]


def _heads_prepare(y, beta_s, gc_s, egc_s, chunk_shift):
    n = y.shape[0]
    nb = n // GDN_BLOCK
    r, c = _iota2(GDN_BLOCK, GDN_BLOCK)
    same = (r >> chunk_shift) == (c >> chunk_shift)
    incl = same & (r >= c)
    strict = same & (r > c)
    gc_t = gc_s.T
    qs, ks, a_list, rhs, decays = [], [], [], [], []
    for h in range(N_HEADS):
        q = _l2norm(y[:, h * HEAD_DIM:(h + 1) * HEAD_DIM]) * (HEAD_DIM ** -0.5)
        k = _l2norm(y[:, D_A + h * HEAD_DIM:D_A + (h + 1) * HEAD_DIM])
        v = y[:, 2 * D_A + h * HEAD_DIM:2 * D_A + (h + 1) * HEAD_DIM]
        la = LANE_A + h
        beta_c = beta_s[:, LANE_B + h:LANE_B + h + 1]
        kb = k * beta_c
        rhs_h = jnp.concatenate([v * beta_c, kb * egc_s[:, la:la + 1]], axis=1)
        for bi in range(nb):
            blk = slice(bi * GDN_BLOCK, (bi + 1) * GDN_BLOCK)
            decay = jnp.exp(jnp.where(incl, gc_s[blk, la:la + 1] - gc_t[la:la + 1, blk], -jnp.inf))
            a_list.append(jnp.where(strict, _dot_nt(kb[blk], k[blk]) * decay, 0.0))
            rhs.append(rhs_h[blk])
            decays.append(decay)
        qs.append(q)
        ks.append(k)
    ts = _unit_lower_inverse(a_list, r, c, chunk_shift)
    sols = [_dot(t, x) for t, x in zip(ts, rhs)]
    us, ws, qkds = [], [], []
    for h in range(N_HEADS):
        sol = jnp.concatenate(sols[h * nb:(h + 1) * nb], axis=0) if nb > 1 else sols[h]
        us.append(sol[:, :HEAD_DIM])
        ws.append(sol[:, HEAD_DIM:])
        qkds.append([_dot_nt(qs[h][bi * GDN_BLOCK:(bi + 1) * GDN_BLOCK], ks[h][bi * GDN_BLOCK:(bi + 1) * GDN_BLOCK])
                     * decays[h * nb + bi] for bi in range(nb)])
    return qs, ks, us, ws, qkds


def _l2norm(x):
    return x * lax.rsqrt(jnp.sum(x * x, axis=-1, keepdims=True) + L2_EPS)


def _gated_rmsnorm(o, z, normw):
    o = o * lax.rsqrt(jnp.mean(o * o, axis=-1, keepdims=True) + RMS_EPS) * normw
    return o * _silu(z)


def _pool_out(s, cnt, p_g, poolw_g, pscale_g):
    d = s / cnt - p_g
    return _dot(d, poolw_g) * pscale_g


def _mixer_prompt_kernel(proj_ref, convw_ref, arow_ref, dtrow_ref, normw_ref, poolw_ref, pscale_ref,
                         mix_ref, sfin_ref, cc_ref, pc_ref, s_ref, *, lb):
    l = pl.program_id(1)
    n = lb
    csz = 1 << CHUNK_SHIFT

    @pl.when(l == 0)
    def _init():
        cc_ref[...] = jnp.zeros_like(cc_ref)
        pc_ref[...] = jnp.zeros_like(pc_ref)
        s_ref[...] = jnp.zeros_like(s_ref)

    u = proj_ref[0, :, 0:C_QKV]
    ext = jnp.concatenate([cc_ref[...], u], axis=0)
    cw = convw_ref[...]
    acc = ext * cw[CONV_W - 1:CONV_W, :]
    for d in range(1, CONV_W):
        acc = acc + pltpu.roll(ext, d, 0) * cw[CONV_W - 1 - d:CONV_W - d, :]
    cc_ref[...] = u[n - 8:n, :]
    y = _silu(acc[8:, :])

    beta_s, gc_s, gl_s = _gate_slabs(proj_ref[0, :, C_BA:C_TOT], arow_ref[...], dtrow_ref[...], CHUNK_SHIFT)
    egc_s = jnp.exp(gc_s)
    ekg_s = jnp.exp(gl_s - gc_s)
    egl_s = jnp.exp(gl_s)

    qs, ks, us, ws, qkds = _heads_prepare(y, beta_s, gc_s, egc_s, CHUNK_SHIFT)

    zero = jnp.zeros((csz, 2 * HEAD_DIM), F32)
    n_chunks = n // csz
    qps, ops, kns, egl_reps = [], [], [], []
    for h in range(N_HEADS):
        la = LANE_A + h
        wu = jnp.concatenate([ws[h], us[h]], axis=1)
        qw = jnp.concatenate([_dot(qkd, wu[bi * GDN_BLOCK:(bi + 1) * GDN_BLOCK])
                              for bi, qkd in enumerate(qkds[h])], axis=0)
        qps.append(qs[h] * egc_s[:, la:la + 1] - qw[:, :HEAD_DIM])
        ops.append(qw[:, HEAD_DIM:])
        kg_t = (ks[h] * ekg_s[:, la:la + 1]).T
        kn = []
        for ci in range(n_chunks):
            rows = slice(ci * csz, (ci + 1) * csz)
            pair = slice((ci // 2) * 2 * csz, (ci // 2 + 1) * 2 * csz)
            half = jnp.concatenate([wu[rows], zero] if ci % 2 == 0 else [zero, wu[rows]], axis=0)
            kn.append(_dot(kg_t[:, pair], half))
        kns.append(kn)
        egl_reps.append(jnp.broadcast_to(egl_s[:, la:la + 1], (n, HEAD_DIM)))

    states = [s_ref[h] for h in range(N_HEADS)]
    outs = [[] for _ in range(N_HEADS)]
    for ci in range(n_chunks):
        rows = slice(ci * csz, (ci + 1) * csz)
        for h in range(N_HEADS):
            s = states[h]
            outs[h].append(_dot(qps[h][rows], s) + ops[h][rows])
            kn = kns[h][ci]
            states[h] = (s * egl_reps[h][ci * csz:ci * csz + 1, :] - _dot(kn[:, :HEAD_DIM], s)) + kn[:, HEAD_DIM:]
    for h in range(N_HEADS):
        hs = slice(h * HEAD_DIM, (h + 1) * HEAD_DIM)
        s_ref[h] = states[h]
        o = jnp.concatenate(outs[h], axis=0)
        mix_ref[0, :, hs] = _gated_rmsnorm(o, proj_ref[0, :, C_Z + h * HEAD_DIM:C_Z + (h + 1) * HEAD_DIM],
                                           normw_ref[...])

    sfin_ref[0] = s_ref[...]

    p = proj_ref[0, :, C_P:C_P + D_B]
    extp = jnp.concatenate([pc_ref[...], p], axis=0)
    pc_ref[...] = p[n - 16:n, :]
    r, c = _iota2(n, n + 16)
    lag = r + 16 - c
    pos = l * n + lax.broadcasted_iota(jnp.int32, (n, 1), 0)
    for gi, w in enumerate(POOL_WINDOWS):
        gs = slice(gi * HEAD_DIM, (gi + 1) * HEAD_DIM)
        band = jnp.where((lag >= 0) & (lag < w), 1.0, 0.0).astype(BF16)
        cnt = jnp.minimum(pos + 1, w).astype(F32)
        mix_ref[0, :, D_A + gi * HEAD_DIM:D_A + (gi + 1) * HEAD_DIM] = _pool_out(
            _dot01(band, extp[:, gs]), cnt, p[:, gs], poolw_ref[gi], pscale_ref[:, gs])


def _mixer_prompt(proj, conv_w, arow, dtrow, normw, poolw, pscale, lb):
    b, seq, _ = proj.shape
    const2 = lambda i, j: (0, 0)
    return pl.pallas_call(
        functools.partial(_mixer_prompt_kernel, lb=lb),
        out_shape=(jax.ShapeDtypeStruct((b, seq, D_MODEL), F32),
                   jax.ShapeDtypeStruct((b, N_HEADS, HEAD_DIM, HEAD_DIM), F32)),
        grid=(b, seq // lb),
        in_specs=[pl.BlockSpec((1, lb, C_TOT), lambda i, j: (i, j, 0)),
                  pl.BlockSpec((CONV_W, C_QKV), const2),
                  pl.BlockSpec((1, 128), const2),
                  pl.BlockSpec((1, 128), const2),
                  pl.BlockSpec((1, HEAD_DIM), const2),
                  pl.BlockSpec((N_GROUPS, HEAD_DIM, HEAD_DIM), lambda i, j: (0, 0, 0)),
                  pl.BlockSpec((1, D_B), const2)],
        out_specs=(pl.BlockSpec((1, lb, D_MODEL), lambda i, j: (i, j, 0)),
                   pl.BlockSpec((1, N_HEADS, HEAD_DIM, HEAD_DIM), lambda i, j: (i, 0, 0, 0))),
        scratch_shapes=[pltpu.VMEM((8, C_QKV), F32), pltpu.VMEM((16, D_B), F32),
                        pltpu.VMEM((N_HEADS, HEAD_DIM, HEAD_DIM), F32)],
        compiler_params=pltpu.CompilerParams(dimension_semantics=("parallel", "arbitrary"),
                                             vmem_limit_bytes=VMEM_LIMIT),
        name="mixer_prompt",
    )(proj, conv_w, arow, dtrow, normw, poolw, pscale)


def _mixer_sample_kernel(proj_ref, cst_ref, pst_ref, sin_ref, convw_ref, arow_ref, dtrow_ref, normw_ref,
                         poolw_ref, pscale_ref, mix_ref, sout_ref, *, ns, seq, start):
    n = ns * seq
    sshift = seq.bit_length() - 1
    rowi = lax.broadcasted_iota(jnp.int32, (n, 1), 0)
    tpos = rowi & (seq - 1)

    u = proj_ref[:, 0:C_QKV]
    st = cst_ref[...]
    cw = convw_ref[...]
    acc = u * cw[CONV_W - 1:CONV_W, :]
    for d in range(1, CONV_W):
        term = jnp.where(tpos >= d, pltpu.roll(u, d, 0), pltpu.roll(st, n - seq + d, 0))
        acc = acc + term * cw[CONV_W - 1 - d:CONV_W - d, :]
    y = _silu(acc)

    beta_s, gc_s, gl_s = _gate_slabs(proj_ref[:, C_BA:C_TOT], arow_ref[...], dtrow_ref[...], sshift)
    egc_s = jnp.exp(gc_s)
    ekg_s = jnp.exp(gl_s - gc_s)
    egl_s = jnp.exp(gl_s)
    qs, ks, us, ws_, qkds = _heads_prepare(y, beta_s, gc_s, egc_s, sshift)

    for h in range(N_HEADS):
        hs = slice(h * HEAD_DIM, (h + 1) * HEAD_DIM)
        la = LANE_A + h
        u_, w_, qkd = us[h], ws_[h], qkds[h][0]
        qg = qs[h] * egc_s[:, la:la + 1]
        kg_t = (ks[h] * ekg_s[:, la:la + 1]).T
        egl_rep = jnp.broadcast_to(egl_s[:, la:la + 1], (n, HEAD_DIM))
        ws_w, ws_q = [], []
        for si in range(ns):
            rows = slice(si * seq, (si + 1) * seq)
            ws = _dot(jnp.concatenate([w_[rows], qg[rows]], axis=0), sin_ref[si, h])
            ws_w.append(ws[:seq])
            ws_q.append(ws[seq:])
        vn = u_ - jnp.concatenate(ws_w, axis=0)
        o = jnp.concatenate(ws_q, axis=0) + _dot(qkd, vn)
        for si in range(ns):
            vmask = jnp.where((rowi >> sshift) == si, vn, 0.0)
            sout_ref[si, h] = sin_ref[si, h] * egl_rep[si * seq:si * seq + 1, :] + _dot(kg_t, vmask)
        mix_ref[:, hs] = _gated_rmsnorm(o, proj_ref[:, C_Z + h * HEAD_DIM:C_Z + (h + 1) * HEAD_DIM], normw_ref[...])

    p = proj_ref[:, C_P:C_P + D_B]
    pst = pst_ref[...]
    r, c = _iota2(n, n)
    band_new_base = ((r >> sshift) == (c >> sshift)) & (r >= c)
    r2, c2 = _iota2(n, ns * 16)
    same2 = (r2 >> sshift) == (c2 >> 4)
    t2 = r2 & (seq - 1)
    j2 = c2 & 15
    pos = start + tpos
    for gi, w in enumerate(POOL_WINDOWS):
        gs = slice(gi * HEAD_DIM, (gi + 1) * HEAD_DIM)
        band_new = jnp.where(band_new_base & ((r - c) < w), 1.0, 0.0).astype(BF16)
        band_st = jnp.where(same2 & (j2 >= 17 + t2 - w), 1.0, 0.0).astype(BF16)
        s = _dot01(band_new, p[:, gs]) + _dot01(band_st, pst[:, gs])
        cnt = jnp.minimum(pos + 1, w).astype(F32)
        mix_ref[:, D_A + gi * HEAD_DIM:D_A + (gi + 1) * HEAD_DIM] = _pool_out(
            s, cnt, p[:, gs], poolw_ref[gi], pscale_ref[:, gs])


def _mixer_sample(proj, cst, pst, sin, conv_w, arow, dtrow, normw, poolw, pscale, ns, seq, start):
    t = proj.shape[0]
    nb = t // seq
    n = ns * seq
    assert n == GDN_BLOCK and seq & (seq - 1) == 0 and seq >= CONV_W - 1
    const1 = lambda i: (0, 0)
    return pl.pallas_call(
        functools.partial(_mixer_sample_kernel, ns=ns, seq=seq, start=start),
        out_shape=(jax.ShapeDtypeStruct((t, D_MODEL), F32),
                   jax.ShapeDtypeStruct((nb, N_HEADS, HEAD_DIM, HEAD_DIM), F32)),
        grid=(nb // ns,),
        in_specs=[pl.BlockSpec((n, C_TOT), lambda i: (i, 0)),
                  pl.BlockSpec((n, C_QKV), lambda i: (i, 0)),
                  pl.BlockSpec((ns * 16, D_B), lambda i: (i, 0)),
                  pl.BlockSpec((ns, N_HEADS, HEAD_DIM, HEAD_DIM), lambda i: (i, 0, 0, 0)),
                  pl.BlockSpec((CONV_W, C_QKV), const1),
                  pl.BlockSpec((1, 128), const1),
                  pl.BlockSpec((1, 128), const1),
                  pl.BlockSpec((1, HEAD_DIM), const1),
                  pl.BlockSpec((N_GROUPS, HEAD_DIM, HEAD_DIM), lambda i: (0, 0, 0)),
                  pl.BlockSpec((1, D_B), const1)],
        out_specs=(pl.BlockSpec((n, D_MODEL), lambda i: (i, 0)),
                   pl.BlockSpec((ns, N_HEADS, HEAD_DIM, HEAD_DIM), lambda i: (i, 0, 0, 0))),
        compiler_params=pltpu.CompilerParams(dimension_semantics=("parallel",), vmem_limit_bytes=VMEM_LIMIT),
        name="mixer_sample",
    )(proj, cst, pst, sin, conv_w, arow, dtrow, normw, poolw, pscale)


ROW_SLAB = D_MODEL // 128


def _to_slabs(ref, x, n):
    for c in range(ROW_SLAB):
        ref[pl.ds(c, n, stride=ROW_SLAB), :] = x[:, c * 128:(c + 1) * 128]


def _from_slabs(ref, n):
    return jnp.concatenate([ref[pl.ds(c, n, stride=ROW_SLAB), :] for c in range(ROW_SLAB)], axis=1)


def _slab(ref, row):
    if isinstance(row, int):
        return ref.at[pl.ds(row * ROW_SLAB, ROW_SLAB)]
    return ref.at[pl.ds(pl.multiple_of(row * ROW_SLAB, ROW_SLAB), ROW_SLAB)]


def _layer_norm(x, g, b):
    mu = jnp.mean(x, axis=-1, keepdims=True)
    xc = x - mu
    var = jnp.mean(xc * xc, axis=-1, keepdims=True)
    return xc * lax.rsqrt(var + LN_EPS) * g + b


def _outproj_router_kernel(mixp_ref, xp_ref, mixs_ref, xs_ref, wout_ref, g1_ref, b1_ref, wrh_ref, wrl_ref, br_ref,
                           h_ref, sel_ref, gw_ref, cnt_ref, carry_ref, *, n_prompt_tiles):
    @pl.when(pl.program_id(0) == 0)
    def _init():
        carry_ref[...] = jnp.zeros_like(carry_ref)

    is_prompt = pl.program_id(0) < n_prompt_tiles
    tm = h_ref.shape[0] // ROW_SLAB
    n_parts = 2 if tm % 16 == 0 else 1
    pm = tm // n_parts
    parts = [slice(i * pm, (i + 1) * pm) for i in range(n_parts)]
    each = lambda fn, *lists: [fn(*args) for args in zip(*lists)]
    lane = lax.broadcasted_iota(jnp.int32, (pm, 128), 1)
    big = jnp.int32(1 << 20)
    neg = -jnp.inf
    lsum = lambda v: jnp.sum(v, axis=1, keepdims=True)
    lmax = lambda v: jnp.max(v, axis=1, keepdims=True)
    lmin = lambda v: jnp.min(v, axis=1, keepdims=True)

    mix = [jnp.where(is_prompt, mixp_ref[rs, :], mixs_ref[rs, :]) for rs in parts]
    x = [jnp.where(is_prompt, xp_ref[rs, :], xs_ref[rs, :]) for rs in parts]
    proj = each(lambda m: _dot(m, wout_ref[...]), mix)
    h = each(lambda xi, pi: _layer_norm(ALPHA * xi + pi, g1_ref[...], b1_ref[...]), x, proj)
    for i, hi in enumerate(h):
        for c in range(ROW_SLAB):
            h_ref[pl.ds(i * pm * ROW_SLAB + c, pm, stride=ROW_SLAB), :] = hi[:, c * 128:(c + 1) * 128]
    f = lambda a, b: jnp.dot(a, b, preferred_element_type=F32)
    split = each(_split3, h)
    logits = each(lambda s: f(s[0], wrh_ref[...]) + (f(s[1], wrh_ref[...]) + f(s[0], wrl_ref[...])) + br_ref[...],
                  split)
    gmask = (lane >= N_EXPERTS) & (lane < N_EXPERTS + N_GROUPS)
    lg = each(lambda l: jnp.where(gmask, l, neg), logits)
    gmax = each(lmax, lg)
    gidx = each(lambda l, m: lmin(jnp.where(l == m, lane - N_EXPERTS, big)), lg, gmax)
    pg = each(lambda l, m: 1.0 / lsum(jnp.where(gmask, jnp.exp(l - m), 0.0)), logits, gmax)
    emask = each(lambda g: (lane < N_EXPERTS) & ((lane >> 3) == g), gidx)
    le = each(lambda m, l: jnp.where(m, l, neg), emask, logits)
    v1 = each(lmax, le)
    i1 = each(lambda l, v, m: lmin(jnp.where((l == v) & m, lane, big)), le, v1, emask)
    emask2 = each(lambda m, i: m & (lane != i), emask, i1)
    le2 = each(lambda m, l: jnp.where(m, l, neg), emask2, logits)
    v2 = each(lmax, le2)
    i2 = each(lambda l, v, m: lmin(jnp.where((l == v) & m, lane, big)), le2, v2, emask2)
    e2 = each(lambda a, b: jnp.exp(a - b), v2, v1)
    for rs, e, g in zip(parts, e2, pg):
        den = 1.0 + e
        gw_ref[rs, :] = jnp.where(lane == 0, (1.0 / den) * g, jnp.where(lane == 1, (e / den) * g, 0.0))
    onehot = each(lambda a, b: jnp.where((lane == a) | (lane == b), 1.0, 0.0), i1, i2)
    r, c = _iota2(pm, pm)
    tri = jnp.where(r > c, 1.0, 0.0).astype(BF16)
    inside = each(lambda o: jnp.dot(tri, o.astype(BF16), preferred_element_type=F32), onehot)
    carry = carry_ref[...]
    for rs, o, ins, a, b in zip(parts, onehot, inside, i1, i2):
        before = ins + carry
        r1 = lsum(jnp.where(lane == a, before, 0.0)).astype(jnp.int32)
        r2 = lsum(jnp.where(lane == b, before, 0.0)).astype(jnp.int32)
        sel_ref[rs, :] = jnp.where(lane == 0, a, jnp.where(lane == 1, b, jnp.where(lane == 2, r1,
                                                                                    jnp.where(lane == 3, r2, 0))))
        carry = carry + jnp.sum(o, axis=0, keepdims=True)
    carry_ref[...] = carry
    cnt_ref[...] = carry.astype(jnp.int32)


def _outproj_router(mix_p, x_p, mix_s, x_s, wout, g1, b1, wrh, wrl, br, tm):
    tp, ts = x_p.shape[0], x_s.shape[0]
    t = tp + ts
    npt = tp // tm
    row = lambda i: (i, 0)
    prow = lambda i: (jnp.minimum(i, npt - 1), 0)
    srow = lambda i: (jnp.maximum(i - npt, 0), 0)
    const = lambda i: (0, 0)
    return pl.pallas_call(
        functools.partial(_outproj_router_kernel, n_prompt_tiles=npt),
        out_shape=(jax.ShapeDtypeStruct((t * ROW_SLAB, 128), F32), jax.ShapeDtypeStruct((t, 128), jnp.int32),
                   jax.ShapeDtypeStruct((t, 128), F32), jax.ShapeDtypeStruct((1, 128), jnp.int32)),
        grid=(t // tm,),
        in_specs=[pl.BlockSpec((tm, D_MODEL), prow), pl.BlockSpec((tm, D_MODEL), prow),
                  pl.BlockSpec((tm, D_MODEL), srow), pl.BlockSpec((tm, D_MODEL), srow),
                  pl.BlockSpec((D_MODEL, D_MODEL), const), pl.BlockSpec((1, D_MODEL), const),
                  pl.BlockSpec((1, D_MODEL), const), pl.BlockSpec((D_MODEL, 128), const),
                  pl.BlockSpec((D_MODEL, 128), const), pl.BlockSpec((1, 128), const)],
        out_specs=(pl.BlockSpec((tm * ROW_SLAB, 128), row), pl.BlockSpec((tm, 128), row),
                   pl.BlockSpec((tm, 128), row), pl.BlockSpec((1, 128), const)),
        scratch_shapes=[pltpu.VMEM((1, 128), F32)],
        compiler_params=pltpu.CompilerParams(dimension_semantics=("arbitrary",), vmem_limit_bytes=VMEM_LIMIT),
        name="outproj_router",
    )(mix_p, x_p, mix_s, x_s, wout, g1, b1, wrh, wrl, br)


EXPERT_TILE = 512
ROUTE_TILE = 256


def _dispatch_kernel(pad_start_ref, pad_cnt_ref, tail_ref, h_ref, pos_ref, xs_ref, zero_ref, sem, zsem, tsem,
                     *, td, tmx):
    @pl.when(pl.program_id(0) == 0)
    def _zero_unused_rows():
        zero_ref[...] = jnp.zeros_like(zero_ref)

        def zcopy(row, n_rows):
            return pltpu.make_async_copy(
                zero_ref.at[pl.ds(0, n_rows * ROW_SLAB)],
                xs_ref.at[pl.ds(pl.multiple_of(row * ROW_SLAB, ROW_SLAB), n_rows * ROW_SLAB)], zsem)

        tile_rows = min(td, tmx) * ROW_SLAB
        tcopy = lambda tile: pltpu.make_async_copy(
            zero_ref.at[pl.ds(0, tile_rows)],
            xs_ref.at[pl.ds(pl.multiple_of(tile * tile_rows, tile_rows), tile_rows)], tsem)

        def pad_pieces(e, start_not_wait):
            start = pad_start_ref[e]
            n = pad_cnt_ref[e]
            piece = tmx // 2
            while piece >= 1:
                @pl.when((n & piece) != 0)
                def _(piece=piece):
                    cp = zcopy(start + (n & ~(2 * piece - 1)), piece)
                    cp.start() if start_not_wait else cp.wait()
                piece //= 2

        def start_pads(e, carry):
            pad_pieces(e, True)
            return carry

        def wait_pads(e, carry):
            pad_pieces(e, False)
            return carry

        lax.fori_loop(0, N_EXPERTS, start_pads, 0)

        def tail_start(r, carry):
            tcopy(tail_ref[0] + r).start()
            return carry

        lax.fori_loop(0, tail_ref[1], tail_start, 0)
        lax.fori_loop(0, N_EXPERTS, wait_pads, 0)

        def tail_wait(r, carry):
            tcopy(0).wait()
            return carry

        lax.fori_loop(0, tail_ref[1], tail_wait, 0)

    copies = []
    for t in range(td):
        for k in range(2):
            cp = pltpu.make_async_copy(_slab(h_ref, t), _slab(xs_ref, pos_ref[0, 0, 2 * t + k]), sem)
            cp.start(priority=k)
            copies.append(cp)
    for cp in copies:
        cp.wait()


def _dispatch(h, pos3, pad_start, pad_cnt, tail, n_rows, td, tmx):
    t = h.shape[0] // ROW_SLAB
    assert td % min(td, tmx) == 0 and tmx % min(td, tmx) == 0 and tmx // 2 <= td
    return pl.pallas_call(
        functools.partial(_dispatch_kernel, td=td, tmx=tmx),
        out_shape=jax.ShapeDtypeStruct((n_rows * ROW_SLAB, 128), F32),
        grid_spec=pltpu.PrefetchScalarGridSpec(
            num_scalar_prefetch=3,
            grid=(t // td,),
            in_specs=[pl.BlockSpec((td * ROW_SLAB, 128), lambda i, *_: (i, 0)),
                      pl.BlockSpec((1, 1, 2 * td), lambda i, *_: (i, 0, 0), memory_space=pltpu.SMEM)],
            out_specs=pl.BlockSpec(memory_space=pl.ANY),
            scratch_shapes=[pltpu.VMEM((td * ROW_SLAB, 128), F32),
                            pltpu.SemaphoreType.DMA, pltpu.SemaphoreType.DMA, pltpu.SemaphoreType.DMA]),
        compiler_params=pltpu.CompilerParams(dimension_semantics=("arbitrary",), vmem_limit_bytes=VMEM_LIMIT),
        name="moe_dispatch",
    )(pad_start, pad_cnt, tail, h, pos3)


N_RING = 3


def _experts_kernel(tile_e_ref, tile_ok_ref, xs_ref, wg_ref, wu_ref, wd_ref, o_ref, ring_ref, sems, *, tmx, nt):
    j = pl.program_id(0)
    ok = tile_ok_ref[j] != 0
    tile_rows = tmx * ROW_SLAB

    def fetch(tile, slot):
        return pltpu.make_async_copy(xs_ref.at[pl.ds(pl.multiple_of(tile * tile_rows, tile_rows), tile_rows)],
                                     ring_ref.at[slot], sems.at[slot])

    @pl.when(j == 0)
    def _prologue():
        fetch(0, 0).start()

        @pl.when(tile_ok_ref[min(1, nt - 1)] != 0)
        def _():
            fetch(1, 1 % N_RING).start()

    ahead = j + (N_RING - 1)

    @pl.when(jnp.logical_and(ahead < nt, tile_ok_ref[jnp.minimum(ahead, nt - 1)] != 0))
    def _prefetch():
        fetch(ahead, lax.rem(ahead, N_RING)).start()

    @pl.when(ok)
    def _compute():
        slot = lax.rem(j, N_RING)
        fetch(j, slot).wait()
        x = _from_slabs(ring_ref.at[slot], tmx).astype(BF16)
        a = jnp.dot(x, wg_ref[0].astype(BF16), preferred_element_type=F32)
        b = jnp.dot(x, wu_ref[0].astype(BF16), preferred_element_type=F32)
        act = (_silu(a) * b).astype(BF16)
        _to_slabs(o_ref, jnp.dot(act, wd_ref[0].astype(BF16), preferred_element_type=F32), tmx)

    @pl.when(jnp.logical_not(ok))
    def _unused_tile():
        o_ref[...] = jnp.zeros_like(o_ref)


def _experts(xs, tile_e, tile_ok, wg, wu, wd, tmx):
    nt = tile_e.shape[0]
    wsel = lambda j, te, ok: (te[j], 0, 0)
    own = lambda j, te, ok: (j, 0)
    return pl.pallas_call(
        functools.partial(_experts_kernel, tmx=tmx, nt=nt),
        out_shape=jax.ShapeDtypeStruct(xs.shape, F32),
        grid_spec=pltpu.PrefetchScalarGridSpec(
            num_scalar_prefetch=2,
            grid=(nt,),
            in_specs=[pl.BlockSpec(memory_space=pl.ANY),
                      pl.BlockSpec((1, D_MODEL, D_EXPERT), wsel),
                      pl.BlockSpec((1, D_MODEL, D_EXPERT), wsel),
                      pl.BlockSpec((1, D_EXPERT, D_MODEL), wsel)],
            out_specs=pl.BlockSpec((tmx * ROW_SLAB, 128), own),
            scratch_shapes=[pltpu.VMEM((N_RING, tmx * ROW_SLAB, 128), F32), pltpu.SemaphoreType.DMA((N_RING,))]),
        compiler_params=pltpu.CompilerParams(dimension_semantics=("arbitrary",), vmem_limit_bytes=VMEM_LIMIT),
        name="moe_experts",
    )(tile_e, tile_ok, xs, wg, wu, wd)


def _combine_kernel(h_ref, gw_ref, pos_ref, npos_ref, g2_ref, b2_ref, os_ref, y_ref, stage_ref, sems,
                    *, td, n_steps):
    s = pl.program_id(0)

    def gather(idx_ref, half, start):
        for t in range(td):
            for k in range(2):
                cp = pltpu.make_async_copy(_slab(os_ref, idx_ref[0, 0, half * 2 * td + 2 * t + k]),
                                           _slab(stage_ref.at[half, k], t), sems.at[half])
                if start:
                    cp.start(priority=k)
                else:
                    cp.wait()

    def finish(half):
        rows = slice(half * td, (half + 1) * td)
        gw = gw_ref[rows, :]
        moe = (gw[:, 0:1] * _from_slabs(stage_ref.at[half, 0], td)
               + gw[:, 1:2] * _from_slabs(stage_ref.at[half, 1], td))
        h = jnp.concatenate([h_ref[pl.ds(half * td * ROW_SLAB + c, td, stride=ROW_SLAB), :]
                             for c in range(ROW_SLAB)], axis=1)
        y_ref[rows, :] = _layer_norm(ALPHA * h + moe, g2_ref[...], b2_ref[...])

    @pl.when(s == 0)
    def _prologue():
        gather(pos_ref, 0, True)

    gather(pos_ref, 1, True)
    gather(pos_ref, 0, False)
    finish(0)

    @pl.when(s + 1 < n_steps)
    def _next_step_first_tile():
        gather(npos_ref, 0, True)

    gather(pos_ref, 1, False)
    finish(1)


def _combine(h, gw, pos, g2, b2, os, td, first_token, n_tokens):
    step = 2 * td
    n_steps = n_tokens // step
    off = first_token // step
    pos4 = pos.reshape(pos.shape[0] // step, 1, 2 * step)
    row = lambda i: (i + off, 0)
    const = lambda i: (0, 0)
    return pl.pallas_call(
        functools.partial(_combine_kernel, td=td, n_steps=n_steps),
        out_shape=jax.ShapeDtypeStruct((n_tokens, D_MODEL), F32),
        grid=(n_steps,),
        in_specs=[pl.BlockSpec((step * ROW_SLAB, 128), row), pl.BlockSpec((step, 128), row),
                  pl.BlockSpec((1, 1, 2 * step), lambda i: (i + off, 0, 0), memory_space=pltpu.SMEM),
                  pl.BlockSpec((1, 1, 2 * step), lambda i: (jnp.minimum(i + 1, n_steps - 1) + off, 0, 0),
                               memory_space=pltpu.SMEM),
                  pl.BlockSpec((1, D_MODEL), const), pl.BlockSpec((1, D_MODEL), const),
                  pl.BlockSpec(memory_space=pl.ANY)],
        out_specs=pl.BlockSpec((step, D_MODEL), lambda i: (i, 0)),
        scratch_shapes=[pltpu.VMEM((2, 2, td * ROW_SLAB, 128), F32), pltpu.SemaphoreType.DMA((2,))],
        compiler_params=pltpu.CompilerParams(dimension_semantics=("arbitrary",), vmem_limit_bytes=VMEM_LIMIT),
        name="moe_combine",
    )(h, gw, pos4, pos4, g2, b2, os)


def _route_plan(sel, cnt, t, tmx, td):
    i32 = jnp.int32
    counts = cnt[0, :N_EXPERTS]
    padded = ((counts + tmx - 1) // tmx) * tmx
    ex = jnp.arange(N_EXPERTS, dtype=i32)
    ends = jnp.sum(jnp.where(ex[None, :] <= ex[:, None], padded[None, :], 0), axis=1).astype(i32)
    offs = ends - padded
    pos = (jnp.sum(jnp.where(sel[:, 0:2, None] == ex, offs, 0), axis=-1) + sel[:, 2:4]).astype(i32)
    nt = 2 * t // tmx + N_EXPERTS
    n_used = ends[-1] // tmx
    tile = jnp.arange(nt, dtype=i32)
    tile_idx = jnp.minimum(tile, jnp.maximum(n_used - 1, 0))
    tile_e = jnp.minimum(jnp.sum((ends[None, :] <= (tile_idx * tmx)[:, None]).astype(i32), axis=1), N_EXPERTS - 1)
    tile_ok = (tile < n_used).astype(i32)
    pieces = tmx // min(td, tmx)
    tail = jnp.stack([n_used * pieces, (nt - n_used) * pieces]).astype(i32)
    return (pos, tile_e, tile_ok, (offs + counts).astype(i32), (padded - counts).astype(i32), tail, nt * tmx)


def _tile(t, want):
    tm = min(want, t)
    while t % tm:
        tm //= 2
    return tm


def _prep_weights(w_in, conv_w, a_log, dt_bias, gdn_norm_w, pool_w, pool_scale, w_out, ln1_g, ln1_b,
                  w_rg, b_rg, w_re, b_re, w_gate, w_up, w_down, ln2_g, ln2_b):
    col_b = 4 * D_A
    col_p = 4 * D_A + 2 * N_HEADS
    w_cat = jnp.concatenate([w_in[:, :col_b], w_in[:, col_p:], w_in[:, col_b:col_p],
                             jnp.zeros((D_MODEL, 128 - 2 * N_HEADS), w_in.dtype)], axis=1).astype(BF16)
    lane_pad = lambda v, off: jnp.zeros((1, 128), F32).at[0, off:off + v.shape[0]].set(v.astype(F32))
    w_r = jnp.concatenate([w_re, w_rg, jnp.zeros((D_MODEL, 128 - N_EXPERTS - N_GROUPS), F32)], axis=1)
    wrh = w_r.astype(BF16)
    wrl = (w_r - wrh.astype(F32)).astype(BF16)
    b_r = jnp.zeros((1, 128), F32).at[0, :N_EXPERTS].set(b_re).at[0, N_EXPERTS:N_EXPERTS + N_GROUPS].set(b_rg)
    return dict(
        w_cat=w_cat, conv_w=conv_w, arow=lane_pad(a_log, LANE_A), dtrow=lane_pad(dt_bias, LANE_A),
        normw=gdn_norm_w.reshape(1, HEAD_DIM), poolw=pool_w.astype(BF16), pscale=pool_scale.reshape(1, D_B),
        wout=w_out.astype(BF16), g1=ln1_g.reshape(1, D_MODEL), b1=ln1_b.reshape(1, D_MODEL),
        wrh=wrh, wrl=wrl, br=b_r,
        wg=w_gate.reshape(N_EXPERTS, D_MODEL, D_EXPERT), wu=w_up.reshape(N_EXPERTS, D_MODEL, D_EXPERT),
        wd=w_down.reshape(N_EXPERTS, D_EXPERT, D_MODEL),
        g2=ln2_g.reshape(1, D_MODEL), b2=ln2_b.reshape(1, D_MODEL))


def _post_mixer(mix_p, x_p, mix_s, x_s, p):
    tp, ts = x_p.shape[0], x_s.shape[0]
    t = tp + ts
    tm = math.gcd(_tile(tp, 512), _tile(ts, 512))
    h, sel, gw, cnt = _outproj_router(mix_p, x_p, mix_s, x_s, p["wout"], p["g1"], p["b1"], p["wrh"], p["wrl"],
                                      p["br"], tm)
    td = math.gcd(_tile(tp, ROUTE_TILE), _tile(ts, ROUTE_TILE))
    pos, tile_e, tile_ok, pad_start, pad_cnt, tail, n_rows = _route_plan(sel, cnt, t, EXPERT_TILE, td)
    xs = _dispatch(h, pos.reshape(t // td, 1, 2 * td), pad_start, pad_cnt, tail, n_rows, td, EXPERT_TILE)
    os = _experts(xs, tile_e, tile_ok, p["wg"], p["wu"], p["wd"], EXPERT_TILE)
    tc = td if (tp // td) % 2 == 0 and (ts // td) % 2 == 0 else td // 2
    y_p = _combine(h, gw, pos, p["g2"], p["b2"], os, tc, 0, tp)
    y_s = _combine(h, gw, pos, p["g2"], p["b2"], os, tc, tp, ts)
    return y_p, y_s


def _mix_prompt(x, p, lb=256):
    b, seq, _ = x.shape
    x2d = x.reshape(b * seq, D_MODEL)
    proj = _in_proj(x2d, p["w_cat"], _tile(b * seq, 1024)).reshape(b, seq, C_TOT)
    mix, s_fin = _mixer_prompt(proj, p["conv_w"], p["arow"], p["dtrow"], p["normw"], p["poolw"], p["pscale"],
                               min(lb, seq))
    conv_new = proj[:, seq - (CONV_W - 1):, 0:C_QKV]
    pool_new = proj[:, seq - POOL_BUF:, C_P:C_P + D_B]
    return x2d, mix.reshape(b * seq, D_MODEL), s_fin, conv_new, pool_new


def _mix_sample(x, s0, conv0, pool0, start, p, ns=16):
    b, seq, _ = x.shape
    x2d = x.reshape(b * seq, D_MODEL)
    proj = _in_proj(x2d, p["w_cat"], _tile(b * seq, 1024))
    cst = jnp.pad(conv0, ((0, 0), (seq - (CONV_W - 1), 0), (0, 0))).reshape(b * seq, C_QKV)
    pst = jnp.pad(pool0, ((0, 0), (1, 0), (0, 0))).reshape(b * 16, D_B)
    mix, s_new = _mixer_sample(proj, cst, pst, s0, p["conv_w"], p["arow"], p["dtrow"], p["normw"], p["poolw"],
                               p["pscale"], min(ns, b), seq, start)
    proj3 = proj.reshape(b, seq, C_TOT)
    conv_new = proj3[:, seq - (CONV_W - 1):, 0:C_QKV]
    pool_new = jnp.concatenate([pool0[:, seq:, :], proj3[:, :, C_P:C_P + D_B]], axis=1)
    return x2d, mix, s_new, conv_new, pool_new


def _layer(x_prompt, x_sample, s0, conv0, pool0, start, p):
    xp2d, mix_p, dp, cp, pp = _mix_prompt(x_prompt, p)
    xs2d, mix_s, ds, cs, ps = _mix_sample(x_sample, s0, conv0, pool0, start, p)
    y_p, y_s = _post_mixer(mix_p, xp2d, mix_s, xs2d, p)
    return y_p.reshape(x_prompt.shape), y_s.reshape(x_sample.shape), (dp, cp, pp), (ds, cs, ps)


def kernel(x_prompt, x_sample, state_delta, state_conv, state_pool, w_in, conv_w, a_log, dt_bias, gdn_norm_w,
           pool_w, pool_scale, w_out, ln1_g, ln1_b, w_rg, b_rg, w_re, b_re, w_gate, w_up, w_down, ln2_g, ln2_b):
    depth = w_in.shape[0]
    past_len = 16384
    yp, ys = x_prompt, x_sample
    outs = [[] for _ in range(6)]
    for l in range(depth):
        p = _prep_weights(w_in[l], conv_w[l], a_log[l], dt_bias[l], gdn_norm_w[l], pool_w[l], pool_scale[l],
                          w_out[l], ln1_g[l], ln1_b[l], w_rg[l], b_rg[l], w_re[l], b_re[l], w_gate[l], w_up[l],
                          w_down[l], ln2_g[l], ln2_b[l])
        yp, ys, st_p, st_s = _layer(yp, ys, state_delta[l], state_conv[l], state_pool[l], past_len, p)
        for lst, v in zip(outs, st_p + st_s):
            lst.append(v)
    return (yp, ys) + tuple(jnp.stack(v) for v in outs)
```

```python
import functools
import itertools
import math

import jax
import jax.numpy as jnp
from jax import lax
from jax.experimental import pallas as pl
from jax.experimental.pallas import tpu as pltpu

F32 = jnp.float32
BF16 = jnp.bfloat16

D_MODEL = 1024
D_A = 512
D_B = 512
HEAD_DIM = 128
N_HEADS = 4
CONV_W = 4
CHUNK_SHIFT = 6
GDN_BLOCK = 128
POOL_WINDOWS = (2, 4, 8, 16)
POOL_BUF = 15
N_GROUPS = 4
E_PER_GROUP = 8
N_EXPERTS = N_GROUPS * E_PER_GROUP
D_EXPERT = 256
ALPHA = 2.0 ** 0.25
LN_EPS = 1e-5
RMS_EPS = 1e-6
L2_EPS = 1e-6

C_QKV = 3 * D_A
C_Z = 3 * D_A
C_P = 4 * D_A
C_BA = 4 * D_A + D_B
C_TOT = C_BA + 128
LANE_B = 0
LANE_A = N_HEADS

VMEM_LIMIT = 56 * 1024 * 1024


def _dot(a, b):
    return jnp.dot(a.astype(BF16), b.astype(BF16), preferred_element_type=F32)


def _dot_nt(a, b):
    return lax.dot_general(a.astype(BF16), b.astype(BF16), (((1,), (1,)), ((), ())), preferred_element_type=F32)


def _split3(x):
    hi = x.astype(BF16)
    r = x - hi.astype(F32)
    mid = r.astype(BF16)
    lo = (r - mid.astype(F32)).astype(BF16)
    return hi, mid, lo


def _dot01(m01, x):
    hi, mid, lo = _split3(x)
    f = lambda p: jnp.dot(m01, p, preferred_element_type=F32)
    return f(hi) + f(mid) + f(lo)


def _silu(x):
    return x * jax.nn.sigmoid(x)


def _softplus(x):
    return jnp.maximum(x, 0.0) + jnp.log1p(jnp.exp(-jnp.abs(x)))


def _iota2(n, m):
    return lax.broadcasted_iota(jnp.int32, (n, m), 0), lax.broadcasted_iota(jnp.int32, (n, m), 1)


def _proj_kernel(x_ref, w_ref, o_ref):
    o_ref[...] = jnp.dot(x_ref[...].astype(BF16), w_ref[...], preferred_element_type=F32)


def _in_proj(x2d, w_cat, tm):
    t = x2d.shape[0]
    return pl.pallas_call(
        _proj_kernel,
        out_shape=jax.ShapeDtypeStruct((t, C_TOT), F32),
        grid=(t // tm,),
        in_specs=[pl.BlockSpec((tm, D_MODEL), lambda i: (i, 0)),
                  pl.BlockSpec((D_MODEL, C_TOT), lambda i: (0, 0))],
        out_specs=pl.BlockSpec((tm, C_TOT), lambda i: (i, 0)),
        compiler_params=pltpu.CompilerParams(dimension_semantics=("parallel",), vmem_limit_bytes=VMEM_LIMIT),
        name="in_proj",
    )(x2d, w_cat)


def _unit_lower_inverse(a_list, r, c, chunk_shift):
    b0 = min(4, chunk_shift)
    eye = jnp.where(r == c, 1.0, 0.0).astype(F32)
    blk = (r >> b0) == (c >> b0)
    xs = [jnp.where(blk, a, 0.0) for a in a_list]
    ts = [eye - x for x in xs]
    for _ in range(b0 - 1):
        xs = [_dot(x, x) for x in xs]
        ts = [t + _dot(t, x) for t, x in zip(ts, xs)]
    for lvl in range(b0, chunk_shift):
        m = ((r >> (lvl + 1)) == (c >> (lvl + 1))) & ((r >> lvl) != (c >> lvl))
        tmp = [_dot(t, jnp.where(m, a, 0.0)) for t, a in zip(ts, a_list)]
        ts = [t - _dot(x, t) for t, x in zip(ts, tmp)]
    return ts


def _gate_slabs(ba, arow, dtrow, chunk_shift):
    n = ba.shape[0]
    beta = jax.nn.sigmoid(ba)
    g = -jnp.exp(arow) * _softplus(ba + dtrow)
    r, c = _iota2(n, n)
    same = (r >> chunk_shift) == (c >> chunk_shift)
    ltri = jnp.where(same & (r >= c), 1.0, 0.0).astype(BF16)
    lall = jnp.where(same, 1.0, 0.0).astype(BF16)
    cs = _dot01(jnp.concatenate([ltri, lall], axis=0), g)
    return beta, cs[:n], cs[n:]


def _heads_prepare(ys, betas, gcs, egcs, chunk_shift):
    n = ys[0].shape[0]
    nb = n // GDN_BLOCK
    r, c = _iota2(GDN_BLOCK, GDN_BLOCK)
    same = (r >> chunk_shift) == (c >> chunk_shift)
    incl = same & (r >= c)
    strict = same & (r > c)
    qs, ks, a_list, rhs, decays = [], [], [], [], []
    gc_ts = [g.T for g in gcs]
    for (y, beta_s, gc_s, egc_s, gc_t), h in itertools.product(zip(ys, betas, gcs, egcs, gc_ts), range(N_HEADS)):
        q = _l2norm(y[:, h * HEAD_DIM:(h + 1) * HEAD_DIM]) * (HEAD_DIM ** -0.5)
        k = _l2norm(y[:, D_A + h * HEAD_DIM:D_A + (h + 1) * HEAD_DIM])
        v = y[:, 2 * D_A + h * HEAD_DIM:2 * D_A + (h + 1) * HEAD_DIM]
        la = LANE_A + h
        beta_c = beta_s[:, LANE_B + h:LANE_B + h + 1]
        kb = k * beta_c
        rhs_h = jnp.concatenate([v * beta_c, kb * egc_s[:, la:la + 1]], axis=1)
        for bi in range(nb):
            blk = slice(bi * GDN_BLOCK, (bi + 1) * GDN_BLOCK)
            decay = jnp.exp(jnp.where(incl, gc_s[blk, la:la + 1] - gc_t[la:la + 1, blk], -jnp.inf))
            a_list.append(jnp.where(strict, _dot_nt(kb[blk], k[blk]) * decay, 0.0))
            rhs.append(rhs_h[blk])
            decays.append(decay)
        qs.append(q)
        ks.append(k)
    ts = _unit_lower_inverse(a_list, r, c, chunk_shift)
    sols = [_dot(t, x) for t, x in zip(ts, rhs)]
    us, ws, qkds = [], [], []
    for h in range(len(qs)):
        sol = jnp.concatenate(sols[h * nb:(h + 1) * nb], axis=0) if nb > 1 else sols[h]
        us.append(sol[:, :HEAD_DIM])
        ws.append(sol[:, HEAD_DIM:])
        qkds.append([_dot_nt(qs[h][bi * GDN_BLOCK:(bi + 1) * GDN_BLOCK], ks[h][bi * GDN_BLOCK:(bi + 1) * GDN_BLOCK])
                     * decays[h * nb + bi] for bi in range(nb)])
    return qs, ks, us, ws, qkds


def _l2norm(x):
    return x * lax.rsqrt(jnp.sum(x * x, axis=-1, keepdims=True) + L2_EPS)


def _gated_rmsnorm(o, z, normw):
    o = o * lax.rsqrt(jnp.mean(o * o, axis=-1, keepdims=True) + RMS_EPS) * normw
    return o * _silu(z)


def _pool_out(s, cnt, p_g, poolw_g, pscale_g):
    d = s / cnt - p_g
    return _dot(d, poolw_g) * pscale_g


def _mixer_prompt_kernel(proj_ref, convw_ref, arow_ref, dtrow_ref, normw_ref, poolw_ref, pscale_ref,
                         mix_ref, sfin_ref, cc_ref, pc_ref, s_ref, *, lb, nseq):
    l = pl.program_id(1)
    n = lb
    csz = 1 << CHUNK_SHIFT

    @pl.when(l == 0)
    def _init():
        cc_ref[...] = jnp.zeros_like(cc_ref)
        pc_ref[...] = jnp.zeros_like(pc_ref)
        s_ref[...] = jnp.zeros_like(s_ref)

    cw = convw_ref[...]
    ys, betas, gcs, egcs, ekgs, egls = [], [], [], [], [], []
    for si in range(nseq):
        u = proj_ref[si, :, 0:C_QKV]
        ext = jnp.concatenate([cc_ref[si], u], axis=0)
        acc = ext * cw[CONV_W - 1:CONV_W, :]
        for d in range(1, CONV_W):
            acc = acc + pltpu.roll(ext, d, 0) * cw[CONV_W - 1 - d:CONV_W - d, :]
        cc_ref[si] = u[n - 8:n, :]
        ys.append(_silu(acc[8:, :]))
        beta_s, gc_s, gl_s = _gate_slabs(proj_ref[si, :, C_BA:C_TOT], arow_ref[...], dtrow_ref[...], CHUNK_SHIFT)
        betas.append(beta_s)
        gcs.append(gc_s)
        egcs.append(jnp.exp(gc_s))
        ekgs.append(jnp.exp(gl_s - gc_s))
        egls.append(jnp.exp(gl_s))

    qs, ks, us, ws, qkds = _heads_prepare(ys, betas, gcs, egcs, CHUNK_SHIFT)
    n_heads = nseq * N_HEADS

    zero = jnp.zeros((csz, 2 * HEAD_DIM), F32)
    n_chunks = n // csz
    qps, ops, kns, egl_reps = [], [], [], []
    for hh in range(n_heads):
        si, h = divmod(hh, N_HEADS)
        la = LANE_A + h
        wu = jnp.concatenate([ws[hh], us[hh]], axis=1)
        qw = jnp.concatenate([_dot(qkd, wu[bi * GDN_BLOCK:(bi + 1) * GDN_BLOCK])
                              for bi, qkd in enumerate(qkds[hh])], axis=0)
        qps.append(qs[hh] * egcs[si][:, la:la + 1] - qw[:, :HEAD_DIM])
        ops.append(qw[:, HEAD_DIM:])
        kg_t = (ks[hh] * ekgs[si][:, la:la + 1]).T
        kn = []
        for ci in range(n_chunks):
            rows = slice(ci * csz, (ci + 1) * csz)
            pair = slice((ci // 2) * 2 * csz, (ci // 2 + 1) * 2 * csz)
            half = jnp.concatenate([wu[rows], zero] if ci % 2 == 0 else [zero, wu[rows]], axis=0)
            kn.append(_dot(kg_t[:, pair], half))
        kns.append(kn)
        egl_reps.append(jnp.broadcast_to(egls[si][:, la:la + 1], (n, HEAD_DIM)))

    states = [s_ref[hh // N_HEADS, hh % N_HEADS] for hh in range(n_heads)]
    outs = [[] for _ in range(n_heads)]
    for ci in range(n_chunks):
        rows = slice(ci * csz, (ci + 1) * csz)
        for hh in range(n_heads):
            s = states[hh]
            outs[hh].append(_dot(qps[hh][rows], s) + ops[hh][rows])
            kn = kns[hh][ci]
            states[hh] = (s * egl_reps[hh][ci * csz:ci * csz + 1, :] - _dot(kn[:, :HEAD_DIM], s)) + kn[:, HEAD_DIM:]
    for hh in range(n_heads):
        si, h = divmod(hh, N_HEADS)
        hs = slice(h * HEAD_DIM, (h + 1) * HEAD_DIM)
        s_ref[si, h] = states[hh]
        o = jnp.concatenate(outs[hh], axis=0)
        mix_ref[si, :, hs] = _gated_rmsnorm(o, proj_ref[si, :, C_Z + h * HEAD_DIM:C_Z + (h + 1) * HEAD_DIM],
                                            normw_ref[...])

    sfin_ref[...] = s_ref[...]

    r, c = _iota2(n, n + 16)
    lag = r + 16 - c
    pos = l * n + lax.broadcasted_iota(jnp.int32, (n, 1), 0)
    for si in range(nseq):
        p = proj_ref[si, :, C_P:C_P + D_B]
        extp = jnp.concatenate([pc_ref[si], p], axis=0)
        pc_ref[si] = p[n - 16:n, :]
        for gi, w in enumerate(POOL_WINDOWS):
            gs = slice(gi * HEAD_DIM, (gi + 1) * HEAD_DIM)
            band = jnp.where((lag >= 0) & (lag < w), 1.0, 0.0).astype(BF16)
            cnt = jnp.minimum(pos + 1, w).astype(F32)
            mix_ref[si, :, D_A + gi * HEAD_DIM:D_A + (gi + 1) * HEAD_DIM] = _pool_out(
                _dot01(band, extp[:, gs]), cnt, p[:, gs], poolw_ref[gi], pscale_ref[:, gs])


def _mixer_prompt(proj, conv_w, arow, dtrow, normw, poolw, pscale, lb, nseq):
    b, seq, _ = proj.shape
    const2 = lambda i, j: (0, 0)
    return pl.pallas_call(
        functools.partial(_mixer_prompt_kernel, lb=lb, nseq=nseq),
        out_shape=(jax.ShapeDtypeStruct((b, seq, D_MODEL), F32),
                   jax.ShapeDtypeStruct((b, N_HEADS, HEAD_DIM, HEAD_DIM), F32)),
        grid=(b // nseq, seq // lb),
        in_specs=[pl.BlockSpec((nseq, lb, C_TOT), lambda i, j: (i, j, 0)),
                  pl.BlockSpec((CONV_W, C_QKV), const2),
                  pl.BlockSpec((1, 128), const2),
                  pl.BlockSpec((1, 128), const2),
                  pl.BlockSpec((1, HEAD_DIM), const2),
                  pl.BlockSpec((N_GROUPS, HEAD_DIM, HEAD_DIM), lambda i, j: (0, 0, 0)),
                  pl.BlockSpec((1, D_B), const2)],
        out_specs=(pl.BlockSpec((nseq, lb, D_MODEL), lambda i, j: (i, j, 0)),
                   pl.BlockSpec((nseq, N_HEADS, HEAD_DIM, HEAD_DIM), lambda i, j: (i, 0, 0, 0))),
        scratch_shapes=[pltpu.VMEM((nseq, 8, C_QKV), F32), pltpu.VMEM((nseq, 16, D_B), F32),
                        pltpu.VMEM((nseq, N_HEADS, HEAD_DIM, HEAD_DIM), F32)],
        compiler_params=pltpu.CompilerParams(dimension_semantics=("parallel", "arbitrary"),
                                             vmem_limit_bytes=VMEM_LIMIT),
        name="mixer_prompt",
    )(proj, conv_w, arow, dtrow, normw, poolw, pscale)


def _mixer_sample_kernel(proj_ref, cst_ref, pst_ref, sin_ref, convw_ref, arow_ref, dtrow_ref, normw_ref,
                         poolw_ref, pscale_ref, mix_ref, sout_ref, *, ns, seq, start):
    n = ns * seq
    sshift = seq.bit_length() - 1
    rowi = lax.broadcasted_iota(jnp.int32, (n, 1), 0)
    tpos = rowi & (seq - 1)

    u = proj_ref[:, 0:C_QKV]
    st = cst_ref[...]
    cw = convw_ref[...]
    acc = u * cw[CONV_W - 1:CONV_W, :]
    for d in range(1, CONV_W):
        term = jnp.where(tpos >= d, pltpu.roll(u, d, 0), pltpu.roll(st, n - seq + d, 0))
        acc = acc + term * cw[CONV_W - 1 - d:CONV_W - d, :]
    y = _silu(acc)

    beta_s, gc_s, gl_s = _gate_slabs(proj_ref[:, C_BA:C_TOT], arow_ref[...], dtrow_ref[...], sshift)
    egc_s = jnp.exp(gc_s)
    ekg_s = jnp.exp(gl_s - gc_s)
    egl_s = jnp.exp(gl_s)
    qs, ks, us, ws_, qkds = _heads_prepare([y], [beta_s], [gc_s], [egc_s], sshift)

    for h in range(N_HEADS):
        hs = slice(h * HEAD_DIM, (h + 1) * HEAD_DIM)
        la = LANE_A + h
        u_, w_, qkd = us[h], ws_[h], qkds[h][0]
        qg = qs[h] * egc_s[:, la:la + 1]
        kg_t = (ks[h] * ekg_s[:, la:la + 1]).T
        egl_rep = jnp.broadcast_to(egl_s[:, la:la + 1], (n, HEAD_DIM))
        ws_w, ws_q = [], []
        for si in range(ns):
            rows = slice(si * seq, (si + 1) * seq)
            ws = _dot(jnp.concatenate([w_[rows], qg[rows]], axis=0), sin_ref[si, h])
            ws_w.append(ws[:seq])
            ws_q.append(ws[seq:])
        vn = u_ - jnp.concatenate(ws_w, axis=0)
        o = jnp.concatenate(ws_q, axis=0) + _dot(qkd, vn)
        for si in range(ns):
            vmask = jnp.where((rowi >> sshift) == si, vn, 0.0)
            sout_ref[si, h] = sin_ref[si, h] * egl_rep[si * seq:si * seq + 1, :] + _dot(kg_t, vmask)
        mix_ref[:, hs] = _gated_rmsnorm(o, proj_ref[:, C_Z + h * HEAD_DIM:C_Z + (h + 1) * HEAD_DIM], normw_ref[...])

    p = proj_ref[:, C_P:C_P + D_B]
    pst = pst_ref[...]
    r, c = _iota2(n, n)
    band_new_base = ((r >> sshift) == (c >> sshift)) & (r >= c)
    r2, c2 = _iota2(n, ns * 16)
    same2 = (r2 >> sshift) == (c2 >> 4)
    t2 = r2 & (seq - 1)
    j2 = c2 & 15
    pos = start + tpos
    for gi, w in enumerate(POOL_WINDOWS):
        gs = slice(gi * HEAD_DIM, (gi + 1) * HEAD_DIM)
        band_new = jnp.where(band_new_base & ((r - c) < w), 1.0, 0.0).astype(BF16)
        band_st = jnp.where(same2 & (j2 >= 17 + t2 - w), 1.0, 0.0).astype(BF16)
        s = _dot01(band_new, p[:, gs]) + _dot01(band_st, pst[:, gs])
        cnt = jnp.minimum(pos + 1, w).astype(F32)
        mix_ref[:, D_A + gi * HEAD_DIM:D_A + (gi + 1) * HEAD_DIM] = _pool_out(
            s, cnt, p[:, gs], poolw_ref[gi], pscale_ref[:, gs])


def _mixer_sample(proj, cst, pst, sin, conv_w, arow, dtrow, normw, poolw, pscale, ns, seq, start):
    t = proj.shape[0]
    nb = t // seq
    n = ns * seq
    assert n == GDN_BLOCK and seq & (seq - 1) == 0 and seq >= CONV_W - 1
    const1 = lambda i: (0, 0)
    return pl.pallas_call(
        functools.partial(_mixer_sample_kernel, ns=ns, seq=seq, start=start),
        out_shape=(jax.ShapeDtypeStruct((t, D_MODEL), F32),
                   jax.ShapeDtypeStruct((nb, N_HEADS, HEAD_DIM, HEAD_DIM), F32)),
        grid=(nb // ns,),
        in_specs=[pl.BlockSpec((n, C_TOT), lambda i: (i, 0)),
                  pl.BlockSpec((n, C_QKV), lambda i: (i, 0)),
                  pl.BlockSpec((ns * 16, D_B), lambda i: (i, 0)),
                  pl.BlockSpec((ns, N_HEADS, HEAD_DIM, HEAD_DIM), lambda i: (i, 0, 0, 0)),
                  pl.BlockSpec((CONV_W, C_QKV), const1),
                  pl.BlockSpec((1, 128), const1),
                  pl.BlockSpec((1, 128), const1),
                  pl.BlockSpec((1, HEAD_DIM), const1),
                  pl.BlockSpec((N_GROUPS, HEAD_DIM, HEAD_DIM), lambda i: (0, 0, 0)),
                  pl.BlockSpec((1, D_B), const1)],
        out_specs=(pl.BlockSpec((n, D_MODEL), lambda i: (i, 0)),
                   pl.BlockSpec((ns, N_HEADS, HEAD_DIM, HEAD_DIM), lambda i: (i, 0, 0, 0))),
        compiler_params=pltpu.CompilerParams(dimension_semantics=("parallel",), vmem_limit_bytes=VMEM_LIMIT),
        name="mixer_sample",
    )(proj, cst, pst, sin, conv_w, arow, dtrow, normw, poolw, pscale)


ROW_SLAB = D_MODEL // 128


def _to_slabs(ref, x, n):
    for c in range(ROW_SLAB):
        ref[pl.ds(c, n, stride=ROW_SLAB), :] = x[:, c * 128:(c + 1) * 128]


def _from_slabs(ref, n):
    return jnp.concatenate([ref[pl.ds(c, n, stride=ROW_SLAB), :] for c in range(ROW_SLAB)], axis=1)


def _slab(ref, row):
    if isinstance(row, int):
        return ref.at[pl.ds(row * ROW_SLAB, ROW_SLAB)]
    return ref.at[pl.ds(pl.multiple_of(row * ROW_SLAB, ROW_SLAB), ROW_SLAB)]


def _layer_norm(x, g, b):
    mu = jnp.mean(x, axis=-1, keepdims=True)
    xc = x - mu
    var = jnp.mean(xc * xc, axis=-1, keepdims=True)
    return xc * lax.rsqrt(var + LN_EPS) * g + b


def _outproj_router_kernel(mixp_ref, xp_ref, mixs_ref, xs_ref, wout_ref, g1_ref, b1_ref, wrh_ref, wrl_ref, br_ref,
                           h_ref, sel_ref, gw_ref, cnt_ref, carry_ref, *, n_prompt_tiles):
    @pl.when(pl.program_id(0) == 0)
    def _init():
        carry_ref[...] = jnp.zeros_like(carry_ref)

    is_prompt = pl.program_id(0) < n_prompt_tiles
    tm = h_ref.shape[0] // ROW_SLAB
    n_parts = 2 if tm % 16 == 0 else 1
    pm = tm // n_parts
    parts = [slice(i * pm, (i + 1) * pm) for i in range(n_parts)]
    each = lambda fn, *lists: [fn(*args) for args in zip(*lists)]
    lane = lax.broadcasted_iota(jnp.int32, (pm, 128), 1)
    big = jnp.int32(1 << 20)
    neg = -jnp.inf
    lsum = lambda v: jnp.sum(v, axis=1, keepdims=True)
    lmax = lambda v: jnp.max(v, axis=1, keepdims=True)
    lmin = lambda v: jnp.min(v, axis=1, keepdims=True)

    mix = [jnp.where(is_prompt, mixp_ref[rs, :], mixs_ref[rs, :]) for rs in parts]
    x = [jnp.where(is_prompt, xp_ref[rs, :], xs_ref[rs, :]) for rs in parts]
    proj = each(lambda m: _dot(m, wout_ref[...]), mix)
    h = each(lambda xi, pi: _layer_norm(ALPHA * xi + pi, g1_ref[...], b1_ref[...]), x, proj)
    for i, hi in enumerate(h):
        for c in range(ROW_SLAB):
            h_ref[pl.ds(i * pm * ROW_SLAB + c, pm, stride=ROW_SLAB), :] = hi[:, c * 128:(c + 1) * 128]
    f = lambda a, b: jnp.dot(a, b, preferred_element_type=F32)
    split = each(_split3, h)
    logits = each(lambda s: f(s[0], wrh_ref[...]) + (f(s[1], wrh_ref[...]) + f(s[0], wrl_ref[...])) + br_ref[...],
                  split)
    gmask = (lane >= N_EXPERTS) & (lane < N_EXPERTS + N_GROUPS)
    lg = each(lambda l: jnp.where(gmask, l, neg), logits)
    gmax = each(lmax, lg)
    gidx = each(lambda l, m: lmin(jnp.where(l == m, lane - N_EXPERTS, big)), lg, gmax)
    pg = each(lambda l, m: 1.0 / lsum(jnp.where(gmask, jnp.exp(l - m), 0.0)), logits, gmax)
    emask = each(lambda g: (lane < N_EXPERTS) & ((lane >> 3) == g), gidx)
    le = each(lambda m, l: jnp.where(m, l, neg), emask, logits)
    v1 = each(lmax, le)
    i1 = each(lambda l, v, m: lmin(jnp.where((l == v) & m, lane, big)), le, v1, emask)
    emask2 = each(lambda m, i: m & (lane != i), emask, i1)
    le2 = each(lambda m, l: jnp.where(m, l, neg), emask2, logits)
    v2 = each(lmax, le2)
    i2 = each(lambda l, v, m: lmin(jnp.where((l == v) & m, lane, big)), le2, v2, emask2)
    e2 = each(lambda a, b: jnp.exp(a - b), v2, v1)
    for rs, e, g in zip(parts, e2, pg):
        den = 1.0 + e
        gw_ref[rs, :] = jnp.where(lane == 0, (1.0 / den) * g, jnp.where(lane == 1, (e / den) * g, 0.0))
    onehot = each(lambda a, b: jnp.where((lane == a) | (lane == b), 1.0, 0.0), i1, i2)
    r, c = _iota2(pm, pm)
    tri = jnp.where(r > c, 1.0, 0.0).astype(BF16)
    inside = each(lambda o: jnp.dot(tri, o.astype(BF16), preferred_element_type=F32), onehot)
    carry = carry_ref[...]
    for rs, o, ins, a, b in zip(parts, onehot, inside, i1, i2):
        before = ins + carry
        r1 = lsum(jnp.where(lane == a, before, 0.0)).astype(jnp.int32)
        r2 = lsum(jnp.where(lane == b, before, 0.0)).astype(jnp.int32)
        sel_ref[rs, :] = jnp.where(lane == 0, a, jnp.where(lane == 1, b, jnp.where(lane == 2, r1,
                                                                                    jnp.where(lane == 3, r2, 0))))
        carry = carry + jnp.sum(o, axis=0, keepdims=True)
    carry_ref[...] = carry
    cnt_ref[...] = carry.astype(jnp.int32)


def _outproj_router(mix_p, x_p, mix_s, x_s, wout, g1, b1, wrh, wrl, br, tm):
    tp, ts = x_p.shape[0], x_s.shape[0]
    t = tp + ts
    npt = tp // tm
    row = lambda i: (i, 0)
    prow = lambda i: (jnp.minimum(i, npt - 1), 0)
    srow = lambda i: (jnp.maximum(i - npt, 0), 0)
    const = lambda i: (0, 0)
    return pl.pallas_call(
        functools.partial(_outproj_router_kernel, n_prompt_tiles=npt),
        out_shape=(jax.ShapeDtypeStruct((t * ROW_SLAB, 128), F32), jax.ShapeDtypeStruct((t, 128), jnp.int32),
                   jax.ShapeDtypeStruct((t, 128), F32), jax.ShapeDtypeStruct((1, 128), jnp.int32)),
        grid=(t // tm,),
        in_specs=[pl.BlockSpec((tm, D_MODEL), prow), pl.BlockSpec((tm, D_MODEL), prow),
                  pl.BlockSpec((tm, D_MODEL), srow), pl.BlockSpec((tm, D_MODEL), srow),
                  pl.BlockSpec((D_MODEL, D_MODEL), const), pl.BlockSpec((1, D_MODEL), const),
                  pl.BlockSpec((1, D_MODEL), const), pl.BlockSpec((D_MODEL, 128), const),
                  pl.BlockSpec((D_MODEL, 128), const), pl.BlockSpec((1, 128), const)],
        out_specs=(pl.BlockSpec((tm * ROW_SLAB, 128), row), pl.BlockSpec((tm, 128), row),
                   pl.BlockSpec((tm, 128), row), pl.BlockSpec((1, 128), const)),
        scratch_shapes=[pltpu.VMEM((1, 128), F32)],
        compiler_params=pltpu.CompilerParams(dimension_semantics=("arbitrary",), vmem_limit_bytes=VMEM_LIMIT),
        name="outproj_router",
    )(mix_p, x_p, mix_s, x_s, wout, g1, b1, wrh, wrl, br)


EXPERT_TILE = 512
ROUTE_TILE = 256


def _dispatch_kernel(pad_start_ref, pad_cnt_ref, tail_ref, h_ref, pos_ref, xs_ref, zero_ref, sem, zsem, tsem,
                     *, td, tmx):
    @pl.when(pl.program_id(0) == 0)
    def _zero_unused_rows():
        zero_ref[...] = jnp.zeros_like(zero_ref)

        def zcopy(row, n_rows):
            return pltpu.make_async_copy(
                zero_ref.at[pl.ds(0, n_rows * ROW_SLAB)],
                xs_ref.at[pl.ds(pl.multiple_of(row * ROW_SLAB, ROW_SLAB), n_rows * ROW_SLAB)], zsem)

        tile_rows = min(td, tmx) * ROW_SLAB
        tcopy = lambda tile: pltpu.make_async_copy(
            zero_ref.at[pl.ds(0, tile_rows)],
            xs_ref.at[pl.ds(pl.multiple_of(tile * tile_rows, tile_rows), tile_rows)], tsem)

        def pad_pieces(e, start_not_wait):
            start = pad_start_ref[e]
            n = pad_cnt_ref[e]
            piece = tmx // 2
            while piece >= 1:
                @pl.when((n & piece) != 0)
                def _(piece=piece):
                    cp = zcopy(start + (n & ~(2 * piece - 1)), piece)
                    cp.start() if start_not_wait else cp.wait()
                piece //= 2

        def start_pads(e, carry):
            pad_pieces(e, True)
            return carry

        def wait_pads(e, carry):
            pad_pieces(e, False)
            return carry

        lax.fori_loop(0, N_EXPERTS, start_pads, 0)

        def tail_start(r, carry):
            tcopy(tail_ref[0] + r).start()
            return carry

        lax.fori_loop(0, tail_ref[1], tail_start, 0)
        lax.fori_loop(0, N_EXPERTS, wait_pads, 0)

        def tail_wait(r, carry):
            tcopy(0).wait()
            return carry

        lax.fori_loop(0, tail_ref[1], tail_wait, 0)

    copies = []
    for t in range(td):
        for k in range(2):
            cp = pltpu.make_async_copy(_slab(h_ref, t), _slab(xs_ref, pos_ref[0, 0, k * td + t]), sem)
            cp.start(priority=k)
            copies.append(cp)
    for cp in copies:
        cp.wait()


def _dispatch(h, pos3, pad_start, pad_cnt, tail, n_rows, td, tmx):
    t = h.shape[0] // ROW_SLAB
    assert td % min(td, tmx) == 0 and tmx % min(td, tmx) == 0 and tmx // 2 <= td
    return pl.pallas_call(
        functools.partial(_dispatch_kernel, td=td, tmx=tmx),
        out_shape=jax.ShapeDtypeStruct((n_rows * ROW_SLAB, 128), F32),
        grid_spec=pltpu.PrefetchScalarGridSpec(
            num_scalar_prefetch=3,
            grid=(t // td,),
            in_specs=[pl.BlockSpec((td * ROW_SLAB, 128), lambda i, *_: (i, 0)),
                      pl.BlockSpec((1, 1, 2 * td), lambda i, *_: (i, 0, 0), memory_space=pltpu.SMEM)],
            out_specs=pl.BlockSpec(memory_space=pl.ANY),
            scratch_shapes=[pltpu.VMEM((td * ROW_SLAB, 128), F32),
                            pltpu.SemaphoreType.DMA, pltpu.SemaphoreType.DMA, pltpu.SemaphoreType.DMA]),
        compiler_params=pltpu.CompilerParams(dimension_semantics=("arbitrary",), vmem_limit_bytes=VMEM_LIMIT),
        name="moe_dispatch",
    )(pad_start, pad_cnt, tail, h, pos3)


N_RING = 3


def _experts_kernel(tile_e_ref, tile_ok_ref, xs_ref, wg_ref, wu_ref, wd_ref, o_ref, ring_ref, sems, *, tmx, nt):
    j = pl.program_id(0)
    ok = tile_ok_ref[j] != 0
    tile_rows = tmx * ROW_SLAB

    def fetch(tile, slot):
        return pltpu.make_async_copy(xs_ref.at[pl.ds(pl.multiple_of(tile * tile_rows, tile_rows), tile_rows)],
                                     ring_ref.at[slot], sems.at[slot])

    @pl.when(j == 0)
    def _prologue():
        fetch(0, 0).start()

        @pl.when(tile_ok_ref[min(1, nt - 1)] != 0)
        def _():
            fetch(1, 1 % N_RING).start()

    ahead = j + (N_RING - 1)

    @pl.when(jnp.logical_and(ahead < nt, tile_ok_ref[jnp.minimum(ahead, nt - 1)] != 0))
    def _prefetch():
        fetch(ahead, lax.rem(ahead, N_RING)).start()

    @pl.when(ok)
    def _compute():
        slot = lax.rem(j, N_RING)
        fetch(j, slot).wait()
        x = _from_slabs(ring_ref.at[slot], tmx).astype(BF16)
        a = jnp.dot(x, wg_ref[0].astype(BF16), preferred_element_type=F32)
        b = jnp.dot(x, wu_ref[0].astype(BF16), preferred_element_type=F32)
        act = (_silu(a) * b).astype(BF16)
        _to_slabs(o_ref, jnp.dot(act, wd_ref[0].astype(BF16), preferred_element_type=F32), tmx)

    @pl.when(jnp.logical_not(ok))
    def _unused_tile():
        o_ref[...] = jnp.zeros_like(o_ref)


def _experts(xs, tile_e, tile_ok, wg, wu, wd, tmx):
    nt = tile_e.shape[0]
    wsel = lambda j, te, ok: (te[j], 0, 0)
    own = lambda j, te, ok: (j, 0)
    return pl.pallas_call(
        functools.partial(_experts_kernel, tmx=tmx, nt=nt),
        out_shape=jax.ShapeDtypeStruct(xs.shape, F32),
        grid_spec=pltpu.PrefetchScalarGridSpec(
            num_scalar_prefetch=2,
            grid=(nt,),
            in_specs=[pl.BlockSpec(memory_space=pl.ANY),
                      pl.BlockSpec((1, D_MODEL, D_EXPERT), wsel),
                      pl.BlockSpec((1, D_MODEL, D_EXPERT), wsel),
                      pl.BlockSpec((1, D_EXPERT, D_MODEL), wsel)],
            out_specs=pl.BlockSpec((tmx * ROW_SLAB, 128), own),
            scratch_shapes=[pltpu.VMEM((N_RING, tmx * ROW_SLAB, 128), F32), pltpu.SemaphoreType.DMA((N_RING,))]),
        compiler_params=pltpu.CompilerParams(dimension_semantics=("arbitrary",), vmem_limit_bytes=VMEM_LIMIT),
        name="moe_experts",
    )(tile_e, tile_ok, xs, wg, wu, wd)


def _combine_kernel(h_ref, gw_ref, pos_ref, npos_ref, g2_ref, b2_ref, os_ref, y_ref, stage_ref, sems,
                    *, td, n_steps):
    s = pl.program_id(0)

    def gather(idx_ref, half, start):
        for t in range(td):
            for k in range(2):
                cp = pltpu.make_async_copy(_slab(os_ref, idx_ref[0, 0, (half * 2 + k) * td + t]),
                                           _slab(stage_ref.at[half, k], t), sems.at[half])
                if start:
                    cp.start(priority=k)
                else:
                    cp.wait()

    def finish(half):
        rows = slice(half * td, (half + 1) * td)
        gw = gw_ref[rows, :]
        moe = (gw[:, 0:1] * _from_slabs(stage_ref.at[half, 0], td)
               + gw[:, 1:2] * _from_slabs(stage_ref.at[half, 1], td))
        h = jnp.concatenate([h_ref[pl.ds(half * td * ROW_SLAB + c, td, stride=ROW_SLAB), :]
                             for c in range(ROW_SLAB)], axis=1)
        y_ref[rows, :] = _layer_norm(ALPHA * h + moe, g2_ref[...], b2_ref[...])

    @pl.when(s == 0)
    def _prologue():
        gather(pos_ref, 0, True)

    gather(pos_ref, 1, True)
    gather(pos_ref, 0, False)
    finish(0)

    @pl.when(s + 1 < n_steps)
    def _next_step_first_tile():
        gather(npos_ref, 0, True)

    gather(pos_ref, 1, False)
    finish(1)


def _combine(h, gw, pos, g2, b2, os, td, first_token, n_tokens):
    step = 2 * td
    n_steps = n_tokens // step
    off = first_token // step
    pos4 = jnp.stack([q.reshape(-1, 2, td) for q in pos], axis=2).reshape(-1, 1, 2 * step)
    row = lambda i: (i + off, 0)
    const = lambda i: (0, 0)
    return pl.pallas_call(
        functools.partial(_combine_kernel, td=td, n_steps=n_steps),
        out_shape=jax.ShapeDtypeStruct((n_tokens, D_MODEL), F32),
        grid=(n_steps,),
        in_specs=[pl.BlockSpec((step * ROW_SLAB, 128), row), pl.BlockSpec((step, 128), row),
                  pl.BlockSpec((1, 1, 2 * step), lambda i: (i + off, 0, 0), memory_space=pltpu.SMEM),
                  pl.BlockSpec((1, 1, 2 * step), lambda i: (jnp.minimum(i + 1, n_steps - 1) + off, 0, 0),
                               memory_space=pltpu.SMEM),
                  pl.BlockSpec((1, D_MODEL), const), pl.BlockSpec((1, D_MODEL), const),
                  pl.BlockSpec(memory_space=pl.ANY)],
        out_specs=pl.BlockSpec((step, D_MODEL), lambda i: (i, 0)),
        scratch_shapes=[pltpu.VMEM((2, 2, td * ROW_SLAB, 128), F32), pltpu.SemaphoreType.DMA((2,))],
        compiler_params=pltpu.CompilerParams(dimension_semantics=("arbitrary",), vmem_limit_bytes=VMEM_LIMIT),
        name="moe_combine",
    )(h, gw, pos4, pos4, g2, b2, os)


def _route_plan(sel, cnt, t, tmx, td):
    i32 = jnp.int32
    counts = cnt[0, :N_EXPERTS]
    padded = ((counts + tmx - 1) // tmx) * tmx
    ex = jnp.arange(N_EXPERTS, dtype=i32)
    ends = jnp.sum(jnp.where(ex[None, :] <= ex[:, None], padded[None, :], 0), axis=1).astype(i32)
    offs = ends - padded
    pos = tuple((jnp.sum(jnp.where(sel[:, k, None] == ex, offs, 0), axis=-1) + sel[:, 2 + k]).astype(i32)
                for k in range(2))
    nt = 2 * t // tmx + N_EXPERTS
    n_used = ends[-1] // tmx
    tile = jnp.arange(nt, dtype=i32)
    tile_idx = jnp.minimum(tile, jnp.maximum(n_used - 1, 0))
    tile_e = jnp.minimum(jnp.sum((ends[None, :] <= (tile_idx * tmx)[:, None]).astype(i32), axis=1), N_EXPERTS - 1)
    tile_ok = (tile < n_used).astype(i32)
    pieces = tmx // min(td, tmx)
    tail = jnp.stack([n_used * pieces, (nt - n_used) * pieces]).astype(i32)
    return (pos, tile_e, tile_ok, (offs + counts).astype(i32), (padded - counts).astype(i32), tail, nt * tmx)


def _tile(t, want):
    tm = min(want, t)
    while t % tm:
        tm //= 2
    return tm


def _prep_weights(w_in, conv_w, a_log, dt_bias, gdn_norm_w, pool_w, pool_scale, w_out, ln1_g, ln1_b,
                  w_rg, b_rg, w_re, b_re, w_gate, w_up, w_down, ln2_g, ln2_b):
    col_b = 4 * D_A
    col_p = 4 * D_A + 2 * N_HEADS
    w_cat = jnp.concatenate([w_in[:, :col_b], w_in[:, col_p:], w_in[:, col_b:col_p],
                             jnp.zeros((D_MODEL, 128 - 2 * N_HEADS), w_in.dtype)], axis=1).astype(BF16)
    lane_pad = lambda v, off: jnp.zeros((1, 128), F32).at[0, off:off + v.shape[0]].set(v.astype(F32))
    w_r = jnp.concatenate([w_re, w_rg, jnp.zeros((D_MODEL, 128 - N_EXPERTS - N_GROUPS), F32)], axis=1)
    wrh = w_r.astype(BF16)
    wrl = (w_r - wrh.astype(F32)).astype(BF16)
    b_r = jnp.zeros((1, 128), F32).at[0, :N_EXPERTS].set(b_re).at[0, N_EXPERTS:N_EXPERTS + N_GROUPS].set(b_rg)
    return dict(
        w_cat=w_cat, conv_w=conv_w, arow=lane_pad(a_log, LANE_A), dtrow=lane_pad(dt_bias, LANE_A),
        normw=gdn_norm_w.reshape(1, HEAD_DIM), poolw=pool_w.astype(BF16), pscale=pool_scale.reshape(1, D_B),
        wout=w_out.astype(BF16), g1=ln1_g.reshape(1, D_MODEL), b1=ln1_b.reshape(1, D_MODEL),
        wrh=wrh, wrl=wrl, br=b_r,
        wg=w_gate.reshape(N_EXPERTS, D_MODEL, D_EXPERT), wu=w_up.reshape(N_EXPERTS, D_MODEL, D_EXPERT),
        wd=w_down.reshape(N_EXPERTS, D_EXPERT, D_MODEL),
        g2=ln2_g.reshape(1, D_MODEL), b2=ln2_b.reshape(1, D_MODEL))


def _post_mixer(mix_p, x_p, mix_s, x_s, p):
    tp, ts = x_p.shape[0], x_s.shape[0]
    t = tp + ts
    tm = math.gcd(_tile(tp, 512), _tile(ts, 512))
    h, sel, gw, cnt = _outproj_router(mix_p, x_p, mix_s, x_s, p["wout"], p["g1"], p["b1"], p["wrh"], p["wrl"],
                                      p["br"], tm)
    td = math.gcd(_tile(tp, ROUTE_TILE), _tile(ts, ROUTE_TILE))
    pos, tile_e, tile_ok, pad_start, pad_cnt, tail, n_rows = _route_plan(sel, cnt, t, EXPERT_TILE, td)
    pos3 = jnp.stack([q.reshape(t // td, td) for q in pos], axis=1).reshape(t // td, 1, 2 * td)
    xs = _dispatch(h, pos3, pad_start, pad_cnt, tail, n_rows, td, EXPERT_TILE)
    os = _experts(xs, tile_e, tile_ok, p["wg"], p["wu"], p["wd"], EXPERT_TILE)
    tc = td if (tp // td) % 2 == 0 and (ts // td) % 2 == 0 else td // 2
    y_p = _combine(h, gw, pos, p["g2"], p["b2"], os, tc, 0, tp)
    y_s = _combine(h, gw, pos, p["g2"], p["b2"], os, tc, tp, ts)
    return y_p, y_s


def _mix_prompt(x, p, lb=256):
    b, seq, _ = x.shape
    x2d = x.reshape(b * seq, D_MODEL)
    proj = _in_proj(x2d, p["w_cat"], _tile(b * seq, 1024)).reshape(b, seq, C_TOT)
    mix, s_fin = _mixer_prompt(proj, p["conv_w"], p["arow"], p["dtrow"], p["normw"], p["poolw"], p["pscale"],
                               min(lb, seq), 2 if b % 2 == 0 else 1)
    conv_new = proj[:, seq - (CONV_W - 1):, 0:C_QKV]
    pool_new = proj[:, seq - POOL_BUF:, C_P:C_P + D_B]
    return x2d, mix.reshape(b * seq, D_MODEL), s_fin, conv_new, pool_new


def _mix_sample(x, s0, conv0, pool0, start, p, ns=16):
    b, seq, _ = x.shape
    x2d = x.reshape(b * seq, D_MODEL)
    proj = _in_proj(x2d, p["w_cat"], _tile(b * seq, 1024))
    cst = jnp.pad(conv0, ((0, 0), (seq - (CONV_W - 1), 0), (0, 0))).reshape(b * seq, C_QKV)
    pst = jnp.pad(pool0, ((0, 0), (1, 0), (0, 0))).reshape(b * 16, D_B)
    mix, s_new = _mixer_sample(proj, cst, pst, s0, p["conv_w"], p["arow"], p["dtrow"], p["normw"], p["poolw"],
                               p["pscale"], min(ns, b), seq, start)
    proj3 = proj.reshape(b, seq, C_TOT)
    conv_new = proj3[:, seq - (CONV_W - 1):, 0:C_QKV]
    pool_new = jnp.concatenate([pool0[:, seq:, :], proj3[:, :, C_P:C_P + D_B]], axis=1)
    return x2d, mix, s_new, conv_new, pool_new


def _layer(x_prompt, x_sample, s0, conv0, pool0, start, p):
    xp2d, mix_p, dp, cp, pp = _mix_prompt(x_prompt, p)
    xs2d, mix_s, ds, cs, ps = _mix_sample(x_sample, s0, conv0, pool0, start, p)
    y_p, y_s = _post_mixer(mix_p, xp2d, mix_s, xs2d, p)
    return y_p.reshape(x_prompt.shape), y_s.reshape(x_sample.shape), (dp, cp, pp), (ds, cs, ps)


def kernel(x_prompt, x_sample, state_delta, state_conv, state_pool, w_in, conv_w, a_log, dt_bias, gdn_norm_w,
           pool_w, pool_scale, w_out, ln1_g, ln1_b, w_rg, b_rg, w_re, b_re, w_gate, w_up, w_down, ln2_g, ln2_b):
    depth = w_in.shape[0]
    past_len = 16384
    yp, ys = x_prompt, x_sample
    outs = [[] for _ in range(6)]
    for l in range(depth):
        p = _prep_weights(w_in[l], conv_w[l], a_log[l], dt_bias[l], gdn_norm_w[l], pool_w[l], pool_scale[l],
                          w_out[l], ln1_g[l], ln1_b[l], w_rg[l], b_rg[l], w_re[l], b_re[l], w_gate[l], w_up[l],
                          w_down[l], ln2_g[l], ln2_b[l])
        yp, ys, st_p, st_s = _layer(yp, ys, state_delta[l], state_conv[l], state_pool[l], past_len, p)
        for lst, v in zip(outs, st_p + st_s):
            lst.append(v)
    return (yp, ys) + tuple(jnp.stack(v) for v in outs)
```

```python
import functools
import itertools
import math

import jax
import jax.numpy as jnp
from jax import lax
from jax.experimental import pallas as pl
from jax.experimental.pallas import tpu as pltpu

F32 = jnp.float32
BF16 = jnp.bfloat16

D_MODEL = 1024
D_A = 512
D_B = 512
HEAD_DIM = 128
N_HEADS = 4
CONV_W = 4
CHUNK_SHIFT = 6
GDN_BLOCK = 128
POOL_WINDOWS = (2, 4, 8, 16)
POOL_BUF = 15
N_GROUPS = 4
E_PER_GROUP = 8
N_EXPERTS = N_GROUPS * E_PER_GROUP
D_EXPERT = 256
ALPHA = 2.0 ** 0.25
LN_EPS = 1e-5
RMS_EPS = 1e-6
L2_EPS = 1e-6

C_QKV = 3 * D_A
C_Z = 3 * D_A
C_P = 4 * D_A
C_BA = 4 * D_A + D_B
C_TOT = C_BA + 128
LANE_B = 0
LANE_A = N_HEADS

VMEM_LIMIT = 56 * 1024 * 1024


def _dot(a, b):
    return jnp.dot(a.astype(BF16), b.astype(BF16), preferred_element_type=F32)


def _dot_nt(a, b):
    return lax.dot_general(a.astype(BF16), b.astype(BF16), (((1,), (1,)), ((), ())), preferred_element_type=F32)


def _split3(x):
    hi = x.astype(BF16)
    r = x - hi.astype(F32)
    mid = r.astype(BF16)
    lo = (r - mid.astype(F32)).astype(BF16)
    return hi, mid, lo


def _dot01(m01, x):
    hi, mid, lo = _split3(x)
    f = lambda p: jnp.dot(m01, p, preferred_element_type=F32)
    return f(hi) + f(mid) + f(lo)


def _silu(x):
    return x * jax.nn.sigmoid(x)


def _softplus(x):
    return jnp.maximum(x, 0.0) + jnp.log1p(jnp.exp(-jnp.abs(x)))


def _iota2(n, m):
    return lax.broadcasted_iota(jnp.int32, (n, m), 0), lax.broadcasted_iota(jnp.int32, (n, m), 1)


def _proj_kernel(x_ref, w_ref, o_ref):
    o_ref[...] = jnp.dot(x_ref[...].astype(BF16), w_ref[...], preferred_element_type=F32)


def _in_proj(x2d, w_cat, tm):
    t = x2d.shape[0]
    return pl.pallas_call(
        _proj_kernel,
        out_shape=jax.ShapeDtypeStruct((t, C_TOT), F32),
        grid=(t // tm,),
        in_specs=[pl.BlockSpec((tm, D_MODEL), lambda i: (i, 0)),
                  pl.BlockSpec((D_MODEL, C_TOT), lambda i: (0, 0))],
        out_specs=pl.BlockSpec((tm, C_TOT), lambda i: (i, 0)),
        compiler_params=pltpu.CompilerParams(dimension_semantics=("parallel",), vmem_limit_bytes=VMEM_LIMIT),
        name="in_proj",
    )(x2d, w_cat)


def _unit_lower_inverse(a_list, r, c, chunk_shift):
    b0 = min(4, chunk_shift)
    eye = jnp.where(r == c, 1.0, 0.0).astype(F32)
    blk = (r >> b0) == (c >> b0)
    xs = [jnp.where(blk, a, 0.0) for a in a_list]
    ts = [eye - x for x in xs]
    for _ in range(b0 - 1):
        xs = [_dot(x, x) for x in xs]
        ts = [t + _dot(t, x) for t, x in zip(ts, xs)]
    for lvl in range(b0, chunk_shift):
        m = ((r >> (lvl + 1)) == (c >> (lvl + 1))) & ((r >> lvl) != (c >> lvl))
        tmp = [_dot(t, jnp.where(m, a, 0.0)) for t, a in zip(ts, a_list)]
        ts = [t - _dot(x, t) for t, x in zip(ts, tmp)]
    return ts


def _gate_slabs(ba, arow, dtrow, chunk_shift):
    n = ba.shape[0]
    beta = jax.nn.sigmoid(ba)
    g = -jnp.exp(arow) * _softplus(ba + dtrow)
    r, c = _iota2(n, n)
    same = (r >> chunk_shift) == (c >> chunk_shift)
    ltri = jnp.where(same & (r >= c), 1.0, 0.0).astype(BF16)
    lall = jnp.where(same, 1.0, 0.0).astype(BF16)
    cs = _dot01(jnp.concatenate([ltri, lall], axis=0), g)
    return beta, cs[:n], cs[n:]


def _heads_prepare(ys, betas, gcs, egcs, chunk_shift):
    n = ys[0].shape[0]
    nb = n // GDN_BLOCK
    r, c = _iota2(GDN_BLOCK, GDN_BLOCK)
    same = (r >> chunk_shift) == (c >> chunk_shift)
    incl = same & (r >= c)
    strict = same & (r > c)
    qs, ks, a_list, rhs, decays = [], [], [], [], []
    gc_ts = [g.T for g in gcs]
    for (y, beta_s, gc_s, egc_s, gc_t), h in itertools.product(zip(ys, betas, gcs, egcs, gc_ts), range(N_HEADS)):
        q = _l2norm(y[:, h * HEAD_DIM:(h + 1) * HEAD_DIM]) * (HEAD_DIM ** -0.5)
        k = _l2norm(y[:, D_A + h * HEAD_DIM:D_A + (h + 1) * HEAD_DIM])
        v = y[:, 2 * D_A + h * HEAD_DIM:2 * D_A + (h + 1) * HEAD_DIM]
        la = LANE_A + h
        beta_c = beta_s[:, LANE_B + h:LANE_B + h + 1]
        kb = k * beta_c
        rhs_h = jnp.concatenate([v * beta_c, kb * egc_s[:, la:la + 1]], axis=1)
        for bi in range(nb):
            blk = slice(bi * GDN_BLOCK, (bi + 1) * GDN_BLOCK)
            decay = jnp.exp(jnp.where(incl, gc_s[blk, la:la + 1] - gc_t[la:la + 1, blk], -jnp.inf))
            a_list.append(jnp.where(strict, _dot_nt(kb[blk], k[blk]) * decay, 0.0))
            rhs.append(rhs_h[blk])
            decays.append(decay)
        qs.append(q)
        ks.append(k)
    ts = _unit_lower_inverse(a_list, r, c, chunk_shift)
    sols = [_dot(t, x) for t, x in zip(ts, rhs)]
    us, ws, qkds = [], [], []
    for h in range(len(qs)):
        sol = jnp.concatenate(sols[h * nb:(h + 1) * nb], axis=0) if nb > 1 else sols[h]
        us.append(sol[:, :HEAD_DIM])
        ws.append(sol[:, HEAD_DIM:])
        qkds.append([_dot_nt(qs[h][bi * GDN_BLOCK:(bi + 1) * GDN_BLOCK], ks[h][bi * GDN_BLOCK:(bi + 1) * GDN_BLOCK])
                     * decays[h * nb + bi] for bi in range(nb)])
    return qs, ks, us, ws, qkds


def _l2norm(x):
    return x * lax.rsqrt(jnp.sum(x * x, axis=-1, keepdims=True) + L2_EPS)


def _gated_rmsnorm(o, z, normw):
    o = o * lax.rsqrt(jnp.mean(o * o, axis=-1, keepdims=True) + RMS_EPS) * normw
    return o * _silu(z)


def _pool_out(s, cnt, p_g, poolw_g, pscale_g):
    d = s / cnt - p_g
    return _dot(d, poolw_g) * pscale_g


def _mixer_prompt_kernel(proj_ref, convw_ref, arow_ref, dtrow_ref, normw_ref, poolw_ref, pscale_ref,
                         mix_ref, sfin_ref, cc_ref, pc_ref, s_ref, *, lb, nseq):
    l = pl.program_id(1)
    n = lb
    csz = 1 << CHUNK_SHIFT

    @pl.when(l == 0)
    def _init():
        cc_ref[...] = jnp.zeros_like(cc_ref)
        pc_ref[...] = jnp.zeros_like(pc_ref)
        s_ref[...] = jnp.zeros_like(s_ref)

    cw = convw_ref[...]
    ys, betas, gcs, egcs, ekgs, egls = [], [], [], [], [], []
    for si in range(nseq):
        u = proj_ref[si, :, 0:C_QKV]
        ext = jnp.concatenate([cc_ref[si], u], axis=0)
        acc = ext * cw[CONV_W - 1:CONV_W, :]
        for d in range(1, CONV_W):
            acc = acc + pltpu.roll(ext, d, 0) * cw[CONV_W - 1 - d:CONV_W - d, :]
        cc_ref[si] = u[n - 8:n, :]
        ys.append(_silu(acc[8:, :]))
        beta_s, gc_s, gl_s = _gate_slabs(proj_ref[si, :, C_BA:C_TOT], arow_ref[...], dtrow_ref[...], CHUNK_SHIFT)
        betas.append(beta_s)
        gcs.append(gc_s)
        egcs.append(jnp.exp(gc_s))
        ekgs.append(jnp.exp(gl_s - gc_s))
        egls.append(jnp.exp(gl_s))

    qs, ks, us, ws, qkds = _heads_prepare(ys, betas, gcs, egcs, CHUNK_SHIFT)
    n_heads = nseq * N_HEADS

    zero = jnp.zeros((csz, 2 * HEAD_DIM), F32)
    n_chunks = n // csz
    qps, ops, kns, egl_reps = [], [], [], []
    for hh in range(n_heads):
        si, h = divmod(hh, N_HEADS)
        la = LANE_A + h
        wu = jnp.concatenate([ws[hh], us[hh]], axis=1)
        qw = jnp.concatenate([_dot(qkd, wu[bi * GDN_BLOCK:(bi + 1) * GDN_BLOCK])
                              for bi, qkd in enumerate(qkds[hh])], axis=0)
        qps.append(qs[hh] * egcs[si][:, la:la + 1] - qw[:, :HEAD_DIM])
        ops.append(qw[:, HEAD_DIM:])
        kg_t = (ks[hh] * ekgs[si][:, la:la + 1]).T
        kn = []
        for ci in range(n_chunks):
            rows = slice(ci * csz, (ci + 1) * csz)
            pair = slice((ci // 2) * 2 * csz, (ci // 2 + 1) * 2 * csz)
            half = jnp.concatenate([wu[rows], zero] if ci % 2 == 0 else [zero, wu[rows]], axis=0)
            kn.append(_dot(kg_t[:, pair], half))
        kns.append(kn)
        egl_reps.append(jnp.broadcast_to(egls[si][:, la:la + 1], (n, HEAD_DIM)))

    states = [s_ref[hh // N_HEADS, hh % N_HEADS] for hh in range(n_heads)]
    outs = [[] for _ in range(n_heads)]
    for ci in range(n_chunks):
        rows = slice(ci * csz, (ci + 1) * csz)
        for hh in range(n_heads):
            s = states[hh]
            outs[hh].append(_dot(qps[hh][rows], s) + ops[hh][rows])
            kn = kns[hh][ci]
            states[hh] = (s * egl_reps[hh][ci * csz:ci * csz + 1, :] - _dot(kn[:, :HEAD_DIM], s)) + kn[:, HEAD_DIM:]
    for hh in range(n_heads):
        si, h = divmod(hh, N_HEADS)
        hs = slice(h * HEAD_DIM, (h + 1) * HEAD_DIM)
        s_ref[si, h] = states[hh]
        o = jnp.concatenate(outs[hh], axis=0)
        mix_ref[si, :, hs] = _gated_rmsnorm(o, proj_ref[si, :, C_Z + h * HEAD_DIM:C_Z + (h + 1) * HEAD_DIM],
                                            normw_ref[...])

    sfin_ref[...] = s_ref[...]

    r, c = _iota2(n, n + 16)
    lag = r + 16 - c
    pos = l * n + lax.broadcasted_iota(jnp.int32, (n, 1), 0)
    for si in range(nseq):
        p = proj_ref[si, :, C_P:C_P + D_B]
        extp = jnp.concatenate([pc_ref[si], p], axis=0)
        pc_ref[si] = p[n - 16:n, :]
        for gi, w in enumerate(POOL_WINDOWS):
            gs = slice(gi * HEAD_DIM, (gi + 1) * HEAD_DIM)
            band = jnp.where((lag >= 0) & (lag < w), 1.0, 0.0).astype(BF16)
            cnt = jnp.minimum(pos + 1, w).astype(F32)
            mix_ref[si, :, D_A + gi * HEAD_DIM:D_A + (gi + 1) * HEAD_DIM] = _pool_out(
                _dot01(band, extp[:, gs]), cnt, p[:, gs], poolw_ref[gi], pscale_ref[:, gs])


def _mixer_prompt(proj, conv_w, arow, dtrow, normw, poolw, pscale, lb, nseq):
    b, seq, _ = proj.shape
    const2 = lambda i, j: (0, 0)
    return pl.pallas_call(
        functools.partial(_mixer_prompt_kernel, lb=lb, nseq=nseq),
        out_shape=(jax.ShapeDtypeStruct((b, seq, D_MODEL), F32),
                   jax.ShapeDtypeStruct((b, N_HEADS, HEAD_DIM, HEAD_DIM), F32)),
        grid=(b // nseq, seq // lb),
        in_specs=[pl.BlockSpec((nseq, lb, C_TOT), lambda i, j: (i, j, 0)),
                  pl.BlockSpec((CONV_W, C_QKV), const2),
                  pl.BlockSpec((1, 128), const2),
                  pl.BlockSpec((1, 128), const2),
                  pl.BlockSpec((1, HEAD_DIM), const2),
                  pl.BlockSpec((N_GROUPS, HEAD_DIM, HEAD_DIM), lambda i, j: (0, 0, 0)),
                  pl.BlockSpec((1, D_B), const2)],
        out_specs=(pl.BlockSpec((nseq, lb, D_MODEL), lambda i, j: (i, j, 0)),
                   pl.BlockSpec((nseq, N_HEADS, HEAD_DIM, HEAD_DIM), lambda i, j: (i, 0, 0, 0))),
        scratch_shapes=[pltpu.VMEM((nseq, 8, C_QKV), F32), pltpu.VMEM((nseq, 16, D_B), F32),
                        pltpu.VMEM((nseq, N_HEADS, HEAD_DIM, HEAD_DIM), F32)],
        compiler_params=pltpu.CompilerParams(dimension_semantics=("parallel", "arbitrary"),
                                             vmem_limit_bytes=VMEM_LIMIT),
        name="mixer_prompt",
    )(proj, conv_w, arow, dtrow, normw, poolw, pscale)


def _mixer_sample_kernel(proj_ref, cst_ref, pst_ref, sin_ref, convw_ref, arow_ref, dtrow_ref, normw_ref,
                         poolw_ref, pscale_ref, mix_ref, sout_ref, *, ns, seq, start):
    n = ns * seq
    sshift = seq.bit_length() - 1
    rowi = lax.broadcasted_iota(jnp.int32, (n, 1), 0)
    tpos = rowi & (seq - 1)

    u = proj_ref[:, 0:C_QKV]
    st = cst_ref[...]
    cw = convw_ref[...]
    acc = u * cw[CONV_W - 1:CONV_W, :]
    for d in range(1, CONV_W):
        term = jnp.where(tpos >= d, pltpu.roll(u, d, 0), pltpu.roll(st, n - seq + d, 0))
        acc = acc + term * cw[CONV_W - 1 - d:CONV_W - d, :]
    y = _silu(acc)

    beta_s, gc_s, gl_s = _gate_slabs(proj_ref[:, C_BA:C_TOT], arow_ref[...], dtrow_ref[...], sshift)
    egc_s = jnp.exp(gc_s)
    ekg_s = jnp.exp(gl_s - gc_s)
    egl_s = jnp.exp(gl_s)
    qs, ks, us, ws_, qkds = _heads_prepare([y], [beta_s], [gc_s], [egc_s], sshift)

    for h in range(N_HEADS):
        hs = slice(h * HEAD_DIM, (h + 1) * HEAD_DIM)
        la = LANE_A + h
        u_, w_, qkd = us[h], ws_[h], qkds[h][0]
        qg = qs[h] * egc_s[:, la:la + 1]
        kg_t = (ks[h] * ekg_s[:, la:la + 1]).T
        egl_rep = jnp.broadcast_to(egl_s[:, la:la + 1], (n, HEAD_DIM))
        ws_w, ws_q = [], []
        for si in range(ns):
            rows = slice(si * seq, (si + 1) * seq)
            ws = _dot(jnp.concatenate([w_[rows], qg[rows]], axis=0), sin_ref[si, h])
            ws_w.append(ws[:seq])
            ws_q.append(ws[seq:])
        vn = u_ - jnp.concatenate(ws_w, axis=0)
        o = jnp.concatenate(ws_q, axis=0) + _dot(qkd, vn)
        for si in range(ns):
            vmask = jnp.where((rowi >> sshift) == si, vn, 0.0)
            sout_ref[si, h] = sin_ref[si, h] * egl_rep[si * seq:si * seq + 1, :] + _dot(kg_t, vmask)
        mix_ref[:, hs] = _gated_rmsnorm(o, proj_ref[:, C_Z + h * HEAD_DIM:C_Z + (h + 1) * HEAD_DIM], normw_ref[...])

    p = proj_ref[:, C_P:C_P + D_B]
    pst = pst_ref[...]
    r, c = _iota2(n, n)
    band_new_base = ((r >> sshift) == (c >> sshift)) & (r >= c)
    r2, c2 = _iota2(n, ns * 16)
    same2 = (r2 >> sshift) == (c2 >> 4)
    t2 = r2 & (seq - 1)
    j2 = c2 & 15
    pos = start + tpos
    for gi, w in enumerate(POOL_WINDOWS):
        gs = slice(gi * HEAD_DIM, (gi + 1) * HEAD_DIM)
        band_new = jnp.where(band_new_base & ((r - c) < w), 1.0, 0.0).astype(BF16)
        band_st = jnp.where(same2 & (j2 >= 17 + t2 - w), 1.0, 0.0).astype(BF16)
        s = _dot01(band_new, p[:, gs]) + _dot01(band_st, pst[:, gs])
        cnt = jnp.minimum(pos + 1, w).astype(F32)
        mix_ref[:, D_A + gi * HEAD_DIM:D_A + (gi + 1) * HEAD_DIM] = _pool_out(
            s, cnt, p[:, gs], poolw_ref[gi], pscale_ref[:, gs])


def _mixer_sample(proj, cst, pst, sin, conv_w, arow, dtrow, normw, poolw, pscale, ns, seq, start):
    t = proj.shape[0]
    nb = t // seq
    n = ns * seq
    assert n == GDN_BLOCK and seq & (seq - 1) == 0 and seq >= CONV_W - 1
    const1 = lambda i: (0, 0)
    return pl.pallas_call(
        functools.partial(_mixer_sample_kernel, ns=ns, seq=seq, start=start),
        out_shape=(jax.ShapeDtypeStruct((t, D_MODEL), F32),
                   jax.ShapeDtypeStruct((nb, N_HEADS, HEAD_DIM, HEAD_DIM), F32)),
        grid=(nb // ns,),
        in_specs=[pl.BlockSpec((n, C_TOT), lambda i: (i, 0)),
                  pl.BlockSpec((n, C_QKV), lambda i: (i, 0)),
                  pl.BlockSpec((ns * 16, D_B), lambda i: (i, 0)),
                  pl.BlockSpec((ns, N_HEADS, HEAD_DIM, HEAD_DIM), lambda i: (i, 0, 0, 0)),
                  pl.BlockSpec((CONV_W, C_QKV), const1),
                  pl.BlockSpec((1, 128), const1),
                  pl.BlockSpec((1, 128), const1),
                  pl.BlockSpec((1, HEAD_DIM), const1),
                  pl.BlockSpec((N_GROUPS, HEAD_DIM, HEAD_DIM), lambda i: (0, 0, 0)),
                  pl.BlockSpec((1, D_B), const1)],
        out_specs=(pl.BlockSpec((n, D_MODEL), lambda i: (i, 0)),
                   pl.BlockSpec((ns, N_HEADS, HEAD_DIM, HEAD_DIM), lambda i: (i, 0, 0, 0))),
        compiler_params=pltpu.CompilerParams(dimension_semantics=("parallel",), vmem_limit_bytes=VMEM_LIMIT),
        name="mixer_sample",
    )(proj, cst, pst, sin, conv_w, arow, dtrow, normw, poolw, pscale)


ROW_SLAB = D_MODEL // 128


def _to_slabs(ref, x, n):
    for c in range(ROW_SLAB):
        ref[pl.ds(c, n, stride=ROW_SLAB), :] = x[:, c * 128:(c + 1) * 128]


def _from_slabs(ref, n):
    return jnp.concatenate([ref[pl.ds(c, n, stride=ROW_SLAB), :] for c in range(ROW_SLAB)], axis=1)


def _slab(ref, row):
    if isinstance(row, int):
        return ref.at[pl.ds(row * ROW_SLAB, ROW_SLAB)]
    return ref.at[pl.ds(pl.multiple_of(row * ROW_SLAB, ROW_SLAB), ROW_SLAB)]


def _layer_norm(x, g, b):
    mu = jnp.mean(x, axis=-1, keepdims=True)
    xc = x - mu
    var = jnp.mean(xc * xc, axis=-1, keepdims=True)
    return xc * lax.rsqrt(var + LN_EPS) * g + b


def _outproj_router_kernel(mixp_ref, xp_ref, mixs_ref, xs_ref, wout_ref, g1_ref, b1_ref, wrh_ref, wrl_ref, br_ref,
                           h_ref, sel_ref, gw_ref, cnt_ref, carry_ref, *, n_prompt_tiles):
    @pl.when(pl.program_id(0) == 0)
    def _init():
        carry_ref[...] = jnp.zeros_like(carry_ref)

    is_prompt = pl.program_id(0) < n_prompt_tiles
    tm = h_ref.shape[0] // ROW_SLAB
    n_parts = 2 if tm % 16 == 0 else 1
    pm = tm // n_parts
    parts = [slice(i * pm, (i + 1) * pm) for i in range(n_parts)]
    each = lambda fn, *lists: [fn(*args) for args in zip(*lists)]
    lane = lax.broadcasted_iota(jnp.int32, (pm, 128), 1)
    big = jnp.int32(1 << 20)
    neg = -jnp.inf
    lsum = lambda v: jnp.sum(v, axis=1, keepdims=True)
    lmax = lambda v: jnp.max(v, axis=1, keepdims=True)
    lmin = lambda v: jnp.min(v, axis=1, keepdims=True)

    mix = [jnp.where(is_prompt, mixp_ref[rs, :], mixs_ref[rs, :]) for rs in parts]
    x = [jnp.where(is_prompt, xp_ref[rs, :], xs_ref[rs, :]) for rs in parts]
    proj = each(lambda m: _dot(m, wout_ref[...]), mix)
    h = each(lambda xi, pi: _layer_norm(ALPHA * xi + pi, g1_ref[...], b1_ref[...]), x, proj)
    for i, hi in enumerate(h):
        for c in range(ROW_SLAB):
            h_ref[pl.ds(i * pm * ROW_SLAB + c, pm, stride=ROW_SLAB), :] = hi[:, c * 128:(c + 1) * 128]
    f = lambda a, b: jnp.dot(a, b, preferred_element_type=F32)
    split = each(_split3, h)
    logits = each(lambda s: f(s[0], wrh_ref[...]) + (f(s[1], wrh_ref[...]) + f(s[0], wrl_ref[...])) + br_ref[...],
                  split)
    gmask = (lane >= N_EXPERTS) & (lane < N_EXPERTS + N_GROUPS)
    lg = each(lambda l: jnp.where(gmask, l, neg), logits)
    gmax = each(lmax, lg)
    gidx = each(lambda l, m: lmin(jnp.where(l == m, lane - N_EXPERTS, big)), lg, gmax)
    pg = each(lambda l, m: 1.0 / lsum(jnp.where(gmask, jnp.exp(l - m), 0.0)), logits, gmax)
    emask = each(lambda g: (lane < N_EXPERTS) & ((lane >> 3) == g), gidx)
    le = each(lambda m, l: jnp.where(m, l, neg), emask, logits)
    v1 = each(lmax, le)
    i1 = each(lambda l, v, m: lmin(jnp.where((l == v) & m, lane, big)), le, v1, emask)
    emask2 = each(lambda m, i: m & (lane != i), emask, i1)
    le2 = each(lambda m, l: jnp.where(m, l, neg), emask2, logits)
    v2 = each(lmax, le2)
    i2 = each(lambda l, v, m: lmin(jnp.where((l == v) & m, lane, big)), le2, v2, emask2)
    e2 = each(lambda a, b: jnp.exp(a - b), v2, v1)
    for rs, e, g in zip(parts, e2, pg):
        den = 1.0 + e
        gw_ref[rs, :] = jnp.where(lane == 0, (1.0 / den) * g, jnp.where(lane == 1, (e / den) * g, 0.0))
    onehot = each(lambda a, b: jnp.where((lane == a) | (lane == b), 1.0, 0.0), i1, i2)
    r, c = _iota2(pm, pm)
    tri = jnp.where(r > c, 1.0, 0.0).astype(BF16)
    inside = each(lambda o: jnp.dot(tri, o.astype(BF16), preferred_element_type=F32), onehot)
    carry = carry_ref[...]
    for rs, o, ins, a, b in zip(parts, onehot, inside, i1, i2):
        before = ins + carry
        r1 = lsum(jnp.where(lane == a, before, 0.0)).astype(jnp.int32)
        r2 = lsum(jnp.where(lane == b, before, 0.0)).astype(jnp.int32)
        sel_ref[rs, :] = jnp.where(lane == 0, a, jnp.where(lane == 1, b, jnp.where(lane == 2, r1,
                                                                                    jnp.where(lane == 3, r2, 0))))
        carry = carry + jnp.sum(o, axis=0, keepdims=True)
    carry_ref[...] = carry
    cnt_ref[...] = carry.astype(jnp.int32)


def _outproj_router(mix_p, x_p, mix_s, x_s, wout, g1, b1, wrh, wrl, br, tm):
    tp, ts = x_p.shape[0], x_s.shape[0]
    t = tp + ts
    npt = tp // tm
    row = lambda i: (i, 0)
    prow = lambda i: (jnp.minimum(i, npt - 1), 0)
    srow = lambda i: (jnp.maximum(i - npt, 0), 0)
    const = lambda i: (0, 0)
    return pl.pallas_call(
        functools.partial(_outproj_router_kernel, n_prompt_tiles=npt),
        out_shape=(jax.ShapeDtypeStruct((t * ROW_SLAB, 128), F32), jax.ShapeDtypeStruct((t, 128), jnp.int32),
                   jax.ShapeDtypeStruct((t, 128), F32), jax.ShapeDtypeStruct((1, 128), jnp.int32)),
        grid=(t // tm,),
        in_specs=[pl.BlockSpec((tm, D_MODEL), prow), pl.BlockSpec((tm, D_MODEL), prow),
                  pl.BlockSpec((tm, D_MODEL), srow), pl.BlockSpec((tm, D_MODEL), srow),
                  pl.BlockSpec((D_MODEL, D_MODEL), const), pl.BlockSpec((1, D_MODEL), const),
                  pl.BlockSpec((1, D_MODEL), const), pl.BlockSpec((D_MODEL, 128), const),
                  pl.BlockSpec((D_MODEL, 128), const), pl.BlockSpec((1, 128), const)],
        out_specs=(pl.BlockSpec((tm * ROW_SLAB, 128), row), pl.BlockSpec((tm, 128), row),
                   pl.BlockSpec((tm, 128), row), pl.BlockSpec((1, 128), const)),
        scratch_shapes=[pltpu.VMEM((1, 128), F32)],
        compiler_params=pltpu.CompilerParams(dimension_semantics=("arbitrary",), vmem_limit_bytes=VMEM_LIMIT),
        name="outproj_router",
    )(mix_p, x_p, mix_s, x_s, wout, g1, b1, wrh, wrl, br)


EXPERT_TILE = 512
ROUTE_TILE = 256


def _dispatch_kernel(pad_start_ref, pad_cnt_ref, tail_ref, h_ref, pos_ref, xs_ref, zero_ref, sem, zsem, tsem,
                     *, td, tmx):
    @pl.when(pl.program_id(0) == 0)
    def _zero_unused_rows():
        zero_ref[...] = jnp.zeros_like(zero_ref)

        def zcopy(row, n_rows):
            return pltpu.make_async_copy(
                zero_ref.at[pl.ds(0, n_rows * ROW_SLAB)],
                xs_ref.at[pl.ds(pl.multiple_of(row * ROW_SLAB, ROW_SLAB), n_rows * ROW_SLAB)], zsem)

        tile_rows = min(td, tmx) * ROW_SLAB
        tcopy = lambda tile: pltpu.make_async_copy(
            zero_ref.at[pl.ds(0, tile_rows)],
            xs_ref.at[pl.ds(pl.multiple_of(tile * tile_rows, tile_rows), tile_rows)], tsem)

        def pad_pieces(e, start_not_wait):
            start = pad_start_ref[e]
            n = pad_cnt_ref[e]
            piece = tmx // 2
            while piece >= 1:
                @pl.when((n & piece) != 0)
                def _(piece=piece):
                    cp = zcopy(start + (n & ~(2 * piece - 1)), piece)
                    cp.start() if start_not_wait else cp.wait()
                piece //= 2

        def start_pads(e, carry):
            pad_pieces(e, True)
            return carry

        def wait_pads(e, carry):
            pad_pieces(e, False)
            return carry

        lax.fori_loop(0, N_EXPERTS, start_pads, 0)

        def tail_start(r, carry):
            tcopy(tail_ref[0] + r).start()
            return carry

        lax.fori_loop(0, tail_ref[1], tail_start, 0)
        lax.fori_loop(0, N_EXPERTS, wait_pads, 0)

        def tail_wait(r, carry):
            tcopy(0).wait()
            return carry

        lax.fori_loop(0, tail_ref[1], tail_wait, 0)

    copies = []
    for t in range(td):
        for k in range(2):
            cp = pltpu.make_async_copy(_slab(h_ref, t), _slab(xs_ref, pos_ref[0, 0, k * td + t]), sem)
            cp.start(priority=k)
            copies.append(cp)
    for cp in copies:
        cp.wait()


def _dispatch(h, pos3, pad_start, pad_cnt, tail, n_rows, td, tmx):
    t = h.shape[0] // ROW_SLAB
    assert td % min(td, tmx) == 0 and tmx % min(td, tmx) == 0 and tmx // 2 <= td
    return pl.pallas_call(
        functools.partial(_dispatch_kernel, td=td, tmx=tmx),
        out_shape=jax.ShapeDtypeStruct((n_rows * ROW_SLAB, 128), F32),
        grid_spec=pltpu.PrefetchScalarGridSpec(
            num_scalar_prefetch=3,
            grid=(t // td,),
            in_specs=[pl.BlockSpec((td * ROW_SLAB, 128), lambda i, *_: (i, 0)),
                      pl.BlockSpec((1, 1, 2 * td), lambda i, *_: (i, 0, 0), memory_space=pltpu.SMEM)],
            out_specs=pl.BlockSpec(memory_space=pl.ANY),
            scratch_shapes=[pltpu.VMEM((td * ROW_SLAB, 128), F32),
                            pltpu.SemaphoreType.DMA, pltpu.SemaphoreType.DMA, pltpu.SemaphoreType.DMA]),
        compiler_params=pltpu.CompilerParams(dimension_semantics=("arbitrary",), vmem_limit_bytes=VMEM_LIMIT),
        name="moe_dispatch",
    )(pad_start, pad_cnt, tail, h, pos3)


N_RING = 3


def _experts_kernel(tile_e_ref, tile_ok_ref, xs_ref, wg_ref, wu_ref, wd_ref, o_ref, ring_ref, sems, *, tmx, nt):
    j = pl.program_id(0)
    ok = tile_ok_ref[j] != 0
    tile_rows = tmx * ROW_SLAB

    def fetch(tile, slot):
        return pltpu.make_async_copy(xs_ref.at[pl.ds(pl.multiple_of(tile * tile_rows, tile_rows), tile_rows)],
                                     ring_ref.at[slot], sems.at[slot])

    @pl.when(j == 0)
    def _prologue():
        fetch(0, 0).start()

        @pl.when(tile_ok_ref[min(1, nt - 1)] != 0)
        def _():
            fetch(1, 1 % N_RING).start()

    ahead = j + (N_RING - 1)

    @pl.when(jnp.logical_and(ahead < nt, tile_ok_ref[jnp.minimum(ahead, nt - 1)] != 0))
    def _prefetch():
        fetch(ahead, lax.rem(ahead, N_RING)).start()

    @pl.when(ok)
    def _compute():
        slot = lax.rem(j, N_RING)
        fetch(j, slot).wait()
        x = _from_slabs(ring_ref.at[slot], tmx).astype(BF16)
        a = jnp.dot(x, wg_ref[0].astype(BF16), preferred_element_type=F32)
        b = jnp.dot(x, wu_ref[0].astype(BF16), preferred_element_type=F32)
        act = (_silu(a) * b).astype(BF16)
        _to_slabs(o_ref, jnp.dot(act, wd_ref[0].astype(BF16), preferred_element_type=F32), tmx)

    @pl.when(jnp.logical_not(ok))
    def _unused_tile():
        o_ref[...] = jnp.zeros_like(o_ref)


def _experts(xs, tile_e, tile_ok, wg, wu, wd, tmx):
    nt = tile_e.shape[0]
    wsel = lambda j, te, ok: (te[j], 0, 0)
    own = lambda j, te, ok: (j, 0)
    return pl.pallas_call(
        functools.partial(_experts_kernel, tmx=tmx, nt=nt),
        out_shape=jax.ShapeDtypeStruct(xs.shape, F32),
        grid_spec=pltpu.PrefetchScalarGridSpec(
            num_scalar_prefetch=2,
            grid=(nt,),
            in_specs=[pl.BlockSpec(memory_space=pl.ANY),
                      pl.BlockSpec((1, D_MODEL, D_EXPERT), wsel),
                      pl.BlockSpec((1, D_MODEL, D_EXPERT), wsel),
                      pl.BlockSpec((1, D_EXPERT, D_MODEL), wsel)],
            out_specs=pl.BlockSpec((tmx * ROW_SLAB, 128), own),
            scratch_shapes=[pltpu.VMEM((N_RING, tmx * ROW_SLAB, 128), F32), pltpu.SemaphoreType.DMA((N_RING,))]),
        compiler_params=pltpu.CompilerParams(dimension_semantics=("arbitrary",), vmem_limit_bytes=VMEM_LIMIT),
        name="moe_experts",
    )(tile_e, tile_ok, xs, wg, wu, wd)


def _combine_kernel(h_ref, gw_ref, pos_ref, npos_ref, g2_ref, b2_ref, os_ref, y_ref, stage_ref, sems,
                    *, td, n_steps):
    s = pl.program_id(0)

    def gather(idx_ref, half, start):
        for t in range(td):
            for k in range(2):
                cp = pltpu.make_async_copy(_slab(os_ref, idx_ref[0, 0, (half * 2 + k) * td + t]),
                                           _slab(stage_ref.at[half, k], t), sems.at[half])
                if start:
                    cp.start(priority=k)
                else:
                    cp.wait()

    def finish(half):
        rows = slice(half * td, (half + 1) * td)
        gw = gw_ref[rows, :]
        moe = (gw[:, 0:1] * _from_slabs(stage_ref.at[half, 0], td)
               + gw[:, 1:2] * _from_slabs(stage_ref.at[half, 1], td))
        h = jnp.concatenate([h_ref[pl.ds(half * td * ROW_SLAB + c, td, stride=ROW_SLAB), :]
                             for c in range(ROW_SLAB)], axis=1)
        y_ref[rows, :] = _layer_norm(ALPHA * h + moe, g2_ref[...], b2_ref[...])

    @pl.when(s == 0)
    def _prologue():
        gather(pos_ref, 0, True)

    gather(pos_ref, 1, True)
    gather(pos_ref, 0, False)
    finish(0)

    @pl.when(s + 1 < n_steps)
    def _next_step_first_tile():
        gather(npos_ref, 0, True)

    gather(pos_ref, 1, False)
    finish(1)


def _combine(h, gw, pos, g2, b2, os, td, first_token, n_tokens):
    step = 2 * td
    n_steps = n_tokens // step
    off = first_token // step
    pos4 = jnp.stack([q.reshape(-1, 2, td) for q in pos], axis=2).reshape(-1, 1, 2 * step)
    row = lambda i: (i + off, 0)
    const = lambda i: (0, 0)
    return pl.pallas_call(
        functools.partial(_combine_kernel, td=td, n_steps=n_steps),
        out_shape=jax.ShapeDtypeStruct((n_tokens, D_MODEL), F32),
        grid=(n_steps,),
        in_specs=[pl.BlockSpec((step * ROW_SLAB, 128), row), pl.BlockSpec((step, 128), row),
                  pl.BlockSpec((1, 1, 2 * step), lambda i: (i + off, 0, 0), memory_space=pltpu.SMEM),
                  pl.BlockSpec((1, 1, 2 * step), lambda i: (jnp.minimum(i + 1, n_steps - 1) + off, 0, 0),
                               memory_space=pltpu.SMEM),
                  pl.BlockSpec((1, D_MODEL), const), pl.BlockSpec((1, D_MODEL), const),
                  pl.BlockSpec(memory_space=pl.ANY)],
        out_specs=pl.BlockSpec((step, D_MODEL), lambda i: (i, 0)),
        scratch_shapes=[pltpu.VMEM((2, 2, td * ROW_SLAB, 128), F32), pltpu.SemaphoreType.DMA((2,))],
        compiler_params=pltpu.CompilerParams(dimension_semantics=("arbitrary",), vmem_limit_bytes=VMEM_LIMIT),
        name="moe_combine",
    )(h, gw, pos4, pos4, g2, b2, os)


def _route_plan(sel, cnt, t, tmx, td):
    i32 = jnp.int32
    counts = cnt[0, :N_EXPERTS]
    padded = ((counts + tmx - 1) // tmx) * tmx
    ex = jnp.arange(N_EXPERTS, dtype=i32)
    ends = jnp.sum(jnp.where(ex[None, :] <= ex[:, None], padded[None, :], 0), axis=1).astype(i32)
    offs = ends - padded
    pos = tuple((jnp.sum(jnp.where(sel[:, k, None] == ex, offs, 0), axis=-1) + sel[:, 2 + k]).astype(i32)
                for k in range(2))
    nt = 2 * t // tmx + N_EXPERTS
    n_used = ends[-1] // tmx
    tile = jnp.arange(nt, dtype=i32)
    tile_idx = jnp.minimum(tile, jnp.maximum(n_used - 1, 0))
    tile_e = jnp.minimum(jnp.sum((ends[None, :] <= (tile_idx * tmx)[:, None]).astype(i32), axis=1), N_EXPERTS - 1)
    tile_ok = (tile < n_used).astype(i32)
    pieces = tmx // min(td, tmx)
    tail = jnp.stack([n_used * pieces, (nt - n_used) * pieces]).astype(i32)
    return (pos, tile_e, tile_ok, (offs + counts).astype(i32), (padded - counts).astype(i32), tail, nt * tmx)


def _tile(t, want):
    tm = min(want, t)
    while t % tm:
        tm //= 2
    return tm


def _prep_weights(w_in, conv_w, a_log, dt_bias, gdn_norm_w, pool_w, pool_scale, w_out, ln1_g, ln1_b,
                  w_rg, b_rg, w_re, b_re, w_gate, w_up, w_down, ln2_g, ln2_b):
    col_b = 4 * D_A
    col_p = 4 * D_A + 2 * N_HEADS
    w_cat = jnp.concatenate([w_in[:, :col_b], w_in[:, col_p:], w_in[:, col_b:col_p],
                             jnp.zeros((D_MODEL, 128 - 2 * N_HEADS), w_in.dtype)], axis=1).astype(BF16)
    lane_pad = lambda v, off: jnp.zeros((1, 128), F32).at[0, off:off + v.shape[0]].set(v.astype(F32))
    w_r = jnp.concatenate([w_re, w_rg, jnp.zeros((D_MODEL, 128 - N_EXPERTS - N_GROUPS), F32)], axis=1)
    wrh = w_r.astype(BF16)
    wrl = (w_r - wrh.astype(F32)).astype(BF16)
    b_r = jnp.zeros((1, 128), F32).at[0, :N_EXPERTS].set(b_re).at[0, N_EXPERTS:N_EXPERTS + N_GROUPS].set(b_rg)
    return dict(
        w_cat=w_cat, conv_w=conv_w, arow=lane_pad(a_log, LANE_A), dtrow=lane_pad(dt_bias, LANE_A),
        normw=gdn_norm_w.reshape(1, HEAD_DIM), poolw=pool_w.astype(BF16), pscale=pool_scale.reshape(1, D_B),
        wout=w_out.astype(BF16), g1=ln1_g.reshape(1, D_MODEL), b1=ln1_b.reshape(1, D_MODEL),
        wrh=wrh, wrl=wrl, br=b_r,
        wg=w_gate.reshape(N_EXPERTS, D_MODEL, D_EXPERT), wu=w_up.reshape(N_EXPERTS, D_MODEL, D_EXPERT),
        wd=w_down.reshape(N_EXPERTS, D_EXPERT, D_MODEL),
        g2=ln2_g.reshape(1, D_MODEL), b2=ln2_b.reshape(1, D_MODEL))


def _post_mixer(mix_p, x_p, mix_s, x_s, p):
    tp, ts = x_p.shape[0], x_s.shape[0]
    t = tp + ts
    tm = math.gcd(_tile(tp, 1024), _tile(ts, 1024))
    h, sel, gw, cnt = _outproj_router(mix_p, x_p, mix_s, x_s, p["wout"], p["g1"], p["b1"], p["wrh"], p["wrl"],
                                      p["br"], tm)
    td = math.gcd(_tile(tp, ROUTE_TILE), _tile(ts, ROUTE_TILE))
    pos, tile_e, tile_ok, pad_start, pad_cnt, tail, n_rows = _route_plan(sel, cnt, t, EXPERT_TILE, td)
    pos3 = jnp.stack([q.reshape(t // td, td) for q in pos], axis=1).reshape(t // td, 1, 2 * td)
    xs = _dispatch(h, pos3, pad_start, pad_cnt, tail, n_rows, td, EXPERT_TILE)
    os = _experts(xs, tile_e, tile_ok, p["wg"], p["wu"], p["wd"], EXPERT_TILE)
    tc = td if (tp // td) % 2 == 0 and (ts // td) % 2 == 0 else td // 2
    y_p = _combine(h, gw, pos, p["g2"], p["b2"], os, tc, 0, tp)
    y_s = _combine(h, gw, pos, p["g2"], p["b2"], os, tc, tp, ts)
    return y_p, y_s


def _mix_prompt(x, p, lb=256):
    b, seq, _ = x.shape
    x2d = x.reshape(b * seq, D_MODEL)
    proj = _in_proj(x2d, p["w_cat"], _tile(b * seq, 1024)).reshape(b, seq, C_TOT)
    mix, s_fin = _mixer_prompt(proj, p["conv_w"], p["arow"], p["dtrow"], p["normw"], p["poolw"], p["pscale"],
                               min(lb, seq), 2 if b % 2 == 0 else 1)
    conv_new = proj[:, seq - (CONV_W - 1):, 0:C_QKV]
    pool_new = proj[:, seq - POOL_BUF:, C_P:C_P + D_B]
    return x2d, mix.reshape(b * seq, D_MODEL), s_fin, conv_new, pool_new


def _mix_sample(x, s0, conv0, pool0, start, p, ns=16):
    b, seq, _ = x.shape
    x2d = x.reshape(b * seq, D_MODEL)
    proj = _in_proj(x2d, p["w_cat"], _tile(b * seq, 1024))
    cst = jnp.pad(conv0, ((0, 0), (seq - (CONV_W - 1), 0), (0, 0))).reshape(b * seq, C_QKV)
    pst = jnp.pad(pool0, ((0, 0), (1, 0), (0, 0))).reshape(b * 16, D_B)
    mix, s_new = _mixer_sample(proj, cst, pst, s0, p["conv_w"], p["arow"], p["dtrow"], p["normw"], p["poolw"],
                               p["pscale"], min(ns, b), seq, start)
    proj3 = proj.reshape(b, seq, C_TOT)
    conv_new = proj3[:, seq - (CONV_W - 1):, 0:C_QKV]
    pool_new = jnp.concatenate([pool0[:, seq:, :], proj3[:, :, C_P:C_P + D_B]], axis=1)
    return x2d, mix, s_new, conv_new, pool_new


def _layer(x_prompt, x_sample, s0, conv0, pool0, start, p):
    xp2d, mix_p, dp, cp, pp = _mix_prompt(x_prompt, p)
    xs2d, mix_s, ds, cs, ps = _mix_sample(x_sample, s0, conv0, pool0, start, p)
    y_p, y_s = _post_mixer(mix_p, xp2d, mix_s, xs2d, p)
    return y_p.reshape(x_prompt.shape), y_s.reshape(x_sample.shape), (dp, cp, pp), (ds, cs, ps)


def kernel(x_prompt, x_sample, state_delta, state_conv, state_pool, w_in, conv_w, a_log, dt_bias, gdn_norm_w,
           pool_w, pool_scale, w_out, ln1_g, ln1_b, w_rg, b_rg, w_re, b_re, w_gate, w_up, w_down, ln2_g, ln2_b):
    depth = w_in.shape[0]
    past_len = 16384
    yp, ys = x_prompt, x_sample
    outs = [[] for _ in range(6)]
    for l in range(depth):
        p = _prep_weights(w_in[l], conv_w[l], a_log[l], dt_bias[l], gdn_norm_w[l], pool_w[l], pool_scale[l],
                          w_out[l], ln1_g[l], ln1_b[l], w_rg[l], b_rg[l], w_re[l], b_re[l], w_gate[l], w_up[l],
                          w_down[l], ln2_g[l], ln2_b[l])
        yp, ys, st_p, st_s = _layer(yp, ys, state_delta[l], state_conv[l], state_pool[l], past_len, p)
        for lst, v in zip(outs, st_p + st_s):
            lst.append(v)
    return (yp, ys) + tuple(jnp.stack(v) for v in outs)
```

```python
import functools
import itertools
import math

import jax
import jax.numpy as jnp
from jax import lax
from jax.experimental import pallas as pl
from jax.experimental.pallas import tpu as pltpu

F32 = jnp.float32
BF16 = jnp.bfloat16

D_MODEL = 1024
D_A = 512
D_B = 512
HEAD_DIM = 128
N_HEADS = 4
CONV_W = 4
CHUNK_SHIFT = 6
GDN_BLOCK = 128
POOL_WINDOWS = (2, 4, 8, 16)
POOL_BUF = 15
N_GROUPS = 4
E_PER_GROUP = 8
N_EXPERTS = N_GROUPS * E_PER_GROUP
D_EXPERT = 256
ALPHA = 2.0 ** 0.25
LN_EPS = 1e-5
RMS_EPS = 1e-6
L2_EPS = 1e-6

C_QKV = 3 * D_A
C_Z = 3 * D_A
C_P = 4 * D_A
C_BA = 4 * D_A + D_B
C_TOT = C_BA + 128
LANE_B = 0
LANE_A = N_HEADS

VMEM_LIMIT = 56 * 1024 * 1024


def _dot(a, b):
    return jnp.dot(a.astype(BF16), b.astype(BF16), preferred_element_type=F32)


def _dot_nt(a, b):
    return lax.dot_general(a.astype(BF16), b.astype(BF16), (((1,), (1,)), ((), ())), preferred_element_type=F32)


def _split3(x):
    hi = x.astype(BF16)
    r = x - hi.astype(F32)
    mid = r.astype(BF16)
    lo = (r - mid.astype(F32)).astype(BF16)
    return hi, mid, lo


def _dot01(m01, x):
    hi, mid, lo = _split3(x)
    f = lambda p: jnp.dot(m01, p, preferred_element_type=F32)
    return f(hi) + f(mid) + f(lo)


def _silu(x):
    return x * jax.nn.sigmoid(x)


def _softplus(x):
    return jnp.maximum(x, 0.0) + jnp.log1p(jnp.exp(-jnp.abs(x)))


def _iota2(n, m):
    return lax.broadcasted_iota(jnp.int32, (n, m), 0), lax.broadcasted_iota(jnp.int32, (n, m), 1)


def _proj_kernel(x_ref, w_ref, o_ref):
    o_ref[...] = jnp.dot(x_ref[...].astype(BF16), w_ref[...], preferred_element_type=F32)


def _in_proj(x2d, w_cat, tm):
    t = x2d.shape[0]
    return pl.pallas_call(
        _proj_kernel,
        out_shape=jax.ShapeDtypeStruct((t, C_TOT), F32),
        grid=(t // tm,),
        in_specs=[pl.BlockSpec((tm, D_MODEL), lambda i: (i, 0)),
                  pl.BlockSpec((D_MODEL, C_TOT), lambda i: (0, 0))],
        out_specs=pl.BlockSpec((tm, C_TOT), lambda i: (i, 0)),
        compiler_params=pltpu.CompilerParams(dimension_semantics=("parallel",), vmem_limit_bytes=VMEM_LIMIT),
        name="in_proj",
    )(x2d, w_cat)


def _unit_lower_inverse(a_list, r, c, chunk_shift):
    b0 = min(4, chunk_shift)
    eye = jnp.where(r == c, 1.0, 0.0).astype(F32)
    blk = (r >> b0) == (c >> b0)
    xs = [jnp.where(blk, a, 0.0) for a in a_list]
    ts = [eye - x for x in xs]
    for _ in range(b0 - 1):
        xs = [_dot(x, x) for x in xs]
        ts = [t + _dot(t, x) for t, x in zip(ts, xs)]
    for lvl in range(b0, chunk_shift):
        m = ((r >> (lvl + 1)) == (c >> (lvl + 1))) & ((r >> lvl) != (c >> lvl))
        tmp = [_dot(t, jnp.where(m, a, 0.0)) for t, a in zip(ts, a_list)]
        ts = [t - _dot(x, t) for t, x in zip(ts, tmp)]
    return ts


def _gate_slabs(ba, arow, dtrow, chunk_shift):
    n = ba.shape[0]
    beta = jax.nn.sigmoid(ba)
    g = -jnp.exp(arow) * _softplus(ba + dtrow)
    r, c = _iota2(n, n)
    same = (r >> chunk_shift) == (c >> chunk_shift)
    ltri = jnp.where(same & (r >= c), 1.0, 0.0).astype(BF16)
    lall = jnp.where(same, 1.0, 0.0).astype(BF16)
    cs = _dot01(jnp.concatenate([ltri, lall], axis=0), g)
    return beta, cs[:n], cs[n:]


def _heads_prepare(ys, betas, gcs, egcs, chunk_shift):
    n = ys[0].shape[0]
    nb = n // GDN_BLOCK
    r, c = _iota2(GDN_BLOCK, GDN_BLOCK)
    same = (r >> chunk_shift) == (c >> chunk_shift)
    incl = same & (r >= c)
    strict = same & (r > c)
    qs, ks, a_list, rhs, decays = [], [], [], [], []
    gc_ts = [g.T for g in gcs]
    for (y, beta_s, gc_s, egc_s, gc_t), h in itertools.product(zip(ys, betas, gcs, egcs, gc_ts), range(N_HEADS)):
        q = _l2norm(y[:, h * HEAD_DIM:(h + 1) * HEAD_DIM]) * (HEAD_DIM ** -0.5)
        k = _l2norm(y[:, D_A + h * HEAD_DIM:D_A + (h + 1) * HEAD_DIM])
        v = y[:, 2 * D_A + h * HEAD_DIM:2 * D_A + (h + 1) * HEAD_DIM]
        la = LANE_A + h
        beta_c = beta_s[:, LANE_B + h:LANE_B + h + 1]
        kb = k * beta_c
        rhs_h = jnp.concatenate([v * beta_c, kb * egc_s[:, la:la + 1]], axis=1)
        for bi in range(nb):
            blk = slice(bi * GDN_BLOCK, (bi + 1) * GDN_BLOCK)
            decay = jnp.exp(jnp.where(incl, gc_s[blk, la:la + 1] - gc_t[la:la + 1, blk], -jnp.inf))
            a_list.append(jnp.where(strict, _dot_nt(kb[blk], k[blk]) * decay, 0.0))
            rhs.append(rhs_h[blk])
            decays.append(decay)
        qs.append(q)
        ks.append(k)
    ts = _unit_lower_inverse(a_list, r, c, chunk_shift)
    sols = [_dot(t, x) for t, x in zip(ts, rhs)]
    us, ws, qkds = [], [], []
    for h in range(len(qs)):
        sol = jnp.concatenate(sols[h * nb:(h + 1) * nb], axis=0) if nb > 1 else sols[h]
        us.append(sol[:, :HEAD_DIM])
        ws.append(sol[:, HEAD_DIM:])
        qkds.append([_dot_nt(qs[h][bi * GDN_BLOCK:(bi + 1) * GDN_BLOCK], ks[h][bi * GDN_BLOCK:(bi + 1) * GDN_BLOCK])
                     * decays[h * nb + bi] for bi in range(nb)])
    return qs, ks, us, ws, qkds


def _l2norm(x):
    return x * lax.rsqrt(jnp.sum(x * x, axis=-1, keepdims=True) + L2_EPS)


def _gated_rmsnorm(o, z, normw):
    o = o * lax.rsqrt(jnp.mean(o * o, axis=-1, keepdims=True) + RMS_EPS) * normw
    return o * _silu(z)


def _pool_out(s, cnt, p_g, poolw_g, pscale_g):
    d = s / cnt - p_g
    return _dot(d, poolw_g) * pscale_g


def _mixer_prompt_kernel(proj_ref, convw_ref, arow_ref, dtrow_ref, normw_ref, poolw_ref, pscale_ref,
                         mix_ref, sfin_ref, cc_ref, pc_ref, s_ref, *, lb, nseq):
    l = pl.program_id(1)
    n = lb
    csz = 1 << CHUNK_SHIFT

    @pl.when(l == 0)
    def _init():
        cc_ref[...] = jnp.zeros_like(cc_ref)
        pc_ref[...] = jnp.zeros_like(pc_ref)
        s_ref[...] = jnp.zeros_like(s_ref)

    cw = convw_ref[...]
    ys, betas, gcs, egcs, ekgs, egls = [], [], [], [], [], []
    for si in range(nseq):
        u = proj_ref[si, :, 0:C_QKV]
        ext = jnp.concatenate([cc_ref[si], u], axis=0)
        acc = ext * cw[CONV_W - 1:CONV_W, :]
        for d in range(1, CONV_W):
            acc = acc + pltpu.roll(ext, d, 0) * cw[CONV_W - 1 - d:CONV_W - d, :]
        cc_ref[si] = u[n - 8:n, :]
        ys.append(_silu(acc[8:, :]))
        beta_s, gc_s, gl_s = _gate_slabs(proj_ref[si, :, C_BA:C_TOT], arow_ref[...], dtrow_ref[...], CHUNK_SHIFT)
        betas.append(beta_s)
        gcs.append(gc_s)
        egcs.append(jnp.exp(gc_s))
        ekgs.append(jnp.exp(gl_s - gc_s))
        egls.append(jnp.exp(gl_s))

    qs, ks, us, ws, qkds = _heads_prepare(ys, betas, gcs, egcs, CHUNK_SHIFT)
    n_heads = nseq * N_HEADS

    zero = jnp.zeros((csz, 2 * HEAD_DIM), F32)
    n_chunks = n // csz
    qps, ops, kns, egl_reps = [], [], [], []
    for hh in range(n_heads):
        si, h = divmod(hh, N_HEADS)
        la = LANE_A + h
        wu = jnp.concatenate([ws[hh], us[hh]], axis=1)
        qw = jnp.concatenate([_dot(qkd, wu[bi * GDN_BLOCK:(bi + 1) * GDN_BLOCK])
                              for bi, qkd in enumerate(qkds[hh])], axis=0)
        qps.append(qs[hh] * egcs[si][:, la:la + 1] - qw[:, :HEAD_DIM])
        ops.append(qw[:, HEAD_DIM:])
        kg_t = (ks[hh] * ekgs[si][:, la:la + 1]).T
        kn = []
        for ci in range(n_chunks):
            rows = slice(ci * csz, (ci + 1) * csz)
            pair = slice((ci // 2) * 2 * csz, (ci // 2 + 1) * 2 * csz)
            half = jnp.concatenate([wu[rows], zero] if ci % 2 == 0 else [zero, wu[rows]], axis=0)
            kn.append(_dot(kg_t[:, pair], half))
        kns.append(kn)
        egl_reps.append(jnp.broadcast_to(egls[si][:, la:la + 1], (n, HEAD_DIM)))

    states = [s_ref[hh // N_HEADS, hh % N_HEADS] for hh in range(n_heads)]
    outs = [[] for _ in range(n_heads)]
    for ci in range(n_chunks):
        rows = slice(ci * csz, (ci + 1) * csz)
        for hh in range(n_heads):
            s = states[hh]
            outs[hh].append(_dot(qps[hh][rows], s) + ops[hh][rows])
            kn = kns[hh][ci]
            states[hh] = (s * egl_reps[hh][ci * csz:ci * csz + 1, :] - _dot(kn[:, :HEAD_DIM], s)) + kn[:, HEAD_DIM:]
    for hh in range(n_heads):
        si, h = divmod(hh, N_HEADS)
        hs = slice(h * HEAD_DIM, (h + 1) * HEAD_DIM)
        s_ref[si, h] = states[hh]
        o = jnp.concatenate(outs[hh], axis=0)
        mix_ref[si, :, hs] = _gated_rmsnorm(o, proj_ref[si, :, C_Z + h * HEAD_DIM:C_Z + (h + 1) * HEAD_DIM],
                                            normw_ref[...])

    sfin_ref[...] = s_ref[...]

    r, c = _iota2(n, n + 16)
    lag = r + 16 - c
    pos = l * n + lax.broadcasted_iota(jnp.int32, (n, 1), 0)
    for si in range(nseq):
        p = proj_ref[si, :, C_P:C_P + D_B]
        extp = jnp.concatenate([pc_ref[si], p], axis=0)
        pc_ref[si] = p[n - 16:n, :]
        for gi, w in enumerate(POOL_WINDOWS):
            gs = slice(gi * HEAD_DIM, (gi + 1) * HEAD_DIM)
            band = jnp.where((lag >= 0) & (lag < w), 1.0, 0.0).astype(BF16)
            cnt = jnp.minimum(pos + 1, w).astype(F32)
            mix_ref[si, :, D_A + gi * HEAD_DIM:D_A + (gi + 1) * HEAD_DIM] = _pool_out(
                _dot01(band, extp[:, gs]), cnt, p[:, gs], poolw_ref[gi], pscale_ref[:, gs])


def _mixer_prompt(proj, conv_w, arow, dtrow, normw, poolw, pscale, lb, nseq):
    b, seq, _ = proj.shape
    const2 = lambda i, j: (0, 0)
    return pl.pallas_call(
        functools.partial(_mixer_prompt_kernel, lb=lb, nseq=nseq),
        out_shape=(jax.ShapeDtypeStruct((b, seq, D_MODEL), F32),
                   jax.ShapeDtypeStruct((b, N_HEADS, HEAD_DIM, HEAD_DIM), F32)),
        grid=(b // nseq, seq // lb),
        in_specs=[pl.BlockSpec((nseq, lb, C_TOT), lambda i, j: (i, j, 0)),
                  pl.BlockSpec((CONV_W, C_QKV), const2),
                  pl.BlockSpec((1, 128), const2),
                  pl.BlockSpec((1, 128), const2),
                  pl.BlockSpec((1, HEAD_DIM), const2),
                  pl.BlockSpec((N_GROUPS, HEAD_DIM, HEAD_DIM), lambda i, j: (0, 0, 0)),
                  pl.BlockSpec((1, D_B), const2)],
        out_specs=(pl.BlockSpec((nseq, lb, D_MODEL), lambda i, j: (i, j, 0)),
                   pl.BlockSpec((nseq, N_HEADS, HEAD_DIM, HEAD_DIM), lambda i, j: (i, 0, 0, 0))),
        scratch_shapes=[pltpu.VMEM((nseq, 8, C_QKV), F32), pltpu.VMEM((nseq, 16, D_B), F32),
                        pltpu.VMEM((nseq, N_HEADS, HEAD_DIM, HEAD_DIM), F32)],
        compiler_params=pltpu.CompilerParams(dimension_semantics=("parallel", "arbitrary"),
                                             vmem_limit_bytes=VMEM_LIMIT),
        name="mixer_prompt",
    )(proj, conv_w, arow, dtrow, normw, poolw, pscale)


def _mixer_sample_kernel(proj_ref, cst_ref, pst_ref, sin_ref, convw_ref, arow_ref, dtrow_ref, normw_ref,
                         poolw_ref, pscale_ref, mix_ref, sout_ref, *, ns, seq, start):
    n = ns * seq
    sshift = seq.bit_length() - 1
    rowi = lax.broadcasted_iota(jnp.int32, (n, 1), 0)
    tpos = rowi & (seq - 1)

    u = proj_ref[:, 0:C_QKV]
    st = cst_ref[...]
    cw = convw_ref[...]
    acc = u * cw[CONV_W - 1:CONV_W, :]
    for d in range(1, CONV_W):
        term = jnp.where(tpos >= d, pltpu.roll(u, d, 0), pltpu.roll(st, n - seq + d, 0))
        acc = acc + term * cw[CONV_W - 1 - d:CONV_W - d, :]
    y = _silu(acc)

    beta_s, gc_s, gl_s = _gate_slabs(proj_ref[:, C_BA:C_TOT], arow_ref[...], dtrow_ref[...], sshift)
    egc_s = jnp.exp(gc_s)
    ekg_s = jnp.exp(gl_s - gc_s)
    egl_s = jnp.exp(gl_s)
    qs, ks, us, ws_, qkds = _heads_prepare([y], [beta_s], [gc_s], [egc_s], sshift)

    for h in range(N_HEADS):
        hs = slice(h * HEAD_DIM, (h + 1) * HEAD_DIM)
        la = LANE_A + h
        u_, w_, qkd = us[h], ws_[h], qkds[h][0]
        qg = qs[h] * egc_s[:, la:la + 1]
        kg_t = (ks[h] * ekg_s[:, la:la + 1]).T
        egl_rep = jnp.broadcast_to(egl_s[:, la:la + 1], (n, HEAD_DIM))
        ws_w, ws_q = [], []
        for si in range(ns):
            rows = slice(si * seq, (si + 1) * seq)
            ws = _dot(jnp.concatenate([w_[rows], qg[rows]], axis=0), sin_ref[si, h])
            ws_w.append(ws[:seq])
            ws_q.append(ws[seq:])
        vn = u_ - jnp.concatenate(ws_w, axis=0)
        o = jnp.concatenate(ws_q, axis=0) + _dot(qkd, vn)
        for si in range(ns):
            vmask = jnp.where((rowi >> sshift) == si, vn, 0.0)
            sout_ref[si, h] = sin_ref[si, h] * egl_rep[si * seq:si * seq + 1, :] + _dot(kg_t, vmask)
        mix_ref[:, hs] = _gated_rmsnorm(o, proj_ref[:, C_Z + h * HEAD_DIM:C_Z + (h + 1) * HEAD_DIM], normw_ref[...])

    p = proj_ref[:, C_P:C_P + D_B]
    pst = pst_ref[...]
    r, c = _iota2(n, n)
    band_new_base = ((r >> sshift) == (c >> sshift)) & (r >= c)
    r2, c2 = _iota2(n, ns * 16)
    same2 = (r2 >> sshift) == (c2 >> 4)
    t2 = r2 & (seq - 1)
    j2 = c2 & 15
    pos = start + tpos
    for gi, w in enumerate(POOL_WINDOWS):
        gs = slice(gi * HEAD_DIM, (gi + 1) * HEAD_DIM)
        band_new = jnp.where(band_new_base & ((r - c) < w), 1.0, 0.0).astype(BF16)
        band_st = jnp.where(same2 & (j2 >= 17 + t2 - w), 1.0, 0.0).astype(BF16)
        s = _dot01(band_new, p[:, gs]) + _dot01(band_st, pst[:, gs])
        cnt = jnp.minimum(pos + 1, w).astype(F32)
        mix_ref[:, D_A + gi * HEAD_DIM:D_A + (gi + 1) * HEAD_DIM] = _pool_out(
            s, cnt, p[:, gs], poolw_ref[gi], pscale_ref[:, gs])


def _mixer_sample(proj, cst, pst, sin, conv_w, arow, dtrow, normw, poolw, pscale, ns, seq, start):
    t = proj.shape[0]
    nb = t // seq
    n = ns * seq
    assert n == GDN_BLOCK and seq & (seq - 1) == 0 and seq >= CONV_W - 1
    const1 = lambda i: (0, 0)
    return pl.pallas_call(
        functools.partial(_mixer_sample_kernel, ns=ns, seq=seq, start=start),
        out_shape=(jax.ShapeDtypeStruct((t, D_MODEL), F32),
                   jax.ShapeDtypeStruct((nb, N_HEADS, HEAD_DIM, HEAD_DIM), F32)),
        grid=(nb // ns,),
        in_specs=[pl.BlockSpec((n, C_TOT), lambda i: (i, 0)),
                  pl.BlockSpec((n, C_QKV), lambda i: (i, 0)),
                  pl.BlockSpec((ns * 16, D_B), lambda i: (i, 0)),
                  pl.BlockSpec((ns, N_HEADS, HEAD_DIM, HEAD_DIM), lambda i: (i, 0, 0, 0)),
                  pl.BlockSpec((CONV_W, C_QKV), const1),
                  pl.BlockSpec((1, 128), const1),
                  pl.BlockSpec((1, 128), const1),
                  pl.BlockSpec((1, HEAD_DIM), const1),
                  pl.BlockSpec((N_GROUPS, HEAD_DIM, HEAD_DIM), lambda i: (0, 0, 0)),
                  pl.BlockSpec((1, D_B), const1)],
        out_specs=(pl.BlockSpec((n, D_MODEL), lambda i: (i, 0)),
                   pl.BlockSpec((ns, N_HEADS, HEAD_DIM, HEAD_DIM), lambda i: (i, 0, 0, 0))),
        compiler_params=pltpu.CompilerParams(dimension_semantics=("parallel",), vmem_limit_bytes=VMEM_LIMIT),
        name="mixer_sample",
    )(proj, cst, pst, sin, conv_w, arow, dtrow, normw, poolw, pscale)


ROW_SLAB = D_MODEL // 128


def _to_slabs(ref, x, n):
    for c in range(ROW_SLAB):
        ref[pl.ds(c, n, stride=ROW_SLAB), :] = x[:, c * 128:(c + 1) * 128]


def _from_slabs(ref, n):
    return jnp.concatenate([ref[pl.ds(c, n, stride=ROW_SLAB), :] for c in range(ROW_SLAB)], axis=1)


def _slab(ref, row):
    if isinstance(row, int):
        return ref.at[pl.ds(row * ROW_SLAB, ROW_SLAB)]
    return ref.at[pl.ds(pl.multiple_of(row * ROW_SLAB, ROW_SLAB), ROW_SLAB)]


def _layer_norm(x, g, b):
    mu = jnp.mean(x, axis=-1, keepdims=True)
    xc = x - mu
    var = jnp.mean(xc * xc, axis=-1, keepdims=True)
    return xc * lax.rsqrt(var + LN_EPS) * g + b


def _outproj_router_kernel(mixp_ref, xp_ref, mixs_ref, xs_ref, wout_ref, g1_ref, b1_ref, wrh_ref, wrl_ref, br_ref,
                           h_ref, sel_ref, gw_ref, cnt_ref, carry_ref, *, n_prompt_tiles):
    @pl.when(pl.program_id(0) == 0)
    def _init():
        carry_ref[...] = jnp.zeros_like(carry_ref)

    is_prompt = pl.program_id(0) < n_prompt_tiles
    tm = h_ref.shape[0] // ROW_SLAB
    n_parts = 2 if tm % 16 == 0 else 1
    pm = tm // n_parts
    parts = [slice(i * pm, (i + 1) * pm) for i in range(n_parts)]
    each = lambda fn, *lists: [fn(*args) for args in zip(*lists)]
    lane = lax.broadcasted_iota(jnp.int32, (pm, 128), 1)
    big = jnp.int32(1 << 20)
    neg = -jnp.inf
    lsum = lambda v: jnp.sum(v, axis=1, keepdims=True)
    lmax = lambda v: jnp.max(v, axis=1, keepdims=True)
    lmin = lambda v: jnp.min(v, axis=1, keepdims=True)

    mix = [jnp.where(is_prompt, mixp_ref[rs, :], mixs_ref[rs, :]) for rs in parts]
    x = [jnp.where(is_prompt, xp_ref[rs, :], xs_ref[rs, :]) for rs in parts]
    proj = each(lambda m: _dot(m, wout_ref[...]), mix)
    h = each(lambda xi, pi: _layer_norm(ALPHA * xi + pi, g1_ref[...], b1_ref[...]), x, proj)
    for i, hi in enumerate(h):
        for c in range(ROW_SLAB):
            h_ref[pl.ds(i * pm * ROW_SLAB + c, pm, stride=ROW_SLAB), :] = hi[:, c * 128:(c + 1) * 128]
    f = lambda a, b: jnp.dot(a, b, preferred_element_type=F32)
    split = each(_split3, h)
    logits = each(lambda s: f(s[0], wrh_ref[...]) + (f(s[1], wrh_ref[...]) + f(s[0], wrl_ref[...])) + br_ref[...],
                  split)
    gmask = (lane >= N_EXPERTS) & (lane < N_EXPERTS + N_GROUPS)
    lg = each(lambda l: jnp.where(gmask, l, neg), logits)
    gmax = each(lmax, lg)
    gidx = each(lambda l, m: lmin(jnp.where(l == m, lane - N_EXPERTS, big)), lg, gmax)
    pg = each(lambda l, m: 1.0 / lsum(jnp.where(gmask, jnp.exp(l - m), 0.0)), logits, gmax)
    emask = each(lambda g: (lane < N_EXPERTS) & ((lane >> 3) == g), gidx)
    le = each(lambda m, l: jnp.where(m, l, neg), emask, logits)
    v1 = each(lmax, le)
    i1 = each(lambda l, v, m: lmin(jnp.where((l == v) & m, lane, big)), le, v1, emask)
    emask2 = each(lambda m, i: m & (lane != i), emask, i1)
    le2 = each(lambda m, l: jnp.where(m, l, neg), emask2, logits)
    v2 = each(lmax, le2)
    i2 = each(lambda l, v, m: lmin(jnp.where((l == v) & m, lane, big)), le2, v2, emask2)
    e2 = each(lambda a, b: jnp.exp(a - b), v2, v1)
    for rs, e, g in zip(parts, e2, pg):
        den = 1.0 + e
        gw_ref[rs, :] = jnp.where(lane == 0, (1.0 / den) * g, jnp.where(lane == 1, (e / den) * g, 0.0))
    onehot = each(lambda a, b: jnp.where((lane == a) | (lane == b), 1.0, 0.0), i1, i2)
    r, c = _iota2(pm, pm)
    tri = jnp.where(r > c, 1.0, 0.0).astype(BF16)
    inside = each(lambda o: jnp.dot(tri, o.astype(BF16), preferred_element_type=F32), onehot)
    carry = carry_ref[...]
    for rs, o, ins, a, b in zip(parts, onehot, inside, i1, i2):
        before = ins + carry
        r1 = lsum(jnp.where(lane == a, before, 0.0)).astype(jnp.int32)
        r2 = lsum(jnp.where(lane == b, before, 0.0)).astype(jnp.int32)
        sel_ref[rs, :] = jnp.where(lane == 0, a, jnp.where(lane == 1, b, jnp.where(lane == 2, r1,
                                                                                    jnp.where(lane == 3, r2, 0))))
        carry = carry + jnp.sum(o, axis=0, keepdims=True)
    carry_ref[...] = carry
    cnt_ref[...] = carry.astype(jnp.int32)


def _outproj_router(mix_p, x_p, mix_s, x_s, wout, g1, b1, wrh, wrl, br, tm):
    tp, ts = x_p.shape[0], x_s.shape[0]
    t = tp + ts
    npt = tp // tm
    row = lambda i: (i, 0)
    prow = lambda i: (jnp.minimum(i, npt - 1), 0)
    srow = lambda i: (jnp.maximum(i - npt, 0), 0)
    const = lambda i: (0, 0)
    return pl.pallas_call(
        functools.partial(_outproj_router_kernel, n_prompt_tiles=npt),
        out_shape=(jax.ShapeDtypeStruct((t * ROW_SLAB, 128), F32), jax.ShapeDtypeStruct((t, 128), jnp.int32),
                   jax.ShapeDtypeStruct((t, 128), F32), jax.ShapeDtypeStruct((1, 128), jnp.int32)),
        grid=(t // tm,),
        in_specs=[pl.BlockSpec((tm, D_MODEL), prow), pl.BlockSpec((tm, D_MODEL), prow),
                  pl.BlockSpec((tm, D_MODEL), srow), pl.BlockSpec((tm, D_MODEL), srow),
                  pl.BlockSpec((D_MODEL, D_MODEL), const), pl.BlockSpec((1, D_MODEL), const),
                  pl.BlockSpec((1, D_MODEL), const), pl.BlockSpec((D_MODEL, 128), const),
                  pl.BlockSpec((D_MODEL, 128), const), pl.BlockSpec((1, 128), const)],
        out_specs=(pl.BlockSpec((tm * ROW_SLAB, 128), row), pl.BlockSpec((tm, 128), row),
                   pl.BlockSpec((tm, 128), row), pl.BlockSpec((1, 128), const)),
        scratch_shapes=[pltpu.VMEM((1, 128), F32)],
        compiler_params=pltpu.CompilerParams(dimension_semantics=("arbitrary",), vmem_limit_bytes=VMEM_LIMIT),
        name="outproj_router",
    )(mix_p, x_p, mix_s, x_s, wout, g1, b1, wrh, wrl, br)


EXPERT_TILE = 512
ROUTE_TILE = 512


def _dispatch_kernel(pad_start_ref, pad_cnt_ref, tail_ref, h_ref, pos_ref, xs_ref, zero_ref, sem, zsem, tsem,
                     *, td, tmx):
    @pl.when(pl.program_id(0) == 0)
    def _zero_unused_rows():
        zero_ref[...] = jnp.zeros_like(zero_ref)

        def zcopy(row, n_rows):
            return pltpu.make_async_copy(
                zero_ref.at[pl.ds(0, n_rows * ROW_SLAB)],
                xs_ref.at[pl.ds(pl.multiple_of(row * ROW_SLAB, ROW_SLAB), n_rows * ROW_SLAB)], zsem)

        tile_rows = min(td, tmx) * ROW_SLAB
        tcopy = lambda tile: pltpu.make_async_copy(
            zero_ref.at[pl.ds(0, tile_rows)],
            xs_ref.at[pl.ds(pl.multiple_of(tile * tile_rows, tile_rows), tile_rows)], tsem)

        def pad_pieces(e, start_not_wait):
            start = pad_start_ref[e]
            n = pad_cnt_ref[e]
            piece = tmx // 2
            while piece >= 1:
                @pl.when((n & piece) != 0)
                def _(piece=piece):
                    cp = zcopy(start + (n & ~(2 * piece - 1)), piece)
                    cp.start() if start_not_wait else cp.wait()
                piece //= 2

        def start_pads(e, carry):
            pad_pieces(e, True)
            return carry

        def wait_pads(e, carry):
            pad_pieces(e, False)
            return carry

        lax.fori_loop(0, N_EXPERTS, start_pads, 0)

        def tail_start(r, carry):
            tcopy(tail_ref[0] + r).start()
            return carry

        lax.fori_loop(0, tail_ref[1], tail_start, 0)
        lax.fori_loop(0, N_EXPERTS, wait_pads, 0)

        def tail_wait(r, carry):
            tcopy(0).wait()
            return carry

        lax.fori_loop(0, tail_ref[1], tail_wait, 0)

    copies = []
    for t in range(td):
        for k in range(2):
            cp = pltpu.make_async_copy(_slab(h_ref, t), _slab(xs_ref, pos_ref[0, 0, k * td + t]), sem)
            cp.start(priority=k)
            copies.append(cp)
    for cp in copies:
        cp.wait()


def _dispatch(h, pos3, pad_start, pad_cnt, tail, n_rows, td, tmx):
    t = h.shape[0] // ROW_SLAB
    assert td % min(td, tmx) == 0 and tmx % min(td, tmx) == 0 and tmx // 2 <= td
    return pl.pallas_call(
        functools.partial(_dispatch_kernel, td=td, tmx=tmx),
        out_shape=jax.ShapeDtypeStruct((n_rows * ROW_SLAB, 128), F32),
        grid_spec=pltpu.PrefetchScalarGridSpec(
            num_scalar_prefetch=3,
            grid=(t // td,),
            in_specs=[pl.BlockSpec((td * ROW_SLAB, 128), lambda i, *_: (i, 0)),
                      pl.BlockSpec((1, 1, 2 * td), lambda i, *_: (i, 0, 0), memory_space=pltpu.SMEM)],
            out_specs=pl.BlockSpec(memory_space=pl.ANY),
            scratch_shapes=[pltpu.VMEM((td * ROW_SLAB, 128), F32),
                            pltpu.SemaphoreType.DMA, pltpu.SemaphoreType.DMA, pltpu.SemaphoreType.DMA]),
        compiler_params=pltpu.CompilerParams(dimension_semantics=("arbitrary",), vmem_limit_bytes=VMEM_LIMIT),
        name="moe_dispatch",
    )(pad_start, pad_cnt, tail, h, pos3)


N_RING = 3


def _experts_kernel(tile_e_ref, tile_ok_ref, xs_ref, wg_ref, wu_ref, wd_ref, o_ref, ring_ref, sems, *, tmx, nt):
    j = pl.program_id(0)
    ok = tile_ok_ref[j] != 0
    tile_rows = tmx * ROW_SLAB

    def fetch(tile, slot):
        return pltpu.make_async_copy(xs_ref.at[pl.ds(pl.multiple_of(tile * tile_rows, tile_rows), tile_rows)],
                                     ring_ref.at[slot], sems.at[slot])

    @pl.when(j == 0)
    def _prologue():
        fetch(0, 0).start()

        @pl.when(tile_ok_ref[min(1, nt - 1)] != 0)
        def _():
            fetch(1, 1 % N_RING).start()

    ahead = j + (N_RING - 1)

    @pl.when(jnp.logical_and(ahead < nt, tile_ok_ref[jnp.minimum(ahead, nt - 1)] != 0))
    def _prefetch():
        fetch(ahead, lax.rem(ahead, N_RING)).start()

    @pl.when(ok)
    def _compute():
        slot = lax.rem(j, N_RING)
        fetch(j, slot).wait()
        x = _from_slabs(ring_ref.at[slot], tmx).astype(BF16)
        a = jnp.dot(x, wg_ref[0].astype(BF16), preferred_element_type=F32)
        b = jnp.dot(x, wu_ref[0].astype(BF16), preferred_element_type=F32)
        act = (_silu(a) * b).astype(BF16)
        _to_slabs(o_ref, jnp.dot(act, wd_ref[0].astype(BF16), preferred_element_type=F32), tmx)

    @pl.when(jnp.logical_not(ok))
    def _unused_tile():
        o_ref[...] = jnp.zeros_like(o_ref)


def _experts(xs, tile_e, tile_ok, wg, wu, wd, tmx):
    nt = tile_e.shape[0]
    wsel = lambda j, te, ok: (te[j], 0, 0)
    own = lambda j, te, ok: (j, 0)
    return pl.pallas_call(
        functools.partial(_experts_kernel, tmx=tmx, nt=nt),
        out_shape=jax.ShapeDtypeStruct(xs.shape, F32),
        grid_spec=pltpu.PrefetchScalarGridSpec(
            num_scalar_prefetch=2,
            grid=(nt,),
            in_specs=[pl.BlockSpec(memory_space=pl.ANY),
                      pl.BlockSpec((1, D_MODEL, D_EXPERT), wsel),
                      pl.BlockSpec((1, D_MODEL, D_EXPERT), wsel),
                      pl.BlockSpec((1, D_EXPERT, D_MODEL), wsel)],
            out_specs=pl.BlockSpec((tmx * ROW_SLAB, 128), own),
            scratch_shapes=[pltpu.VMEM((N_RING, tmx * ROW_SLAB, 128), F32), pltpu.SemaphoreType.DMA((N_RING,))]),
        compiler_params=pltpu.CompilerParams(dimension_semantics=("arbitrary",), vmem_limit_bytes=VMEM_LIMIT),
        name="moe_experts",
    )(tile_e, tile_ok, xs, wg, wu, wd)


def _combine_kernel(h_ref, gw_ref, pos_ref, npos_ref, g2_ref, b2_ref, os_ref, y_ref, stage_ref, sems,
                    *, td, n_steps):
    s = pl.program_id(0)

    def gather(idx_ref, half, start):
        for t in range(td):
            for k in range(2):
                cp = pltpu.make_async_copy(_slab(os_ref, idx_ref[0, 0, (half * 2 + k) * td + t]),
                                           _slab(stage_ref.at[half, k], t), sems.at[half])
                if start:
                    cp.start(priority=k)
                else:
                    cp.wait()

    def finish(half):
        rows = slice(half * td, (half + 1) * td)
        gw = gw_ref[rows, :]
        moe = (gw[:, 0:1] * _from_slabs(stage_ref.at[half, 0], td)
               + gw[:, 1:2] * _from_slabs(stage_ref.at[half, 1], td))
        h = jnp.concatenate([h_ref[pl.ds(half * td * ROW_SLAB + c, td, stride=ROW_SLAB), :]
                             for c in range(ROW_SLAB)], axis=1)
        y_ref[rows, :] = _layer_norm(ALPHA * h + moe, g2_ref[...], b2_ref[...])

    @pl.when(s == 0)
    def _prologue():
        gather(pos_ref, 0, True)

    gather(pos_ref, 1, True)
    gather(pos_ref, 0, False)
    finish(0)

    @pl.when(s + 1 < n_steps)
    def _next_step_first_tile():
        gather(npos_ref, 0, True)

    gather(pos_ref, 1, False)
    finish(1)


def _combine(h, gw, pos, g2, b2, os, td, first_token, n_tokens):
    step = 2 * td
    n_steps = n_tokens // step
    off = first_token // step
    pos4 = jnp.stack([q.reshape(-1, 2, td) for q in pos], axis=2).reshape(-1, 1, 2 * step)
    row = lambda i: (i + off, 0)
    const = lambda i: (0, 0)
    return pl.pallas_call(
        functools.partial(_combine_kernel, td=td, n_steps=n_steps),
        out_shape=jax.ShapeDtypeStruct((n_tokens, D_MODEL), F32),
        grid=(n_steps,),
        in_specs=[pl.BlockSpec((step * ROW_SLAB, 128), row), pl.BlockSpec((step, 128), row),
                  pl.BlockSpec((1, 1, 2 * step), lambda i: (i + off, 0, 0), memory_space=pltpu.SMEM),
                  pl.BlockSpec((1, 1, 2 * step), lambda i: (jnp.minimum(i + 1, n_steps - 1) + off, 0, 0),
                               memory_space=pltpu.SMEM),
                  pl.BlockSpec((1, D_MODEL), const), pl.BlockSpec((1, D_MODEL), const),
                  pl.BlockSpec(memory_space=pl.ANY)],
        out_specs=pl.BlockSpec((step, D_MODEL), lambda i: (i, 0)),
        scratch_shapes=[pltpu.VMEM((2, 2, td * ROW_SLAB, 128), F32), pltpu.SemaphoreType.DMA((2,))],
        compiler_params=pltpu.CompilerParams(dimension_semantics=("arbitrary",), vmem_limit_bytes=VMEM_LIMIT),
        name="moe_combine",
    )(h, gw, pos4, pos4, g2, b2, os)


def _route_plan(sel, cnt, t, tmx, td):
    i32 = jnp.int32
    counts = cnt[0, :N_EXPERTS]
    padded = ((counts + tmx - 1) // tmx) * tmx
    ex = jnp.arange(N_EXPERTS, dtype=i32)
    ends = jnp.sum(jnp.where(ex[None, :] <= ex[:, None], padded[None, :], 0), axis=1).astype(i32)
    offs = ends - padded
    pos = tuple((jnp.sum(jnp.where(sel[:, k, None] == ex, offs, 0), axis=-1) + sel[:, 2 + k]).astype(i32)
                for k in range(2))
    nt = 2 * t // tmx + N_EXPERTS
    n_used = ends[-1] // tmx
    tile = jnp.arange(nt, dtype=i32)
    tile_idx = jnp.minimum(tile, jnp.maximum(n_used - 1, 0))
    tile_e = jnp.minimum(jnp.sum((ends[None, :] <= (tile_idx * tmx)[:, None]).astype(i32), axis=1), N_EXPERTS - 1)
    tile_ok = (tile < n_used).astype(i32)
    pieces = tmx // min(td, tmx)
    tail = jnp.stack([n_used * pieces, (nt - n_used) * pieces]).astype(i32)
    return (pos, tile_e, tile_ok, (offs + counts).astype(i32), (padded - counts).astype(i32), tail, nt * tmx)


def _tile(t, want):
    tm = min(want, t)
    while t % tm:
        tm //= 2
    return tm


def _prep_weights(w_in, conv_w, a_log, dt_bias, gdn_norm_w, pool_w, pool_scale, w_out, ln1_g, ln1_b,
                  w_rg, b_rg, w_re, b_re, w_gate, w_up, w_down, ln2_g, ln2_b):
    col_b = 4 * D_A
    col_p = 4 * D_A + 2 * N_HEADS
    w_cat = jnp.concatenate([w_in[:, :col_b], w_in[:, col_p:], w_in[:, col_b:col_p],
                             jnp.zeros((D_MODEL, 128 - 2 * N_HEADS), w_in.dtype)], axis=1).astype(BF16)
    lane_pad = lambda v, off: jnp.zeros((1, 128), F32).at[0, off:off + v.shape[0]].set(v.astype(F32))
    w_r = jnp.concatenate([w_re, w_rg, jnp.zeros((D_MODEL, 128 - N_EXPERTS - N_GROUPS), F32)], axis=1)
    wrh = w_r.astype(BF16)
    wrl = (w_r - wrh.astype(F32)).astype(BF16)
    b_r = jnp.zeros((1, 128), F32).at[0, :N_EXPERTS].set(b_re).at[0, N_EXPERTS:N_EXPERTS + N_GROUPS].set(b_rg)
    return dict(
        w_cat=w_cat, conv_w=conv_w, arow=lane_pad(a_log, LANE_A), dtrow=lane_pad(dt_bias, LANE_A),
        normw=gdn_norm_w.reshape(1, HEAD_DIM), poolw=pool_w.astype(BF16), pscale=pool_scale.reshape(1, D_B),
        wout=w_out.astype(BF16), g1=ln1_g.reshape(1, D_MODEL), b1=ln1_b.reshape(1, D_MODEL),
        wrh=wrh, wrl=wrl, br=b_r,
        wg=w_gate.reshape(N_EXPERTS, D_MODEL, D_EXPERT), wu=w_up.reshape(N_EXPERTS, D_MODEL, D_EXPERT),
        wd=w_down.reshape(N_EXPERTS, D_EXPERT, D_MODEL),
        g2=ln2_g.reshape(1, D_MODEL), b2=ln2_b.reshape(1, D_MODEL))


def _post_mixer(mix_p, x_p, mix_s, x_s, p):
    tp, ts = x_p.shape[0], x_s.shape[0]
    t = tp + ts
    tm = math.gcd(_tile(tp, 1024), _tile(ts, 1024))
    h, sel, gw, cnt = _outproj_router(mix_p, x_p, mix_s, x_s, p["wout"], p["g1"], p["b1"], p["wrh"], p["wrl"],
                                      p["br"], tm)
    td = math.gcd(_tile(tp, ROUTE_TILE), _tile(ts, ROUTE_TILE))
    pos, tile_e, tile_ok, pad_start, pad_cnt, tail, n_rows = _route_plan(sel, cnt, t, EXPERT_TILE, td)
    pos3 = jnp.stack([q.reshape(t // td, td) for q in pos], axis=1).reshape(t // td, 1, 2 * td)
    xs = _dispatch(h, pos3, pad_start, pad_cnt, tail, n_rows, td, EXPERT_TILE)
    os = _experts(xs, tile_e, tile_ok, p["wg"], p["wu"], p["wd"], EXPERT_TILE)
    tc = td if (tp // td) % 2 == 0 and (ts // td) % 2 == 0 else td // 2
    y_p = _combine(h, gw, pos, p["g2"], p["b2"], os, tc, 0, tp)
    y_s = _combine(h, gw, pos, p["g2"], p["b2"], os, tc, tp, ts)
    return y_p, y_s


def _mix_prompt(x, p, lb=256):
    b, seq, _ = x.shape
    x2d = x.reshape(b * seq, D_MODEL)
    proj = _in_proj(x2d, p["w_cat"], _tile(b * seq, 1024)).reshape(b, seq, C_TOT)
    mix, s_fin = _mixer_prompt(proj, p["conv_w"], p["arow"], p["dtrow"], p["normw"], p["poolw"], p["pscale"],
                               min(lb, seq), 2 if b % 2 == 0 else 1)
    conv_new = proj[:, seq - (CONV_W - 1):, 0:C_QKV]
    pool_new = proj[:, seq - POOL_BUF:, C_P:C_P + D_B]
    return x2d, mix.reshape(b * seq, D_MODEL), s_fin, conv_new, pool_new


def _mix_sample(x, s0, conv0, pool0, start, p, ns=16):
    b, seq, _ = x.shape
    x2d = x.reshape(b * seq, D_MODEL)
    proj = _in_proj(x2d, p["w_cat"], _tile(b * seq, 1024))
    cst = jnp.pad(conv0, ((0, 0), (seq - (CONV_W - 1), 0), (0, 0))).reshape(b * seq, C_QKV)
    pst = jnp.pad(pool0, ((0, 0), (1, 0), (0, 0))).reshape(b * 16, D_B)
    mix, s_new = _mixer_sample(proj, cst, pst, s0, p["conv_w"], p["arow"], p["dtrow"], p["normw"], p["poolw"],
                               p["pscale"], min(ns, b), seq, start)
    proj3 = proj.reshape(b, seq, C_TOT)
    conv_new = proj3[:, seq - (CONV_W - 1):, 0:C_QKV]
    pool_new = jnp.concatenate([pool0[:, seq:, :], proj3[:, :, C_P:C_P + D_B]], axis=1)
    return x2d, mix, s_new, conv_new, pool_new


def _layer(x_prompt, x_sample, s0, conv0, pool0, start, p):
    xp2d, mix_p, dp, cp, pp = _mix_prompt(x_prompt, p)
    xs2d, mix_s, ds, cs, ps = _mix_sample(x_sample, s0, conv0, pool0, start, p)
    y_p, y_s = _post_mixer(mix_p, xp2d, mix_s, xs2d, p)
    return y_p.reshape(x_prompt.shape), y_s.reshape(x_sample.shape), (dp, cp, pp), (ds, cs, ps)


def kernel(x_prompt, x_sample, state_delta, state_conv, state_pool, w_in, conv_w, a_log, dt_bias, gdn_norm_w,
           pool_w, pool_scale, w_out, ln1_g, ln1_b, w_rg, b_rg, w_re, b_re, w_gate, w_up, w_down, ln2_g, ln2_b):
    depth = w_in.shape[0]
    past_len = 16384
    yp, ys = x_prompt, x_sample
    outs = [[] for _ in range(6)]
    for l in range(depth):
        p = _prep_weights(w_in[l], conv_w[l], a_log[l], dt_bias[l], gdn_norm_w[l], pool_w[l], pool_scale[l],
                          w_out[l], ln1_g[l], ln1_b[l], w_rg[l], b_rg[l], w_re[l], b_re[l], w_gate[l], w_up[l],
                          w_down[l], ln2_g[l], ln2_b[l])
        yp, ys, st_p, st_s = _layer(yp, ys, state_delta[l], state_conv[l], state_pool[l], past_len, p)
        for lst, v in zip(outs, st_p + st_s):
            lst.append(v)
    return (yp, ys) + tuple(jnp.stack(v) for v in outs)
```

```python
import functools
import itertools
import math

import jax
import jax.numpy as jnp
from jax import lax
from jax.experimental import pallas as pl
from jax.experimental.pallas import tpu as pltpu

F32 = jnp.float32
BF16 = jnp.bfloat16

D_MODEL = 1024
D_A = 512
D_B = 512
HEAD_DIM = 128
N_HEADS = 4
CONV_W = 4
CHUNK_SHIFT = 6
GDN_BLOCK = 128
POOL_WINDOWS = (2, 4, 8, 16)
POOL_BUF = 15
N_GROUPS = 4
E_PER_GROUP = 8
N_EXPERTS = N_GROUPS * E_PER_GROUP
D_EXPERT = 256
ALPHA = 2.0 ** 0.25
LN_EPS = 1e-5
RMS_EPS = 1e-6
L2_EPS = 1e-6

C_QKV = 3 * D_A
C_Z = 3 * D_A
C_P = 4 * D_A
C_BA = 4 * D_A + D_B
C_TOT = C_BA + 128
LANE_B = 0
LANE_A = N_HEADS

VMEM_LIMIT = 56 * 1024 * 1024


def _dot(a, b):
    return jnp.dot(a.astype(BF16), b.astype(BF16), preferred_element_type=F32)


def _dot_nt(a, b):
    return lax.dot_general(a.astype(BF16), b.astype(BF16), (((1,), (1,)), ((), ())), preferred_element_type=F32)


def _split3(x):
    hi = x.astype(BF16)
    r = x - hi.astype(F32)
    mid = r.astype(BF16)
    lo = (r - mid.astype(F32)).astype(BF16)
    return hi, mid, lo


def _dot01(m01, x):
    hi, mid, lo = _split3(x)
    f = lambda p: jnp.dot(m01, p, preferred_element_type=F32)
    return f(hi) + f(mid) + f(lo)


def _silu(x):
    return x * jax.nn.sigmoid(x)


def _softplus(x):
    return jnp.maximum(x, 0.0) + jnp.log1p(jnp.exp(-jnp.abs(x)))


def _iota2(n, m):
    return lax.broadcasted_iota(jnp.int32, (n, m), 0), lax.broadcasted_iota(jnp.int32, (n, m), 1)


def _proj_kernel(x_ref, w_ref, o_ref):
    o_ref[...] = jnp.dot(x_ref[...].astype(BF16), w_ref[...], preferred_element_type=F32)


def _in_proj(x2d, w_cat, tm):
    t = x2d.shape[0]
    return pl.pallas_call(
        _proj_kernel,
        out_shape=jax.ShapeDtypeStruct((t, C_TOT), F32),
        grid=(t // tm,),
        in_specs=[pl.BlockSpec((tm, D_MODEL), lambda i: (i, 0)),
                  pl.BlockSpec((D_MODEL, C_TOT), lambda i: (0, 0))],
        out_specs=pl.BlockSpec((tm, C_TOT), lambda i: (i, 0)),
        compiler_params=pltpu.CompilerParams(dimension_semantics=("parallel",), vmem_limit_bytes=VMEM_LIMIT),
        name="in_proj",
    )(x2d, w_cat)


def _unit_lower_inverse(a_list, r, c, chunk_shift):
    b0 = min(4, chunk_shift)
    eye = jnp.where(r == c, 1.0, 0.0).astype(F32)
    blk = (r >> b0) == (c >> b0)
    xs = [jnp.where(blk, a, 0.0) for a in a_list]
    ts = [eye - x for x in xs]
    for _ in range(b0 - 1):
        xs = [_dot(x, x) for x in xs]
        ts = [t + _dot(t, x) for t, x in zip(ts, xs)]
    for lvl in range(b0, chunk_shift):
        m = ((r >> (lvl + 1)) == (c >> (lvl + 1))) & ((r >> lvl) != (c >> lvl))
        tmp = [_dot(t, jnp.where(m, a, 0.0)) for t, a in zip(ts, a_list)]
        ts = [t - _dot(x, t) for t, x in zip(ts, tmp)]
    return ts


def _gate_slabs(ba, arow, dtrow, chunk_shift):
    n = ba.shape[0]
    beta = jax.nn.sigmoid(ba)
    g = -jnp.exp(arow) * _softplus(ba + dtrow)
    r, c = _iota2(n, n)
    same = (r >> chunk_shift) == (c >> chunk_shift)
    ltri = jnp.where(same & (r >= c), 1.0, 0.0).astype(BF16)
    lall = jnp.where(same, 1.0, 0.0).astype(BF16)
    cs = _dot01(jnp.concatenate([ltri, lall], axis=0), g)
    return beta, cs[:n], cs[n:]


def _heads_prepare(ys, betas, gcs, egcs, chunk_shift):
    n = ys[0].shape[0]
    nb = n // GDN_BLOCK
    r, c = _iota2(GDN_BLOCK, GDN_BLOCK)
    same = (r >> chunk_shift) == (c >> chunk_shift)
    incl = same & (r >= c)
    strict = same & (r > c)
    qs, ks, a_list, rhs, decays = [], [], [], [], []
    gc_ts = [g.T for g in gcs]
    for (y, beta_s, gc_s, egc_s, gc_t), h in itertools.product(zip(ys, betas, gcs, egcs, gc_ts), range(N_HEADS)):
        q = _l2norm(y[:, h * HEAD_DIM:(h + 1) * HEAD_DIM]) * (HEAD_DIM ** -0.5)
        k = _l2norm(y[:, D_A + h * HEAD_DIM:D_A + (h + 1) * HEAD_DIM])
        v = y[:, 2 * D_A + h * HEAD_DIM:2 * D_A + (h + 1) * HEAD_DIM]
        la = LANE_A + h
        beta_c = beta_s[:, LANE_B + h:LANE_B + h + 1]
        kb = k * beta_c
        rhs_h = jnp.concatenate([v * beta_c, kb * egc_s[:, la:la + 1]], axis=1)
        for bi in range(nb):
            blk = slice(bi * GDN_BLOCK, (bi + 1) * GDN_BLOCK)
            decay = jnp.exp(jnp.where(incl, gc_s[blk, la:la + 1] - gc_t[la:la + 1, blk], -jnp.inf))
            a_list.append(jnp.where(strict, _dot_nt(kb[blk], k[blk]) * decay, 0.0))
            rhs.append(rhs_h[blk])
            decays.append(decay)
        qs.append(q)
        ks.append(k)
    ts = _unit_lower_inverse(a_list, r, c, chunk_shift)
    sols = [_dot(t, x) for t, x in zip(ts, rhs)]
    us, ws, qkds = [], [], []
    for h in range(len(qs)):
        sol = jnp.concatenate(sols[h * nb:(h + 1) * nb], axis=0) if nb > 1 else sols[h]
        us.append(sol[:, :HEAD_DIM])
        ws.append(sol[:, HEAD_DIM:])
        qkds.append([_dot_nt(qs[h][bi * GDN_BLOCK:(bi + 1) * GDN_BLOCK], ks[h][bi * GDN_BLOCK:(bi + 1) * GDN_BLOCK])
                     * decays[h * nb + bi] for bi in range(nb)])
    return qs, ks, us, ws, qkds


def _l2norm(x):
    return x * lax.rsqrt(jnp.sum(x * x, axis=-1, keepdims=True) + L2_EPS)


def _gated_rmsnorm(o, z, normw):
    o = o * lax.rsqrt(jnp.mean(o * o, axis=-1, keepdims=True) + RMS_EPS) * normw
    return o * _silu(z)


def _pool_out(s, cnt, p_g, poolw_g, pscale_g):
    d = s / cnt - p_g
    return _dot(d, poolw_g) * pscale_g


def _mixer_prompt_kernel(proj_ref, convw_ref, arow_ref, dtrow_ref, normw_ref, poolw_ref, pscale_ref,
                         mix_ref, sfin_ref, cc_ref, pc_ref, s_ref, *, lb, nseq):
    l = pl.program_id(1)
    n = lb
    csz = 1 << CHUNK_SHIFT

    @pl.when(l == 0)
    def _init():
        cc_ref[...] = jnp.zeros_like(cc_ref)
        pc_ref[...] = jnp.zeros_like(pc_ref)
        s_ref[...] = jnp.zeros_like(s_ref)

    cw = convw_ref[...]
    ys, betas, gcs, egcs, ekgs, egls = [], [], [], [], [], []
    for si in range(nseq):
        u = proj_ref[si, :, 0:C_QKV]
        ext = jnp.concatenate([cc_ref[si], u], axis=0)
        acc = ext * cw[CONV_W - 1:CONV_W, :]
        for d in range(1, CONV_W):
            acc = acc + pltpu.roll(ext, d, 0) * cw[CONV_W - 1 - d:CONV_W - d, :]
        cc_ref[si] = u[n - 8:n, :]
        ys.append(_silu(acc[8:, :]))
        beta_s, gc_s, gl_s = _gate_slabs(proj_ref[si, :, C_BA:C_TOT], arow_ref[...], dtrow_ref[...], CHUNK_SHIFT)
        betas.append(beta_s)
        gcs.append(gc_s)
        egcs.append(jnp.exp(gc_s))
        ekgs.append(jnp.exp(gl_s - gc_s))
        egls.append(jnp.exp(gl_s))

    qs, ks, us, ws, qkds = _heads_prepare(ys, betas, gcs, egcs, CHUNK_SHIFT)
    n_heads = nseq * N_HEADS

    zero = jnp.zeros((csz, 2 * HEAD_DIM), F32)
    n_chunks = n // csz
    qps, ops, kns, egl_reps = [], [], [], []
    for hh in range(n_heads):
        si, h = divmod(hh, N_HEADS)
        la = LANE_A + h
        wu = jnp.concatenate([ws[hh], us[hh]], axis=1)
        qw = jnp.concatenate([_dot(qkd, wu[bi * GDN_BLOCK:(bi + 1) * GDN_BLOCK])
                              for bi, qkd in enumerate(qkds[hh])], axis=0)
        qps.append(qs[hh] * egcs[si][:, la:la + 1] - qw[:, :HEAD_DIM])
        ops.append(qw[:, HEAD_DIM:])
        kg_t = (ks[hh] * ekgs[si][:, la:la + 1]).T
        kn = []
        for ci in range(n_chunks):
            rows = slice(ci * csz, (ci + 1) * csz)
            pair = slice((ci // 2) * 2 * csz, (ci // 2 + 1) * 2 * csz)
            half = jnp.concatenate([wu[rows], zero] if ci % 2 == 0 else [zero, wu[rows]], axis=0)
            kn.append(_dot(kg_t[:, pair], half))
        kns.append(kn)
        egl_reps.append(jnp.broadcast_to(egls[si][:, la:la + 1], (n, HEAD_DIM)))

    states = [s_ref[hh // N_HEADS, hh % N_HEADS] for hh in range(n_heads)]
    outs = [[] for _ in range(n_heads)]
    for ci in range(n_chunks):
        rows = slice(ci * csz, (ci + 1) * csz)
        for hh in range(n_heads):
            s = states[hh]
            outs[hh].append(_dot(qps[hh][rows], s) + ops[hh][rows])
            kn = kns[hh][ci]
            states[hh] = (s * egl_reps[hh][ci * csz:ci * csz + 1, :] - _dot(kn[:, :HEAD_DIM], s)) + kn[:, HEAD_DIM:]
    for hh in range(n_heads):
        si, h = divmod(hh, N_HEADS)
        hs = slice(h * HEAD_DIM, (h + 1) * HEAD_DIM)
        s_ref[si, h] = states[hh]
        o = jnp.concatenate(outs[hh], axis=0)
        mix_ref[si, :, hs] = _gated_rmsnorm(o, proj_ref[si, :, C_Z + h * HEAD_DIM:C_Z + (h + 1) * HEAD_DIM],
                                            normw_ref[...])

    sfin_ref[...] = s_ref[...]

    r, c = _iota2(n, n + 16)
    lag = r + 16 - c
    pos = l * n + lax.broadcasted_iota(jnp.int32, (n, 1), 0)
    for si in range(nseq):
        p = proj_ref[si, :, C_P:C_P + D_B]
        extp = jnp.concatenate([pc_ref[si], p], axis=0)
        pc_ref[si] = p[n - 16:n, :]
        for gi, w in enumerate(POOL_WINDOWS):
            gs = slice(gi * HEAD_DIM, (gi + 1) * HEAD_DIM)
            band = jnp.where((lag >= 0) & (lag < w), 1.0, 0.0).astype(BF16)
            cnt = jnp.minimum(pos + 1, w).astype(F32)
            mix_ref[si, :, D_A + gi * HEAD_DIM:D_A + (gi + 1) * HEAD_DIM] = _pool_out(
                _dot01(band, extp[:, gs]), cnt, p[:, gs], poolw_ref[gi], pscale_ref[:, gs])


def _mixer_prompt(proj, conv_w, arow, dtrow, normw, poolw, pscale, lb, nseq):
    b, seq, _ = proj.shape
    const2 = lambda i, j: (0, 0)
    return pl.pallas_call(
        functools.partial(_mixer_prompt_kernel, lb=lb, nseq=nseq),
        out_shape=(jax.ShapeDtypeStruct((b, seq, D_MODEL), F32),
                   jax.ShapeDtypeStruct((b, N_HEADS, HEAD_DIM, HEAD_DIM), F32)),
        grid=(b // nseq, seq // lb),
        in_specs=[pl.BlockSpec((nseq, lb, C_TOT), lambda i, j: (i, j, 0)),
                  pl.BlockSpec((CONV_W, C_QKV), const2),
                  pl.BlockSpec((1, 128), const2),
                  pl.BlockSpec((1, 128), const2),
                  pl.BlockSpec((1, HEAD_DIM), const2),
                  pl.BlockSpec((N_GROUPS, HEAD_DIM, HEAD_DIM), lambda i, j: (0, 0, 0)),
                  pl.BlockSpec((1, D_B), const2)],
        out_specs=(pl.BlockSpec((nseq, lb, D_MODEL), lambda i, j: (i, j, 0)),
                   pl.BlockSpec((nseq, N_HEADS, HEAD_DIM, HEAD_DIM), lambda i, j: (i, 0, 0, 0))),
        scratch_shapes=[pltpu.VMEM((nseq, 8, C_QKV), F32), pltpu.VMEM((nseq, 16, D_B), F32),
                        pltpu.VMEM((nseq, N_HEADS, HEAD_DIM, HEAD_DIM), F32)],
        compiler_params=pltpu.CompilerParams(dimension_semantics=("parallel", "arbitrary"),
                                             vmem_limit_bytes=VMEM_LIMIT),
        name="mixer_prompt",
    )(proj, conv_w, arow, dtrow, normw, poolw, pscale)


def _mixer_sample_kernel(proj_ref, cst_ref, pst_ref, sin_ref, convw_ref, arow_ref, dtrow_ref, normw_ref,
                         poolw_ref, pscale_ref, mix_ref, sout_ref, *, ns, seq, start):
    n = ns * seq
    sshift = seq.bit_length() - 1
    rowi = lax.broadcasted_iota(jnp.int32, (n, 1), 0)
    tpos = rowi & (seq - 1)

    u = proj_ref[:, 0:C_QKV]
    st = cst_ref[...]
    cw = convw_ref[...]
    acc = u * cw[CONV_W - 1:CONV_W, :]
    for d in range(1, CONV_W):
        term = jnp.where(tpos >= d, pltpu.roll(u, d, 0), pltpu.roll(st, n - seq + d, 0))
        acc = acc + term * cw[CONV_W - 1 - d:CONV_W - d, :]
    y = _silu(acc)

    beta_s, gc_s, gl_s = _gate_slabs(proj_ref[:, C_BA:C_TOT], arow_ref[...], dtrow_ref[...], sshift)
    egc_s = jnp.exp(gc_s)
    ekg_s = jnp.exp(gl_s - gc_s)
    egl_s = jnp.exp(gl_s)
    qs, ks, us, ws_, qkds = _heads_prepare([y], [beta_s], [gc_s], [egc_s], sshift)

    for h in range(N_HEADS):
        hs = slice(h * HEAD_DIM, (h + 1) * HEAD_DIM)
        la = LANE_A + h
        u_, w_, qkd = us[h], ws_[h], qkds[h][0]
        qg = qs[h] * egc_s[:, la:la + 1]
        kg_t = (ks[h] * ekg_s[:, la:la + 1]).T
        egl_rep = jnp.broadcast_to(egl_s[:, la:la + 1], (n, HEAD_DIM))
        ws_w, ws_q = [], []
        for si in range(ns):
            rows = slice(si * seq, (si + 1) * seq)
            ws = _dot(jnp.concatenate([w_[rows], qg[rows]], axis=0), sin_ref[si, h])
            ws_w.append(ws[:seq])
            ws_q.append(ws[seq:])
        vn = u_ - jnp.concatenate(ws_w, axis=0)
        o = jnp.concatenate(ws_q, axis=0) + _dot(qkd, vn)
        for si in range(ns):
            vmask = jnp.where((rowi >> sshift) == si, vn, 0.0)
            sout_ref[si, h] = sin_ref[si, h] * egl_rep[si * seq:si * seq + 1, :] + _dot(kg_t, vmask)
        mix_ref[:, hs] = _gated_rmsnorm(o, proj_ref[:, C_Z + h * HEAD_DIM:C_Z + (h + 1) * HEAD_DIM], normw_ref[...])

    p = proj_ref[:, C_P:C_P + D_B]
    pst = pst_ref[...]
    r, c = _iota2(n, n)
    band_new_base = ((r >> sshift) == (c >> sshift)) & (r >= c)
    r2, c2 = _iota2(n, ns * 16)
    same2 = (r2 >> sshift) == (c2 >> 4)
    t2 = r2 & (seq - 1)
    j2 = c2 & 15
    pos = start + tpos
    for gi, w in enumerate(POOL_WINDOWS):
        gs = slice(gi * HEAD_DIM, (gi + 1) * HEAD_DIM)
        band_new = jnp.where(band_new_base & ((r - c) < w), 1.0, 0.0).astype(BF16)
        band_st = jnp.where(same2 & (j2 >= 17 + t2 - w), 1.0, 0.0).astype(BF16)
        s = _dot01(band_new, p[:, gs]) + _dot01(band_st, pst[:, gs])
        cnt = jnp.minimum(pos + 1, w).astype(F32)
        mix_ref[:, D_A + gi * HEAD_DIM:D_A + (gi + 1) * HEAD_DIM] = _pool_out(
            s, cnt, p[:, gs], poolw_ref[gi], pscale_ref[:, gs])


def _mixer_sample(proj, cst, pst, sin, conv_w, arow, dtrow, normw, poolw, pscale, ns, seq, start):
    t = proj.shape[0]
    nb = t // seq
    n = ns * seq
    assert n == GDN_BLOCK and seq & (seq - 1) == 0 and seq >= CONV_W - 1
    const1 = lambda i: (0, 0)
    return pl.pallas_call(
        functools.partial(_mixer_sample_kernel, ns=ns, seq=seq, start=start),
        out_shape=(jax.ShapeDtypeStruct((t, D_MODEL), F32),
                   jax.ShapeDtypeStruct((nb, N_HEADS, HEAD_DIM, HEAD_DIM), F32)),
        grid=(nb // ns,),
        in_specs=[pl.BlockSpec((n, C_TOT), lambda i: (i, 0)),
                  pl.BlockSpec((n, C_QKV), lambda i: (i, 0)),
                  pl.BlockSpec((ns * 16, D_B), lambda i: (i, 0)),
                  pl.BlockSpec((ns, N_HEADS, HEAD_DIM, HEAD_DIM), lambda i: (i, 0, 0, 0)),
                  pl.BlockSpec((CONV_W, C_QKV), const1),
                  pl.BlockSpec((1, 128), const1),
                  pl.BlockSpec((1, 128), const1),
                  pl.BlockSpec((1, HEAD_DIM), const1),
                  pl.BlockSpec((N_GROUPS, HEAD_DIM, HEAD_DIM), lambda i: (0, 0, 0)),
                  pl.BlockSpec((1, D_B), const1)],
        out_specs=(pl.BlockSpec((n, D_MODEL), lambda i: (i, 0)),
                   pl.BlockSpec((ns, N_HEADS, HEAD_DIM, HEAD_DIM), lambda i: (i, 0, 0, 0))),
        compiler_params=pltpu.CompilerParams(dimension_semantics=("parallel",), vmem_limit_bytes=VMEM_LIMIT),
        name="mixer_sample",
    )(proj, cst, pst, sin, conv_w, arow, dtrow, normw, poolw, pscale)


ROW_SLAB = D_MODEL // 128


def _to_slabs(ref, x, n):
    for c in range(ROW_SLAB):
        ref[pl.ds(c, n, stride=ROW_SLAB), :] = x[:, c * 128:(c + 1) * 128]


def _from_slabs(ref, n):
    return jnp.concatenate([ref[pl.ds(c, n, stride=ROW_SLAB), :] for c in range(ROW_SLAB)], axis=1)


def _slab(ref, row):
    if isinstance(row, int):
        return ref.at[pl.ds(row * ROW_SLAB, ROW_SLAB)]
    return ref.at[pl.ds(pl.multiple_of(row * ROW_SLAB, ROW_SLAB), ROW_SLAB)]


def _layer_norm(x, g, b):
    mu = jnp.mean(x, axis=-1, keepdims=True)
    xc = x - mu
    var = jnp.mean(xc * xc, axis=-1, keepdims=True)
    return xc * lax.rsqrt(var + LN_EPS) * g + b


def _outproj_router_kernel(mixp_ref, xp_ref, mixs_ref, xs_ref, wout_ref, g1_ref, b1_ref, wrh_ref, wrl_ref, br_ref,
                           h_ref, sel_ref, gw_ref, cnt_ref, carry_ref, *, n_prompt_tiles):
    @pl.when(pl.program_id(0) == 0)
    def _init():
        carry_ref[...] = jnp.zeros_like(carry_ref)

    is_prompt = pl.program_id(0) < n_prompt_tiles
    tm = h_ref.shape[0] // ROW_SLAB
    n_parts = 2 if tm % 16 == 0 else 1
    pm = tm // n_parts
    parts = [slice(i * pm, (i + 1) * pm) for i in range(n_parts)]
    each = lambda fn, *lists: [fn(*args) for args in zip(*lists)]
    lane = lax.broadcasted_iota(jnp.int32, (pm, 128), 1)
    big = jnp.int32(1 << 20)
    neg = -jnp.inf
    lsum = lambda v: jnp.sum(v, axis=1, keepdims=True)
    lmax = lambda v: jnp.max(v, axis=1, keepdims=True)
    lmin = lambda v: jnp.min(v, axis=1, keepdims=True)

    mix = [jnp.where(is_prompt, mixp_ref[rs, :], mixs_ref[rs, :]) for rs in parts]
    x = [jnp.where(is_prompt, xp_ref[rs, :], xs_ref[rs, :]) for rs in parts]
    proj = each(lambda m: _dot(m, wout_ref[...]), mix)
    h = each(lambda xi, pi: _layer_norm(ALPHA * xi + pi, g1_ref[...], b1_ref[...]), x, proj)
    for i, hi in enumerate(h):
        for c in range(ROW_SLAB):
            h_ref[pl.ds(i * pm * ROW_SLAB + c, pm, stride=ROW_SLAB), :] = hi[:, c * 128:(c + 1) * 128]
    f = lambda a, b: jnp.dot(a, b, preferred_element_type=F32)
    split = each(_split3, h)
    logits = each(lambda s: f(s[0], wrh_ref[...]) + (f(s[1], wrh_ref[...]) + f(s[0], wrl_ref[...])) + br_ref[...],
                  split)
    gmask = (lane >= N_EXPERTS) & (lane < N_EXPERTS + N_GROUPS)
    lg = each(lambda l: jnp.where(gmask, l, neg), logits)
    gmax = each(lmax, lg)
    gidx = each(lambda l, m: lmin(jnp.where(l == m, lane - N_EXPERTS, big)), lg, gmax)
    pg = each(lambda l, m: 1.0 / lsum(jnp.where(gmask, jnp.exp(l - m), 0.0)), logits, gmax)
    emask = each(lambda g: (lane < N_EXPERTS) & ((lane >> 3) == g), gidx)
    le = each(lambda m, l: jnp.where(m, l, neg), emask, logits)
    v1 = each(lmax, le)
    i1 = each(lambda l, v, m: lmin(jnp.where((l == v) & m, lane, big)), le, v1, emask)
    emask2 = each(lambda m, i: m & (lane != i), emask, i1)
    le2 = each(lambda m, l: jnp.where(m, l, neg), emask2, logits)
    v2 = each(lmax, le2)
    i2 = each(lambda l, v, m: lmin(jnp.where((l == v) & m, lane, big)), le2, v2, emask2)
    e2 = each(lambda a, b: jnp.exp(a - b), v2, v1)
    for rs, e, g in zip(parts, e2, pg):
        den = 1.0 + e
        gw_ref[rs, :] = jnp.where(lane == 0, (1.0 / den) * g, jnp.where(lane == 1, (e / den) * g, 0.0))
    onehot = each(lambda a, b: jnp.where((lane == a) | (lane == b), 1.0, 0.0), i1, i2)
    r, c = _iota2(pm, pm)
    tri = jnp.where(r > c, 1.0, 0.0).astype(BF16)
    inside = each(lambda o: jnp.dot(tri, o.astype(BF16), preferred_element_type=F32), onehot)
    carry = carry_ref[...]
    for rs, o, ins, a, b in zip(parts, onehot, inside, i1, i2):
        before = ins + carry
        r1 = lsum(jnp.where(lane == a, before, 0.0)).astype(jnp.int32)
        r2 = lsum(jnp.where(lane == b, before, 0.0)).astype(jnp.int32)
        sel_ref[rs, :] = jnp.where(lane == 0, a, jnp.where(lane == 1, b, jnp.where(lane == 2, r1,
                                                                                    jnp.where(lane == 3, r2, 0))))
        carry = carry + jnp.sum(o, axis=0, keepdims=True)
    carry_ref[...] = carry
    cnt_ref[...] = carry.astype(jnp.int32)


def _outproj_router(mix_p, x_p, mix_s, x_s, wout, g1, b1, wrh, wrl, br, tm):
    tp, ts = x_p.shape[0], x_s.shape[0]
    t = tp + ts
    npt = tp // tm
    row = lambda i: (i, 0)
    prow = lambda i: (jnp.minimum(i, npt - 1), 0)
    srow = lambda i: (jnp.maximum(i - npt, 0), 0)
    const = lambda i: (0, 0)
    return pl.pallas_call(
        functools.partial(_outproj_router_kernel, n_prompt_tiles=npt),
        out_shape=(jax.ShapeDtypeStruct((t * ROW_SLAB, 128), F32), jax.ShapeDtypeStruct((t, 128), jnp.int32),
                   jax.ShapeDtypeStruct((t, 128), F32), jax.ShapeDtypeStruct((1, 128), jnp.int32)),
        grid=(t // tm,),
        in_specs=[pl.BlockSpec((tm, D_MODEL), prow), pl.BlockSpec((tm, D_MODEL), prow),
                  pl.BlockSpec((tm, D_MODEL), srow), pl.BlockSpec((tm, D_MODEL), srow),
                  pl.BlockSpec((D_MODEL, D_MODEL), const), pl.BlockSpec((1, D_MODEL), const),
                  pl.BlockSpec((1, D_MODEL), const), pl.BlockSpec((D_MODEL, 128), const),
                  pl.BlockSpec((D_MODEL, 128), const), pl.BlockSpec((1, 128), const)],
        out_specs=(pl.BlockSpec((tm * ROW_SLAB, 128), row), pl.BlockSpec((tm, 128), row),
                   pl.BlockSpec((tm, 128), row), pl.BlockSpec((1, 128), const)),
        scratch_shapes=[pltpu.VMEM((1, 128), F32)],
        compiler_params=pltpu.CompilerParams(dimension_semantics=("arbitrary",), vmem_limit_bytes=VMEM_LIMIT),
        name="outproj_router",
    )(mix_p, x_p, mix_s, x_s, wout, g1, b1, wrh, wrl, br)


EXPERT_TILE = 512
ROUTE_TILE = 1024


def _dispatch_kernel(pad_start_ref, pad_cnt_ref, tail_ref, h_ref, pos_ref, xs_ref, zero_ref, sem, zsem, tsem,
                     *, td, tmx):
    @pl.when(pl.program_id(0) == 0)
    def _zero_unused_rows():
        zero_ref[...] = jnp.zeros_like(zero_ref)

        def zcopy(row, n_rows):
            return pltpu.make_async_copy(
                zero_ref.at[pl.ds(0, n_rows * ROW_SLAB)],
                xs_ref.at[pl.ds(pl.multiple_of(row * ROW_SLAB, ROW_SLAB), n_rows * ROW_SLAB)], zsem)

        tile_rows = min(td, tmx) * ROW_SLAB
        tcopy = lambda tile: pltpu.make_async_copy(
            zero_ref.at[pl.ds(0, tile_rows)],
            xs_ref.at[pl.ds(pl.multiple_of(tile * tile_rows, tile_rows), tile_rows)], tsem)

        def pad_pieces(e, start_not_wait):
            start = pad_start_ref[e]
            n = pad_cnt_ref[e]
            piece = tmx // 2
            while piece >= 1:
                @pl.when((n & piece) != 0)
                def _(piece=piece):
                    cp = zcopy(start + (n & ~(2 * piece - 1)), piece)
                    cp.start() if start_not_wait else cp.wait()
                piece //= 2

        def start_pads(e, carry):
            pad_pieces(e, True)
            return carry

        def wait_pads(e, carry):
            pad_pieces(e, False)
            return carry

        lax.fori_loop(0, N_EXPERTS, start_pads, 0)

        def tail_start(r, carry):
            tcopy(tail_ref[0] + r).start()
            return carry

        lax.fori_loop(0, tail_ref[1], tail_start, 0)
        lax.fori_loop(0, N_EXPERTS, wait_pads, 0)

        def tail_wait(r, carry):
            tcopy(0).wait()
            return carry

        lax.fori_loop(0, tail_ref[1], tail_wait, 0)

    copies = []
    for t in range(td):
        for k in range(2):
            cp = pltpu.make_async_copy(_slab(h_ref, t), _slab(xs_ref, pos_ref[0, 0, k * td + t]), sem)
            cp.start(priority=k)
            copies.append(cp)
    for cp in copies:
        cp.wait()


def _dispatch(h, pos3, pad_start, pad_cnt, tail, n_rows, td, tmx):
    t = h.shape[0] // ROW_SLAB
    assert td % min(td, tmx) == 0 and tmx % min(td, tmx) == 0 and tmx // 2 <= td
    return pl.pallas_call(
        functools.partial(_dispatch_kernel, td=td, tmx=tmx),
        out_shape=jax.ShapeDtypeStruct((n_rows * ROW_SLAB, 128), F32),
        grid_spec=pltpu.PrefetchScalarGridSpec(
            num_scalar_prefetch=3,
            grid=(t // td,),
            in_specs=[pl.BlockSpec((td * ROW_SLAB, 128), lambda i, *_: (i, 0)),
                      pl.BlockSpec((1, 1, 2 * td), lambda i, *_: (i, 0, 0), memory_space=pltpu.SMEM)],
            out_specs=pl.BlockSpec(memory_space=pl.ANY),
            scratch_shapes=[pltpu.VMEM((td * ROW_SLAB, 128), F32),
                            pltpu.SemaphoreType.DMA, pltpu.SemaphoreType.DMA, pltpu.SemaphoreType.DMA]),
        compiler_params=pltpu.CompilerParams(dimension_semantics=("arbitrary",), vmem_limit_bytes=VMEM_LIMIT),
        name="moe_dispatch",
    )(pad_start, pad_cnt, tail, h, pos3)


N_RING = 3


def _experts_kernel(tile_e_ref, tile_ok_ref, xs_ref, wg_ref, wu_ref, wd_ref, o_ref, ring_ref, sems, *, tmx, nt):
    j = pl.program_id(0)
    ok = tile_ok_ref[j] != 0
    tile_rows = tmx * ROW_SLAB

    def fetch(tile, slot):
        return pltpu.make_async_copy(xs_ref.at[pl.ds(pl.multiple_of(tile * tile_rows, tile_rows), tile_rows)],
                                     ring_ref.at[slot], sems.at[slot])

    @pl.when(j == 0)
    def _prologue():
        fetch(0, 0).start()

        @pl.when(tile_ok_ref[min(1, nt - 1)] != 0)
        def _():
            fetch(1, 1 % N_RING).start()

    ahead = j + (N_RING - 1)

    @pl.when(jnp.logical_and(ahead < nt, tile_ok_ref[jnp.minimum(ahead, nt - 1)] != 0))
    def _prefetch():
        fetch(ahead, lax.rem(ahead, N_RING)).start()

    @pl.when(ok)
    def _compute():
        slot = lax.rem(j, N_RING)
        fetch(j, slot).wait()
        x = _from_slabs(ring_ref.at[slot], tmx).astype(BF16)
        a = jnp.dot(x, wg_ref[0].astype(BF16), preferred_element_type=F32)
        b = jnp.dot(x, wu_ref[0].astype(BF16), preferred_element_type=F32)
        act = (_silu(a) * b).astype(BF16)
        _to_slabs(o_ref, jnp.dot(act, wd_ref[0].astype(BF16), preferred_element_type=F32), tmx)

    @pl.when(jnp.logical_not(ok))
    def _unused_tile():
        o_ref[...] = jnp.zeros_like(o_ref)


def _experts(xs, tile_e, tile_ok, wg, wu, wd, tmx):
    nt = tile_e.shape[0]
    wsel = lambda j, te, ok: (te[j], 0, 0)
    own = lambda j, te, ok: (j, 0)
    return pl.pallas_call(
        functools.partial(_experts_kernel, tmx=tmx, nt=nt),
        out_shape=jax.ShapeDtypeStruct(xs.shape, F32),
        grid_spec=pltpu.PrefetchScalarGridSpec(
            num_scalar_prefetch=2,
            grid=(nt,),
            in_specs=[pl.BlockSpec(memory_space=pl.ANY),
                      pl.BlockSpec((1, D_MODEL, D_EXPERT), wsel),
                      pl.BlockSpec((1, D_MODEL, D_EXPERT), wsel),
                      pl.BlockSpec((1, D_EXPERT, D_MODEL), wsel)],
            out_specs=pl.BlockSpec((tmx * ROW_SLAB, 128), own),
            scratch_shapes=[pltpu.VMEM((N_RING, tmx * ROW_SLAB, 128), F32), pltpu.SemaphoreType.DMA((N_RING,))]),
        compiler_params=pltpu.CompilerParams(dimension_semantics=("arbitrary",), vmem_limit_bytes=VMEM_LIMIT),
        name="moe_experts",
    )(tile_e, tile_ok, xs, wg, wu, wd)


def _combine_kernel(h_ref, gw_ref, pos_ref, npos_ref, g2_ref, b2_ref, os_ref, y_ref, stage_ref, sems,
                    *, td, n_steps):
    s = pl.program_id(0)

    def gather(idx_ref, half, start):
        for t in range(td):
            for k in range(2):
                cp = pltpu.make_async_copy(_slab(os_ref, idx_ref[0, 0, (half * 2 + k) * td + t]),
                                           _slab(stage_ref.at[half, k], t), sems.at[half])
                if start:
                    cp.start(priority=k)
                else:
                    cp.wait()

    def finish(half):
        rows = slice(half * td, (half + 1) * td)
        gw = gw_ref[rows, :]
        moe = (gw[:, 0:1] * _from_slabs(stage_ref.at[half, 0], td)
               + gw[:, 1:2] * _from_slabs(stage_ref.at[half, 1], td))
        h = jnp.concatenate([h_ref[pl.ds(half * td * ROW_SLAB + c, td, stride=ROW_SLAB), :]
                             for c in range(ROW_SLAB)], axis=1)
        y_ref[rows, :] = _layer_norm(ALPHA * h + moe, g2_ref[...], b2_ref[...])

    @pl.when(s == 0)
    def _prologue():
        gather(pos_ref, 0, True)

    gather(pos_ref, 1, True)
    gather(pos_ref, 0, False)
    finish(0)

    @pl.when(s + 1 < n_steps)
    def _next_step_first_tile():
        gather(npos_ref, 0, True)

    gather(pos_ref, 1, False)
    finish(1)


def _combine(h, gw, pos, g2, b2, os, td, first_token, n_tokens):
    step = 2 * td
    n_steps = n_tokens // step
    off = first_token // step
    pos4 = jnp.stack([q.reshape(-1, 2, td) for q in pos], axis=2).reshape(-1, 1, 2 * step)
    row = lambda i: (i + off, 0)
    const = lambda i: (0, 0)
    return pl.pallas_call(
        functools.partial(_combine_kernel, td=td, n_steps=n_steps),
        out_shape=jax.ShapeDtypeStruct((n_tokens, D_MODEL), F32),
        grid=(n_steps,),
        in_specs=[pl.BlockSpec((step * ROW_SLAB, 128), row), pl.BlockSpec((step, 128), row),
                  pl.BlockSpec((1, 1, 2 * step), lambda i: (i + off, 0, 0), memory_space=pltpu.SMEM),
                  pl.BlockSpec((1, 1, 2 * step), lambda i: (jnp.minimum(i + 1, n_steps - 1) + off, 0, 0),
                               memory_space=pltpu.SMEM),
                  pl.BlockSpec((1, D_MODEL), const), pl.BlockSpec((1, D_MODEL), const),
                  pl.BlockSpec(memory_space=pl.ANY)],
        out_specs=pl.BlockSpec((step, D_MODEL), lambda i: (i, 0)),
        scratch_shapes=[pltpu.VMEM((2, 2, td * ROW_SLAB, 128), F32), pltpu.SemaphoreType.DMA((2,))],
        compiler_params=pltpu.CompilerParams(dimension_semantics=("arbitrary",), vmem_limit_bytes=VMEM_LIMIT),
        name="moe_combine",
    )(h, gw, pos4, pos4, g2, b2, os)


def _route_plan(sel, cnt, t, tmx, td):
    i32 = jnp.int32
    counts = cnt[0, :N_EXPERTS]
    padded = ((counts + tmx - 1) // tmx) * tmx
    ex = jnp.arange(N_EXPERTS, dtype=i32)
    ends = jnp.sum(jnp.where(ex[None, :] <= ex[:, None], padded[None, :], 0), axis=1).astype(i32)
    offs = ends - padded
    pos = tuple((jnp.sum(jnp.where(sel[:, k, None] == ex, offs, 0), axis=-1) + sel[:, 2 + k]).astype(i32)
                for k in range(2))
    nt = 2 * t // tmx + N_EXPERTS
    n_used = ends[-1] // tmx
    tile = jnp.arange(nt, dtype=i32)
    tile_idx = jnp.minimum(tile, jnp.maximum(n_used - 1, 0))
    tile_e = jnp.minimum(jnp.sum((ends[None, :] <= (tile_idx * tmx)[:, None]).astype(i32), axis=1), N_EXPERTS - 1)
    tile_ok = (tile < n_used).astype(i32)
    pieces = tmx // min(td, tmx)
    tail = jnp.stack([n_used * pieces, (nt - n_used) * pieces]).astype(i32)
    return (pos, tile_e, tile_ok, (offs + counts).astype(i32), (padded - counts).astype(i32), tail, nt * tmx)


def _tile(t, want):
    tm = min(want, t)
    while t % tm:
        tm //= 2
    return tm


def _prep_weights(w_in, conv_w, a_log, dt_bias, gdn_norm_w, pool_w, pool_scale, w_out, ln1_g, ln1_b,
                  w_rg, b_rg, w_re, b_re, w_gate, w_up, w_down, ln2_g, ln2_b):
    col_b = 4 * D_A
    col_p = 4 * D_A + 2 * N_HEADS
    w_cat = jnp.concatenate([w_in[:, :col_b], w_in[:, col_p:], w_in[:, col_b:col_p],
                             jnp.zeros((D_MODEL, 128 - 2 * N_HEADS), w_in.dtype)], axis=1).astype(BF16)
    lane_pad = lambda v, off: jnp.zeros((1, 128), F32).at[0, off:off + v.shape[0]].set(v.astype(F32))
    w_r = jnp.concatenate([w_re, w_rg, jnp.zeros((D_MODEL, 128 - N_EXPERTS - N_GROUPS), F32)], axis=1)
    wrh = w_r.astype(BF16)
    wrl = (w_r - wrh.astype(F32)).astype(BF16)
    b_r = jnp.zeros((1, 128), F32).at[0, :N_EXPERTS].set(b_re).at[0, N_EXPERTS:N_EXPERTS + N_GROUPS].set(b_rg)
    return dict(
        w_cat=w_cat, conv_w=conv_w, arow=lane_pad(a_log, LANE_A), dtrow=lane_pad(dt_bias, LANE_A),
        normw=gdn_norm_w.reshape(1, HEAD_DIM), poolw=pool_w.astype(BF16), pscale=pool_scale.reshape(1, D_B),
        wout=w_out.astype(BF16), g1=ln1_g.reshape(1, D_MODEL), b1=ln1_b.reshape(1, D_MODEL),
        wrh=wrh, wrl=wrl, br=b_r,
        wg=w_gate.reshape(N_EXPERTS, D_MODEL, D_EXPERT), wu=w_up.reshape(N_EXPERTS, D_MODEL, D_EXPERT),
        wd=w_down.reshape(N_EXPERTS, D_EXPERT, D_MODEL),
        g2=ln2_g.reshape(1, D_MODEL), b2=ln2_b.reshape(1, D_MODEL))


def _post_mixer(mix_p, x_p, mix_s, x_s, p):
    tp, ts = x_p.shape[0], x_s.shape[0]
    t = tp + ts
    tm = math.gcd(_tile(tp, 1024), _tile(ts, 1024))
    h, sel, gw, cnt = _outproj_router(mix_p, x_p, mix_s, x_s, p["wout"], p["g1"], p["b1"], p["wrh"], p["wrl"],
                                      p["br"], tm)
    td = math.gcd(_tile(tp, ROUTE_TILE), _tile(ts, ROUTE_TILE))
    pos, tile_e, tile_ok, pad_start, pad_cnt, tail, n_rows = _route_plan(sel, cnt, t, EXPERT_TILE, td)
    pos3 = jnp.stack([q.reshape(t // td, td) for q in pos], axis=1).reshape(t // td, 1, 2 * td)
    xs = _dispatch(h, pos3, pad_start, pad_cnt, tail, n_rows, td, EXPERT_TILE)
    os = _experts(xs, tile_e, tile_ok, p["wg"], p["wu"], p["wd"], EXPERT_TILE)
    tc = td if (tp // td) % 2 == 0 and (ts // td) % 2 == 0 else td // 2
    y_p = _combine(h, gw, pos, p["g2"], p["b2"], os, tc, 0, tp)
    y_s = _combine(h, gw, pos, p["g2"], p["b2"], os, tc, tp, ts)
    return y_p, y_s


def _mix_prompt(x, p, lb=256):
    b, seq, _ = x.shape
    x2d = x.reshape(b * seq, D_MODEL)
    proj = _in_proj(x2d, p["w_cat"], _tile(b * seq, 1024)).reshape(b, seq, C_TOT)
    mix, s_fin = _mixer_prompt(proj, p["conv_w"], p["arow"], p["dtrow"], p["normw"], p["poolw"], p["pscale"],
                               min(lb, seq), 2 if b % 2 == 0 else 1)
    conv_new = proj[:, seq - (CONV_W - 1):, 0:C_QKV]
    pool_new = proj[:, seq - POOL_BUF:, C_P:C_P + D_B]
    return x2d, mix.reshape(b * seq, D_MODEL), s_fin, conv_new, pool_new


def _mix_sample(x, s0, conv0, pool0, start, p, ns=16):
    b, seq, _ = x.shape
    x2d = x.reshape(b * seq, D_MODEL)
    proj = _in_proj(x2d, p["w_cat"], _tile(b * seq, 1024))
    cst = jnp.pad(conv0, ((0, 0), (seq - (CONV_W - 1), 0), (0, 0))).reshape(b * seq, C_QKV)
    pst = jnp.pad(pool0, ((0, 0), (1, 0), (0, 0))).reshape(b * 16, D_B)
    mix, s_new = _mixer_sample(proj, cst, pst, s0, p["conv_w"], p["arow"], p["dtrow"], p["normw"], p["poolw"],
                               p["pscale"], min(ns, b), seq, start)
    proj3 = proj.reshape(b, seq, C_TOT)
    conv_new = proj3[:, seq - (CONV_W - 1):, 0:C_QKV]
    pool_new = jnp.concatenate([pool0[:, seq:, :], proj3[:, :, C_P:C_P + D_B]], axis=1)
    return x2d, mix, s_new, conv_new, pool_new


def _layer(x_prompt, x_sample, s0, conv0, pool0, start, p):
    xp2d, mix_p, dp, cp, pp = _mix_prompt(x_prompt, p)
    xs2d, mix_s, ds, cs, ps = _mix_sample(x_sample, s0, conv0, pool0, start, p)
    y_p, y_s = _post_mixer(mix_p, xp2d, mix_s, xs2d, p)
    return y_p.reshape(x_prompt.shape), y_s.reshape(x_sample.shape), (dp, cp, pp), (ds, cs, ps)


def kernel(x_prompt, x_sample, state_delta, state_conv, state_pool, w_in, conv_w, a_log, dt_bias, gdn_norm_w,
           pool_w, pool_scale, w_out, ln1_g, ln1_b, w_rg, b_rg, w_re, b_re, w_gate, w_up, w_down, ln2_g, ln2_b):
    depth = w_in.shape[0]
    past_len = 16384
    yp, ys = x_prompt, x_sample
    outs = [[] for _ in range(6)]
    for l in range(depth):
        p = _prep_weights(w_in[l], conv_w[l], a_log[l], dt_bias[l], gdn_norm_w[l], pool_w[l], pool_scale[l],
                          w_out[l], ln1_g[l], ln1_b[l], w_rg[l], b_rg[l], w_re[l], b_re[l], w_gate[l], w_up[l],
                          w_down[l], ln2_g[l], ln2_b[l])
        yp, ys, st_p, st_s = _layer(yp, ys, state_delta[l], state_conv[l], state_pool[l], past_len, p)
        for lst, v in zip(outs, st_p + st_s):
            lst.append(v)
    return (yp, ys) + tuple(jnp.stack(v) for v in outs)
```
